```python
import math
import jax, jax.numpy as jnp
from jax import lax
import numpy as np

D_MODEL = 1024
BATCH = 4
SEQ = 4096
DEPTH = 1

N_HEADS_A = 8
HEAD_DIM_A = 64
QK_WIDTH = N_HEADS_A * 2 * HEAD_DIM_A
V_WIDTH = N_HEADS_A * 2 * HEAD_DIM_A
Q_BLOCK = 128
CONV_WIDTH = 1024
CONV_TAPS = 31
REL_BUCKETS = 32
REL_MAX_DIST = 128
N_GROUPS = 4
EXPERTS_PER_GROUP = 8
N_EXPERTS = N_GROUPS * EXPERTS_PER_GROUP
TOP_K = 2
D_EXPERT = 512
ROW_BLOCK = 128
IN_COLS = QK_WIDTH + QK_WIDTH + V_WIDTH + 2 * CONV_WIDTH + 2 * D_MODEL
SPLITS = (QK_WIDTH, 2 * QK_WIDTH, 2 * QK_WIDTH + V_WIDTH, 2 * QK_WIDTH + V_WIDTH + 2 * CONV_WIDTH)
DN_ALPHA = (2.0 * DEPTH) ** 0.25
DN_BETA = (8.0 * DEPTH) ** -0.25
LN_EPS = 1e-5
NEG_INF = -1e30

kernel_name = "hybrid_diffattn_conformer_hmoe_deepnorm"


def layer_norm(x, g, b):
    xf = x.astype(jnp.float32)
    mu = jnp.mean(xf, axis=-1, keepdims=True)
    var = jnp.mean(jnp.square(xf - mu), axis=-1, keepdims=True)
    y = (xf - mu) * lax.rsqrt(var + LN_EPS) * g.astype(jnp.float32) + b.astype(jnp.float32)
    return y.astype(x.dtype)


def rms_norm(x, g):
    xf = x.astype(jnp.float32)
    y = xf * lax.rsqrt(jnp.mean(jnp.square(xf), axis=-1, keepdims=True) + LN_EPS) * g.astype(jnp.float32)
    return y.astype(x.dtype)


def rel_bucket(dist):
    max_exact = REL_BUCKETS // 2
    d = jnp.maximum(dist, 1).astype(jnp.float32)
    large = max_exact + (jnp.log(d / max_exact) / math.log(REL_MAX_DIST / max_exact)
                         * (REL_BUCKETS - max_exact)).astype(jnp.int32)
    large = jnp.minimum(large, REL_BUCKETS - 1)
    return jnp.where(dist < max_exact, dist, large)


def diff_attention(q, k, v, rel_table, lam):
    B, S = q.shape[0], q.shape[1]
    n_blk = S // Q_BLOCK
    scale = HEAD_DIM_A ** -0.5
    k_pos = jnp.arange(S, dtype=jnp.int32)
    q_blocks = (q * scale).reshape(B, n_blk, Q_BLOCK, N_HEADS_A, 2, HEAD_DIM_A).transpose(1, 0, 2, 3, 4, 5)
    starts = jnp.arange(n_blk, dtype=jnp.int32) * Q_BLOCK

    def one_block(args):
        q_blk, start = args
        q_pos = start + jnp.arange(Q_BLOCK, dtype=jnp.int32)
        rel = q_pos[:, None] - k_pos[None, :]
        bias = rel_table[rel_bucket(jnp.maximum(rel, 0))]
        bias = bias.transpose(2, 0, 1).astype(jnp.float32)
        logits = jnp.einsum('bqhmd,bkhmd->bhmqk', q_blk, k).astype(jnp.float32) + bias[:, None]
        logits = jnp.where(rel >= 0, logits, NEG_INF)
        p = jax.nn.softmax(logits, axis=-1)
        a = p[:, :, 0] - lam * p[:, :, 1]
        return jnp.einsum('bhqk,bkhe->bqhe', a.astype(v.dtype), v)

    out = lax.map(one_block, (q_blocks, starts))
    return out.transpose(1, 0, 2, 3, 4).reshape(B, S, N_HEADS_A, 2 * HEAD_DIM_A)


def conformer_conv(u, w_dw, b_dw, g_ln, b_ln, w_pw, b_pw):
    a, gate = jnp.split(u, 2, axis=-1)
    h = a * jax.nn.sigmoid(gate)
    h = lax.conv_general_dilated(h, w_dw[:, None, :], window_strides=(1,),
                                 padding=[(CONV_TAPS - 1, 0)],
                                 dimension_numbers=('NWC', 'WIO', 'NWC'),
                                 feature_group_count=CONV_WIDTH) + b_dw
    h = jax.nn.silu(layer_norm(h, g_ln, b_ln))
    return h @ w_pw + b_pw


def hier_moe(x, w_rg, b_rg, w_re, b_re, w_gate, w_up, w_down):
    B, S, D = x.shape
    T = B * S
    A = T * TOP_K
    xf = x.reshape(T, D)
    g_logits = (xf @ w_rg + b_rg).astype(jnp.float32)
    g_prob = jax.nn.softmax(g_logits, axis=-1)
    _, g_idx = lax.top_k(g_logits, 1)
    p_g = jnp.take_along_axis(g_prob, g_idx, axis=-1)
    e_all = (xf @ w_re + b_re).reshape(T, N_GROUPS, EXPERTS_PER_GROUP)
    e_logits = jnp.take_along_axis(e_all, g_idx[:, :, None], axis=1)[:, 0].astype(jnp.float32)
    top_v, top_i = lax.top_k(e_logits, TOP_K)
    gate = p_g * jax.nn.softmax(top_v, axis=-1)
    expert = g_idx * EXPERTS_PER_GROUP + top_i

    flat_e = expert.reshape(A)
    flat_tok = jnp.repeat(jnp.arange(T, dtype=jnp.int32), TOP_K)
    flat_w = gate.reshape(A)
    order = jnp.argsort(flat_e)
    sorted_e = flat_e[order]
    counts = jnp.zeros((N_EXPERTS,), jnp.int32).at[flat_e].add(1)
    starts = jnp.cumsum(counts) - counts
    padded = (counts + ROW_BLOCK - 1) // ROW_BLOCK * ROW_BLOCK
    pad_ends = jnp.cumsum(padded)
    pad_starts = pad_ends - padded
    dest = pad_starts[sorted_e] + (jnp.arange(A, dtype=jnp.int32) - starts[sorted_e])
    P = A + N_EXPERTS * ROW_BLOCK
    row_tok = jnp.zeros((P,), jnp.int32).at[dest].set(flat_tok[order])
    row_w = jnp.zeros((P,), jnp.float32).at[dest].set(flat_w[order])
    n_blocks = P // ROW_BLOCK
    blk_start = jnp.arange(n_blocks, dtype=jnp.int32) * ROW_BLOCK
    blk_e = jnp.minimum(jnp.sum(blk_start[:, None] >= pad_ends[None, :], axis=1), N_EXPERTS - 1)
    xs = xf[row_tok].reshape(n_blocks, ROW_BLOCK, D)

    def run_block(args):
        xb, e = args
        h = jax.nn.silu(xb @ w_gate[e]) * (xb @ w_up[e])
        return h @ w_down[e]

    ys = lax.map(run_block, (xs, blk_e)).reshape(P, D)
    out = jax.ops.segment_sum(ys * row_w[:, None].astype(ys.dtype), row_tok, num_segments=T)
    return out.reshape(B, S, D)


def setup_inputs(seed: int = 0) -> dict:
    key = jax.random.key(seed)
    ks = jax.random.split(key, 24)
    n = lambda k, shape, s: jax.random.normal(k, shape, jnp.float32) * s
    L, D, C = DEPTH, D_MODEL, CONV_WIDTH
    return {
        "x": n(ks[0], (BATCH, SEQ, D), 1.0),
        "w_in": n(ks[1], (L, D, IN_COLS), D ** -0.5),
        "b_in": n(ks[2], (L, IN_COLS), 0.02),
        "diff_lambda": n(ks[3], (L, 4, HEAD_DIM_A), 0.1),
        "head_norm_g": 1.0 + n(ks[4], (L, 2 * HEAD_DIM_A), 0.02),
        "w_o_attn": n(ks[5], (L, V_WIDTH, D), V_WIDTH ** -0.5 * DN_BETA),
        "rel_bias": n(ks[6], (REL_BUCKETS, N_HEADS_A), 0.5),
        "conv_w": n(ks[7], (L, CONV_TAPS, C), CONV_TAPS ** -0.5),
        "conv_b": n(ks[8], (L, C), 0.02),
        "conv_ln_g": 1.0 + n(ks[9], (L, C), 0.02),
        "conv_ln_b": n(ks[10], (L, C), 0.02),
        "w_conv_out": n(ks[11], (L, C, D), C ** -0.5 * DN_BETA),
        "b_conv_out": n(ks[12], (L, D), 0.02),
        "w_out": n(ks[13], (L, D, D), D ** -0.5 * DN_BETA),
        "ln1_g": 1.0 + n(ks[14], (L, D), 0.02),
        "ln1_b": n(ks[15], (L, D), 0.02),
        "router_g_w": n(ks[16], (L, D, N_GROUPS), D ** -0.5),
        "router_g_b": n(ks[17], (L, N_GROUPS), 0.01),
        "router_e_w": n(ks[18], (L, D, N_EXPERTS), D ** -0.5),
        "router_e_b": n(ks[19], (L, N_EXPERTS), 0.01),
        "expert_w_gate": n(ks[20], (L, N_EXPERTS, D, D_EXPERT), D ** -0.5),
        "expert_w_up": n(ks[21], (L, N_EXPERTS, D, D_EXPERT), D ** -0.5),
        "expert_w_down": n(ks[22], (L, N_EXPERTS, D_EXPERT, D), D_EXPERT ** -0.5 * DN_BETA),
        "ln2_g": 1.0 + n(ks[23], (L, D), 0.02),
        "ln2_b": n(jax.random.fold_in(key, 99), (L, D), 0.02),
    }


def reference(x, w_in, b_in, diff_lambda, head_norm_g, w_o_attn, rel_bias, conv_w, conv_b,
              conv_ln_g, conv_ln_b, w_conv_out, b_conv_out, w_out, ln1_g, ln1_b,
              router_g_w, router_g_b, router_e_w, router_e_b, expert_w_gate, expert_w_up,
              expert_w_down, ln2_g, ln2_b):
    B, S, D = x.shape
    for li in range(DEPTH):
        lam_init = 0.8 - 0.6 * math.exp(-0.3 * li)
        proj = x @ w_in[li] + b_in[li]
        q, k, v, u, gates = jnp.split(proj, SPLITS, axis=-1)
        q = q.reshape(B, S, N_HEADS_A, 2, HEAD_DIM_A)
        k = k.reshape(B, S, N_HEADS_A, 2, HEAD_DIM_A)
        v = v.reshape(B, S, N_HEADS_A, 2 * HEAD_DIM_A)
        lv = diff_lambda[li].astype(jnp.float32)
        lam = jnp.exp(jnp.sum(lv[0] * lv[1])) - jnp.exp(jnp.sum(lv[2] * lv[3])) + lam_init
        o = diff_attention(q, k, v, rel_bias, lam)
        o = rms_norm(o, head_norm_g[li]) * (1.0 - lam_init)
        y_a = o.reshape(B, S, V_WIDTH) @ w_o_attn[li]
        y_b = conformer_conv(u, conv_w[li], conv_b[li], conv_ln_g[li], conv_ln_b[li],
                             w_conv_out[li], b_conv_out[li])
        g_a, g_b = jnp.split(jax.nn.sigmoid(gates), 2, axis=-1)
        mixed = (g_a * y_a + g_b * y_b) @ w_out[li]
        x = layer_norm(DN_ALPHA * x + mixed, ln1_g[li], ln1_b[li])
        ffn = hier_moe(x, router_g_w[li], router_g_b[li], router_e_w[li], router_e_b[li],
                       expert_w_gate[li], expert_w_up[li], expert_w_down[li])
        x = layer_norm(DN_ALPHA * x + ffn, ln2_g[li], ln2_b[li])
    return x
```

```python
import functools
import math

import jax
import jax.numpy as jnp
from jax import lax
from jax.experimental import pallas as pl
from jax.experimental.pallas import tpu as pltpu

F32 = jnp.float32
BF16 = jnp.bfloat16
U32 = jnp.uint32
I32 = jnp.int32

N_HEADS = 8
HEAD_DIM = 64
CONV_TAPS = 31
REL_BUCKETS = 32
REL_MAX_DIST = 128
N_GROUPS = 4
EXPERTS_PER_GROUP = 8
N_EXPERTS = N_GROUPS * EXPERTS_PER_GROUP
TOP_K = 2
DEPTH = 1
DN_ALPHA = (2.0 * DEPTH) ** 0.25
LN_EPS = 1e-5
NEG_INF = -1e30
LOG2E = 1.4426950408889634

LANES = 128
VMEM_LIMIT = 56 * 1024 * 1024
CONV_HALO = 32


def _cparams(n_axes):
    return pltpu.CompilerParams(dimension_semantics=("arbitrary",) * n_axes,
                                vmem_limit_bytes=VMEM_LIMIT)


def _sigmoid(x):
    return 1.0 / (1.0 + jnp.exp(-x))


def _layer_norm(z, g, b):
    mu = jnp.mean(z, axis=-1, keepdims=True)
    zc = z - mu
    var = jnp.mean(zc * zc, axis=-1, keepdims=True)
    return zc * lax.rsqrt(var + LN_EPS) * g + b


def _pack_halves(y):
    n = y.shape[1] // 2
    bits = pltpu.bitcast(y.astype(BF16).astype(F32), U32)
    return (bits[:, :n] >> 16) | (bits[:, n:] & jnp.uint32(0xFFFF0000))


def _unpack_halves(w):
    lo = pltpu.bitcast(w << 16, F32)
    hi = pltpu.bitcast(w & jnp.uint32(0xFFFF0000), F32)
    return jnp.concatenate([lo, hi], axis=1)


def _proj_kernel(x_ref, w_ref, b_ref, q_ref, k_ref, v_ref, h_ref, ga_ref, gb_ref, *, width, chunk, q_scale):
    xb = x_ref[...].astype(BF16)

    def lin(col0):
        return (jnp.dot(xb, w_ref[:, col0:col0 + chunk], preferred_element_type=F32)
                + b_ref[:, col0:col0 + chunk])

    for c in range(0, width, chunk):
        sl = slice(c, c + chunk)
        q_ref[:, sl] = (lin(c) * q_scale).astype(BF16)
        k_ref[:, sl] = lin(width + c).astype(BF16)
        v_ref[:, sl] = lin(2 * width + c).astype(BF16)
        h_ref[:, sl] = (lin(3 * width + c) * _sigmoid(lin(4 * width + c))).astype(BF16)
        ga_ref[:, sl] = _sigmoid(lin(5 * width + c)).astype(BF16)
        gb_ref[:, sl] = _sigmoid(lin(6 * width + c)).astype(BF16)


def _proj(x2, w_in, b_in, *, width, tm, chunk, q_scale):
    t, d = x2.shape
    n_cols = w_in.shape[1]
    out = jax.ShapeDtypeStruct((t, width), BF16)
    row_spec = pl.BlockSpec((tm, width), lambda i: (i, 0))
    return pl.pallas_call(
        functools.partial(_proj_kernel, width=width, chunk=chunk, q_scale=q_scale),
        grid=(t // tm,),
        in_specs=[pl.BlockSpec((tm, d), lambda i: (i, 0)),
                  pl.BlockSpec((d, n_cols), lambda i: (0, 0), pipeline_mode=pl.Buffered(1)),
                  pl.BlockSpec((1, n_cols), lambda i: (0, 0))],
        out_specs=[row_spec] * 6,
        out_shape=[out] * 6,
        compiler_params=_cparams(1),
        name="proj",
    )(x2, w_in, b_in)


def _attn_kernel(tab_ref, bucket_ref, lam_ref, q_ref, k_ref, v_ref, g_ref, o_ref,
                 bias_scr, qz_scr, m_scr, l_scr, acc_scr, *, tq, tk, dh, lam_init):
    h = pl.program_id(0)
    b = pl.program_id(1)
    qi = pl.program_id(2)

    @pl.when((b == 0) & (qi == 0))
    def _build_bias():
        far = tab_ref[REL_BUCKETS - 1, h]
        for t in range(2):
            bk = bucket_ref[t]
            acc = jnp.full((tq, tk), NEG_INF, F32)
            for r in range(REL_BUCKETS):
                acc = jnp.where(bk == r, (tab_ref[r, h] - far) * LOG2E, acc)
            bias_scr[t] = acc

    q = q_ref[0]
    lane = lax.broadcasted_iota(I32, (tq, 2 * dh), 1)
    zero = jnp.zeros_like(q)
    qz_scr[0:tq, :] = jnp.where(lane < dh, q, zero)
    qz_scr[tq:2 * tq, :] = jnp.where(lane >= dh, q, zero)
    m_scr[...] = jnp.full(m_scr.shape, NEG_INF, F32)
    l_scr[...] = jnp.zeros(l_scr.shape, F32)
    acc_scr[...] = jnp.zeros(acc_scr.shape, F32)

    def step(j, bias):
        start = pl.multiple_of(j * tk, tk)
        kj = k_ref[0, pl.ds(start, tk), :]
        vj = v_ref[0, pl.ds(start, tk), :]
        s = lax.dot_general(qz_scr[...], kj, (((1,), (1,)), ((), ())), preferred_element_type=F32)
        if bias is not None:
            s = s + jnp.concatenate([bias, bias], axis=0)
        m_prev = m_scr[...]
        m_new = jnp.maximum(m_prev, jnp.max(s, axis=-1, keepdims=True))
        alpha = jnp.exp2(m_prev - m_new)
        p = jnp.exp2(s - m_new)
        l_scr[...] = alpha * l_scr[...] + jnp.sum(p, axis=-1, keepdims=True)
        acc_scr[...] = alpha * acc_scr[...] + jnp.dot(p.astype(BF16), vj, preferred_element_type=F32)
        m_scr[...] = m_new

    def far_body(j, carry):
        step(j, None)
        return carry

    lax.fori_loop(0, jnp.maximum(qi - 1, 0), far_body, 0)

    @pl.when(qi >= 1)
    def _prev_tile():
        step(qi - 1, bias_scr[1])

    step(qi, bias_scr[0])

    lv = lam_ref[...]
    lam = (jnp.exp(jnp.sum(lv[0:1] * lv[1:2], axis=-1, keepdims=True))
           - jnp.exp(jnp.sum(lv[2:3] * lv[3:4], axis=-1, keepdims=True)) + lam_init)
    acc = acc_scr[...]
    l = l_scr[...]
    o = acc[0:tq] / l[0:tq] - lam * (acc[tq:2 * tq] / l[tq:2 * tq])
    ms = jnp.mean(o * o, axis=-1, keepdims=True)
    y = o * lax.rsqrt(ms + LN_EPS) * g_ref[...] * (1.0 - lam_init)
    o_ref[0] = y.astype(o_ref.dtype)


def _rel_bucket(dist):
    max_exact = REL_BUCKETS // 2
    d = jnp.maximum(dist, 1).astype(F32)
    large = max_exact + (jnp.log(d / max_exact) / math.log(REL_MAX_DIST / max_exact)
                         * (REL_BUCKETS - max_exact)).astype(I32)
    large = jnp.minimum(large, REL_BUCKETS - 1)
    return jnp.where(dist < max_exact, dist, large)


def _near_buckets(tq, tk):
    rel = jnp.arange(tq, dtype=I32)[:, None] - jnp.arange(tk, dtype=I32)[None, :]
    diag = jnp.where(rel >= 0, _rel_bucket(jnp.maximum(rel, 0)), -1)
    prev = _rel_bucket(rel + tk)
    return jnp.stack([diag, prev]).astype(I32)


def _attention(q, k, v, rel_bias, diff_lambda, head_norm_g, *, tq, lam_init):
    bsz, s, width = q.shape
    hw = width // N_HEADS
    tk = tq
    assert hw == 2 * HEAD_DIM and hw % LANES == 0 and s % tq == 0
    assert tk + 1 >= REL_MAX_DIST
    kernel = functools.partial(_attn_kernel, tq=tq, tk=tk, dh=HEAD_DIM, lam_init=lam_init)
    return pl.pallas_call(
        kernel,
        grid=(N_HEADS, bsz, s // tq),
        in_specs=[pl.BlockSpec(memory_space=pltpu.SMEM),
                  pl.BlockSpec((2, tq, tk), lambda h, b, i: (0, 0, 0)),
                  pl.BlockSpec((4, HEAD_DIM), lambda h, b, i: (0, 0)),
                  pl.BlockSpec((1, tq, hw), lambda h, b, i: (b, i, h)),
                  pl.BlockSpec((1, s, hw), lambda h, b, i: (b, 0, h)),
                  pl.BlockSpec((1, s, hw), lambda h, b, i: (b, 0, h)),
                  pl.BlockSpec((1, hw), lambda h, b, i: (0, 0))],
        out_specs=pl.BlockSpec((1, tq, hw), lambda h, b, i: (b, i, h)),
        out_shape=jax.ShapeDtypeStruct((bsz, s, width), BF16),
        scratch_shapes=[pltpu.VMEM((2, tq, tk), F32),
                        pltpu.VMEM((2 * tq, hw), BF16),
                        pltpu.VMEM((2 * tq, 1), F32),
                        pltpu.VMEM((2 * tq, 1), F32),
                        pltpu.VMEM((2 * tq, hw), F32)],
        compiler_params=_cparams(3),
        name="attn",
    )(rel_bias, _near_buckets(tq, tk), diff_lambda, q, k, v, head_norm_g.reshape(1, hw))


def _conv_kernel(h_ref, halo_ref, w_ref, b_ref, g_ref, bb_ref, o_ref, buf, *, ts, rc):
    i = pl.program_id(1)
    halo = halo_ref[0].astype(F32)
    buf[0:CONV_HALO, :] = jnp.where(i > 0, halo, jnp.zeros_like(halo))
    buf[CONV_HALO:CONV_HALO + ts, :] = h_ref[0].astype(F32)
    off = CONV_HALO - (CONV_TAPS - 1)
    for r0 in range(0, ts, rc):
        acc = w_ref[0:1, :] * buf[r0 + off:r0 + off + rc, :]
        for t in range(1, CONV_TAPS):
            acc = acc + w_ref[t:t + 1, :] * buf[r0 + off + t:r0 + off + t + rc, :]
        y = _layer_norm(acc + b_ref[...], g_ref[...], bb_ref[...])
        o_ref[0, r0:r0 + rc, :] = (y * _sigmoid(y)).astype(o_ref.dtype)


def _conv(hglu, conv_w, conv_b, ln_g, ln_b, *, ts, rc):
    bsz, s, c = hglu.shape
    assert s % ts == 0 and ts % CONV_HALO == 0 and CONV_HALO >= CONV_TAPS - 1
    blocks_per_tile = ts // CONV_HALO
    vec = pl.BlockSpec((1, c), lambda b, i: (0, 0))
    return pl.pallas_call(
        functools.partial(_conv_kernel, ts=ts, rc=rc),
        grid=(bsz, s // ts),
        in_specs=[pl.BlockSpec((1, ts, c), lambda b, i: (b, i, 0)),
                  pl.BlockSpec((1, CONV_HALO, c),
                               lambda b, i: (b, jnp.maximum(i * blocks_per_tile - 1, 0), 0)),
                  pl.BlockSpec((CONV_TAPS, c), lambda b, i: (0, 0)),
                  vec, vec, vec],
        out_specs=pl.BlockSpec((1, ts, c), lambda b, i: (b, i, 0)),
        out_shape=jax.ShapeDtypeStruct((bsz, s, c), BF16),
        scratch_shapes=[pltpu.VMEM((CONV_HALO + ts, c), F32)],
        compiler_params=_cparams(2),
        name="conv",
    )(hglu, hglu, conv_w, conv_b.reshape(1, c), ln_g.reshape(1, c), ln_b.reshape(1, c))


def _mix_kernel(o_ref, hc_ref, ga_ref, gb_ref, x_ref, woa_ref, wco_ref, bco_ref, wout_ref,
                g1_ref, b1_ref, wr_ref, br_ref, tri_ref,
                x1_ref, x1p_ref, mi_ref, mf_ref, cnt_ref, carry_scr, *, tm):
    i = pl.program_id(0)

    @pl.when(i == 0)
    def _init():
        carry_scr[...] = jnp.zeros(carry_scr.shape, F32)

    y_a = jnp.dot(o_ref[...], woa_ref[...], preferred_element_type=F32)
    y_b = jnp.dot(hc_ref[...], wco_ref[...], preferred_element_type=F32) + bco_ref[...]
    merged = ga_ref[...].astype(F32) * y_a + gb_ref[...].astype(F32) * y_b
    mixed = jnp.dot(merged.astype(BF16), wout_ref[...], preferred_element_type=F32)
    x1 = _layer_norm(DN_ALPHA * x_ref[...] + mixed, g1_ref[...], b1_ref[...])
    x1_ref[...] = x1
    x1p_ref[...] = _pack_halves(x1)

    logits = jnp.dot(x1.astype(BF16), wr_ref[...], preferred_element_type=F32) + br_ref[...]
    lane = lax.broadcasted_iota(I32, (tm, LANES), 1)
    g_lane = lane < N_GROUPS
    gl = jnp.where(g_lane, logits, NEG_INF)
    g_max = jnp.max(gl, axis=-1, keepdims=True)
    g_idx = jnp.min(jnp.where(gl == g_max, lane, LANES), axis=-1, keepdims=True)
    p_g = 1.0 / jnp.sum(jnp.where(g_lane, jnp.exp(gl - g_max), 0.0), axis=-1, keepdims=True)
    lo = N_GROUPS + g_idx * EXPERTS_PER_GROUP
    el = jnp.where(lane >= lo, jnp.where(lane < lo + EXPERTS_PER_GROUP, logits, NEG_INF), NEG_INF)
    v1 = jnp.max(el, axis=-1, keepdims=True)
    i1 = jnp.min(jnp.where(el == v1, lane, LANES), axis=-1, keepdims=True)
    el2 = jnp.where(lane == i1, NEG_INF, el)
    v2 = jnp.max(el2, axis=-1, keepdims=True)
    i2 = jnp.min(jnp.where(lane == i1, LANES, jnp.where(el2 == v2, lane, LANES)), axis=-1, keepdims=True)
    ex = jnp.exp(v2 - v1)
    w1 = 1.0 / (1.0 + ex)
    gate1 = p_g * w1
    gate2 = p_g * (ex * w1)
    e1 = i1 - N_GROUPS
    e2 = i2 - N_GROUPS

    carry = carry_scr[...]
    tri = tri_ref[...]
    ranks = []
    for e in (e1, e2):
        hit = lane == e
        oh = jnp.where(hit, 1.0, 0.0)
        before = jnp.dot(tri, oh.astype(BF16), preferred_element_type=F32) + carry
        ranks.append(jnp.sum(jnp.where(hit, before, 0.0), axis=-1, keepdims=True).astype(I32))
        carry = carry + jnp.sum(oh, axis=0, keepdims=True)
    carry_scr[...] = carry
    cnt_ref[...] = carry

    mi_ref[...] = jnp.where(lane == 0, e1, jnp.where(lane == 1, e2,
                            jnp.where(lane == 2, ranks[0], jnp.where(lane == 3, ranks[1], 0))))
    mf_ref[...] = jnp.where(lane == 0, gate1, jnp.where(lane == 1, gate2, 0.0))


def _mix(o_n, hc, ga, gb, x2, w_oa, w_co, b_co, w_out, ln_g, ln_b, w_r, b_r, *, tm):
    t, d = x2.shape
    row = lambda w: pl.BlockSpec((tm, w), lambda i: (i, 0))
    full = lambda a: pl.BlockSpec(a.shape, lambda i: (0,) * a.ndim, pipeline_mode=pl.Buffered(1))
    tri = (jnp.arange(tm)[:, None] > jnp.arange(tm)[None, :]).astype(BF16)
    b_co, ln_g, ln_b = b_co.reshape(1, d), ln_g.reshape(1, d), ln_b.reshape(1, d)
    return pl.pallas_call(
        functools.partial(_mix_kernel, tm=tm),
        grid=(t // tm,),
        in_specs=[row(d), row(d), row(d), row(d), row(d),
                  full(w_oa), full(w_co), full(b_co), full(w_out), full(ln_g), full(ln_b),
                  full(w_r), full(b_r), full(tri)],
        out_specs=[row(d), row(d // 2), row(LANES), row(LANES),
                   pl.BlockSpec((1, LANES), lambda i: (0, 0))],
        out_shape=[jax.ShapeDtypeStruct((t, d), F32),
                   jax.ShapeDtypeStruct((t, d // 2), U32),
                   jax.ShapeDtypeStruct((t, LANES), I32),
                   jax.ShapeDtypeStruct((t, LANES), F32),
                   jax.ShapeDtypeStruct((1, LANES), F32)],
        scratch_shapes=[pltpu.VMEM((1, LANES), F32)],
        compiler_params=_cparams(1),
        name="mix",
    )(o_n, hc, ga, gb, x2, w_oa, w_co, b_co, w_out, ln_g, ln_b, w_r, b_r, tri)


def _row_copy(src, src_row, dst, dst_row, sem):
    return pltpu.make_async_copy(src.at[pl.ds(src_row, 1), :], dst.at[pl.ds(dst_row, 1), :], sem)


def _dispatch_kernel(dest_ref, x_ref, xs_in_ref, xs_ref, sem, *, tm):
    del xs_in_ref
    base = pl.program_id(0) * tm

    def issue(r, carry):
        for c in range(TOP_K):
            _row_copy(x_ref, r, xs_ref, dest_ref[TOP_K * (base + r) + c], sem).start()
        return carry

    lax.fori_loop(0, tm, issue, 0)
    for _ in range(TOP_K):
        pltpu.make_async_copy(x_ref, xs_ref.at[pl.ds(0, tm), :], sem).wait()


def _dispatch(dest_flat, x1p, n_rows, *, tm):
    t, w = x1p.shape
    return pl.pallas_call(
        functools.partial(_dispatch_kernel, tm=tm),
        grid_spec=pltpu.PrefetchScalarGridSpec(
            num_scalar_prefetch=1,
            grid=(t // tm,),
            in_specs=[pl.BlockSpec((tm, w), lambda i, dest: (i, 0)),
                      pl.BlockSpec(memory_space=pl.ANY)],
            out_specs=pl.BlockSpec(memory_space=pl.ANY),
            scratch_shapes=[pltpu.SemaphoreType.DMA(())]),
        out_shape=jax.ShapeDtypeStruct((n_rows, w), U32),
        input_output_aliases={2: 0},
        compiler_params=_cparams(1),
        name="dispatch",
    )(dest_flat, x1p, jnp.zeros((n_rows, w), U32))


def _expert_kernel(blk_e_ref, n_used_ref, xs_ref, wg_ref, wu_ref, wd_ref, y_ref):
    del blk_e_ref
    active = pl.program_id(0) < n_used_ref[0]

    @pl.when(jnp.logical_not(active))
    def _unused_block():
        y_ref[...] = jnp.zeros(y_ref.shape, y_ref.dtype)

    @pl.when(active)
    def _():
        xb = _unpack_halves(xs_ref[...]).astype(BF16)
        g = jnp.dot(xb, wg_ref[0], preferred_element_type=F32)
        u = jnp.dot(xb, wu_ref[0], preferred_element_type=F32)
        hid = (g * _sigmoid(g) * u).astype(BF16)
        y_ref[...] = _pack_halves(jnp.dot(hid, wd_ref[0], preferred_element_type=F32))


def _experts(blk_e, n_used, xs, w_gate, w_up, w_down, *, tm):
    n_rows, w = xs.shape
    _, d, de = w_gate.shape
    blk = lambda i, blk_e, n_used: jnp.minimum(i, n_used[0] - 1)
    row_map = lambda i, blk_e, n_used: (blk(i, blk_e, n_used), 0)
    w_map = lambda i, blk_e, n_used: (blk_e[blk(i, blk_e, n_used)], 0, 0)
    return pl.pallas_call(
        _expert_kernel,
        grid_spec=pltpu.PrefetchScalarGridSpec(
            num_scalar_prefetch=2,
            grid=(n_rows // tm,),
            in_specs=[pl.BlockSpec((tm, w), row_map),
                      pl.BlockSpec((1, d, de), w_map),
                      pl.BlockSpec((1, d, de), w_map),
                      pl.BlockSpec((1, de, d), w_map)],
            out_specs=pl.BlockSpec((tm, w), lambda i, blk_e, n_used: (i, 0))),
        out_shape=jax.ShapeDtypeStruct((n_rows, w), U32),
        compiler_params=_cparams(1),
        name="experts",
    )(blk_e, n_used, xs, w_gate, w_up, w_down)


def _combine_kernel(dest_ref, x1_ref, mf_ref, g_ref, b_ref, yp_ref, o_ref, buf, sems, *, tm, n_tiles):
    i = pl.program_id(0)

    def issue(tile, slot):
        base = tile * tm

        def body(r, carry):
            for c in range(TOP_K):
                _row_copy(yp_ref, dest_ref[TOP_K * (base + r) + c], buf.at[slot, c], r, sems.at[slot]).start()
            return carry

        lax.fori_loop(0, tm, body, 0)

    @pl.when(i == 0)
    def _first():
        issue(0, 0)

    @pl.when(i + 1 < n_tiles)
    def _next():
        issue(i + 1, (i + 1) % 2)

    slot = i % 2
    for c in range(TOP_K):
        pltpu.make_async_copy(yp_ref.at[pl.ds(0, tm), :], buf.at[slot, c], sems.at[slot]).wait()

    gates = mf_ref[...]
    ffn = (gates[:, 0:1] * _unpack_halves(buf[slot, 0]) + gates[:, 1:2] * _unpack_halves(buf[slot, 1]))
    o_ref[...] = _layer_norm(DN_ALPHA * x1_ref[...] + ffn, g_ref[...], b_ref[...])


def _combine(dest_flat, x1, mf, ln_g, ln_b, yp, *, tm):
    t, d = x1.shape
    n_tiles = t // tm
    vec = pl.BlockSpec((1, d), lambda i, dest: (0, 0))
    return pl.pallas_call(
        functools.partial(_combine_kernel, tm=tm, n_tiles=n_tiles),
        grid_spec=pltpu.PrefetchScalarGridSpec(
            num_scalar_prefetch=1,
            grid=(n_tiles,),
            in_specs=[pl.BlockSpec((tm, d), lambda i, dest: (i, 0)),
                      pl.BlockSpec((tm, LANES), lambda i, dest: (i, 0)),
                      vec, vec,
                      pl.BlockSpec(memory_space=pl.ANY)],
            out_specs=pl.BlockSpec((tm, d), lambda i, dest: (i, 0)),
            scratch_shapes=[pltpu.VMEM((2, TOP_K, tm, d // 2), U32),
                            pltpu.SemaphoreType.DMA((2,))]),
        out_shape=jax.ShapeDtypeStruct((t, d), F32),
        compiler_params=_cparams(1),
        name="combine",
    )(dest_flat, x1, mf, ln_g.reshape(1, d), ln_b.reshape(1, d), yp)


def _tile(n, pref):
    return pref if n % pref == 0 else n


def kernel(x, w_in, b_in, diff_lambda, head_norm_g, w_o_attn, rel_bias, conv_w, conv_b, conv_ln_g,
           conv_ln_b, w_conv_out, b_conv_out, w_out, ln1_g, ln1_b, router_g_w, router_g_b,
           router_e_w, router_e_b, expert_w_gate, expert_w_up, expert_w_down, ln2_g, ln2_b):
    bsz, s, d = x.shape
    t = bsz * s
    width = N_HEADS * 2 * HEAD_DIM
    assert w_in.shape[0] == DEPTH and w_in.shape[2] == 7 * width and width == d
    assert conv_w.shape[2] == width and TOP_K == 2
    tm_moe = _tile(t, 256)

    for li in range(DEPTH):
        lam_init = 0.8 - 0.6 * math.exp(-0.3 * li)
        x2 = x.reshape(t, d)
        q, k, v, hglu, ga, gb = _proj(
            x2, w_in[li].astype(BF16), b_in[li].reshape(1, -1), width=width, tm=_tile(t, 512),
            chunk=_tile(width, 512), q_scale=HEAD_DIM ** -0.5 * LOG2E)
        o_n = _attention(q.reshape(bsz, s, width), k.reshape(bsz, s, width), v.reshape(bsz, s, width),
                         rel_bias, diff_lambda[li], head_norm_g[li], tq=_tile(s, 256), lam_init=lam_init)
        hc = _conv(hglu.reshape(bsz, s, width), conv_w[li], conv_b[li], conv_ln_g[li], conv_ln_b[li],
                   ts=_tile(s, 256), rc=16)

        n_r = N_GROUPS + N_EXPERTS
        w_r = jnp.pad(jnp.concatenate([router_g_w[li], router_e_w[li]], axis=1),
                      ((0, 0), (0, LANES - n_r))).astype(BF16)
        b_r = jnp.pad(jnp.concatenate([router_g_b[li], router_e_b[li]]), (0, LANES - n_r)).reshape(1, LANES)
        x1, x1p, mi, mf, cnt = _mix(
            o_n.reshape(t, width), hc.reshape(t, width), ga, gb, x2,
            w_o_attn[li].astype(BF16), w_conv_out[li].astype(BF16), b_conv_out[li],
            w_out[li].astype(BF16), ln1_g[li], ln1_b[li], w_r, b_r, tm=_tile(t, 512))

        counts = cnt[0, :N_EXPERTS].astype(I32)
        padded = (counts + tm_moe - 1) // tm_moe * tm_moe
        pad_ends = jnp.cumsum(padded)
        pad_starts = pad_ends - padded
        dest_flat = (pad_starts[mi[:, 0:TOP_K]] + mi[:, TOP_K:2 * TOP_K]).reshape(t * TOP_K)
        n_rows = t * TOP_K + N_EXPERTS * tm_moe
        n_blocks = n_rows // tm_moe
        blk_start = jnp.arange(n_blocks, dtype=I32) * tm_moe
        blk_e = jnp.minimum(jnp.sum(blk_start[:, None] >= pad_ends[None, :], axis=1),
                            N_EXPERTS - 1).astype(I32)
        n_used = (pad_ends[-1:] // tm_moe).astype(I32)

        xs = _dispatch(dest_flat, x1p, n_rows, tm=_tile(t, 512))
        yp = _experts(blk_e, n_used, xs, expert_w_gate[li].astype(BF16), expert_w_up[li].astype(BF16),
                      expert_w_down[li].astype(BF16), tm=tm_moe)
        x = _combine(dest_flat, x1, mf, ln2_g[li], ln2_b[li], yp, tm=_tile(t, 256)).reshape(bsz, s, d)
    return x
```

```python
import functools
import math

import jax
import jax.numpy as jnp
from jax import lax
from jax.experimental import pallas as pl
from jax.experimental.pallas import tpu as pltpu

F32 = jnp.float32
BF16 = jnp.bfloat16
U32 = jnp.uint32
I32 = jnp.int32

N_HEADS = 8
HEAD_DIM = 64
CONV_TAPS = 31
REL_BUCKETS = 32
REL_MAX_DIST = 128
N_GROUPS = 4
EXPERTS_PER_GROUP = 8
N_EXPERTS = N_GROUPS * EXPERTS_PER_GROUP
TOP_K = 2
DEPTH = 1
DN_ALPHA = (2.0 * DEPTH) ** 0.25
LN_EPS = 1e-5
NEG_INF = -1e30
LOG2E = 1.4426950408889634

LANES = 128
SUBLANES = 8
VMEM_LIMIT = 56 * 1024 * 1024
CONV_HALO = 32
CONV_SHIFT_ROWS = 56


def _cparams(n_axes, flags=None):
    return pltpu.CompilerParams(dimension_semantics=("arbitrary",) * n_axes,
                                vmem_limit_bytes=VMEM_LIMIT, flags=flags)


def _sigmoid(x):
    return 1.0 / (1.0 + jnp.exp(-x))


def _layer_norm(z, g, b):
    mu = jnp.mean(z, axis=-1, keepdims=True)
    zc = z - mu
    var = jnp.mean(zc * zc, axis=-1, keepdims=True)
    return zc * lax.rsqrt(var + LN_EPS) * g + b


def _pack_halves(y):
    n = y.shape[1] // 2
    bits = pltpu.bitcast(y.astype(BF16).astype(F32), U32)
    return (bits[:, :n] >> 16) | (bits[:, n:] & jnp.uint32(0xFFFF0000))


def _unpack_halves(w):
    lo = pltpu.bitcast(w << 16, F32)
    hi = pltpu.bitcast(w & jnp.uint32(0xFFFF0000), F32)
    return jnp.concatenate([lo, hi], axis=1)


def _proj_kernel(x_ref, w_ref, b_ref, q_ref, k_ref, v_ref, h_ref, ga_ref, gb_ref, *, width, chunk, q_scale):
    xb = x_ref[...].astype(BF16)

    def lin(col0):
        return (jnp.dot(xb, w_ref[:, col0:col0 + chunk], preferred_element_type=F32)
                + b_ref[:, col0:col0 + chunk])

    for c in range(0, width, chunk):
        sl = slice(c, c + chunk)
        q_ref[:, sl] = (lin(c) * q_scale).astype(BF16)
        k_ref[:, sl] = lin(width + c).astype(BF16)
        v_ref[:, sl] = lin(2 * width + c).astype(BF16)
        h_ref[:, sl] = (lin(3 * width + c) * _sigmoid(lin(4 * width + c))).astype(BF16)
        ga_ref[:, sl] = _sigmoid(lin(5 * width + c)).astype(BF16)
        gb_ref[:, sl] = _sigmoid(lin(6 * width + c)).astype(BF16)


def _proj(x2, w_in, b_in, *, width, tm, chunk, q_scale):
    t, d = x2.shape
    n_cols = w_in.shape[1]
    out = jax.ShapeDtypeStruct((t, width), BF16)
    row_spec = pl.BlockSpec((tm, width), lambda i: (i, 0))
    return pl.pallas_call(
        functools.partial(_proj_kernel, width=width, chunk=chunk, q_scale=q_scale),
        grid=(t // tm,),
        in_specs=[pl.BlockSpec((tm, d), lambda i: (i, 0)),
                  pl.BlockSpec((d, n_cols), lambda i: (0, 0), pipeline_mode=pl.Buffered(1)),
                  pl.BlockSpec((1, n_cols), lambda i: (0, 0))],
        out_specs=[row_spec] * 6,
        out_shape=[out] * 6,
        compiler_params=_cparams(1),
        name="proj",
    )(x2, w_in, b_in)


def _sublane_all(x, op):
    for shift in (4, 2, 1):
        x = op(x, pltpu.roll(x, shift, 0))
    return x


def _attn_kernel(tab_ref, bucket_ref, lam_ref, q_ref, k_ref, v_ref, g_ref, o_ref,
                 bias_scr, qz_scr, vt_scr, s0_scr, s1_scr, p0_scr, p1_scr, a0_scr, a1_scr,
                 m_scr, l_scr, acc_scr, *, tq, tk, dh, lam_init):
    h = pl.program_id(0)
    b = pl.program_id(1)
    qi = pl.program_id(2)
    hw = 2 * dh
    n_kv = vt_scr.shape[0]
    sw = tk
    n_sub = 2 * tq // sw

    @pl.when((b == 0) & (qi == 0))
    def _build_bias():
        far = tab_ref[REL_BUCKETS - 1, h]
        for t in range(3):
            for c in range(tq // sw):
                bk = bucket_ref[t, :, c * sw:(c + 1) * sw]
                acc = jnp.full((tk, sw), NEG_INF, F32)
                for r in range(REL_BUCKETS):
                    acc = jnp.where(bk == r, (tab_ref[r, h] - far) * LOG2E, acc)
                bias_scr[t, :, c * sw:(c + 1) * sw] = acc

    @pl.when(qi == 0)
    def _transpose_values():
        for jj in range(n_kv):
            vt_scr[jj] = v_ref[0, jj * tk:(jj + 1) * tk, :].astype(F32).T.astype(BF16)

    q = q_ref[0]
    lane = lax.broadcasted_iota(I32, (tq, hw), 1)
    zero = jnp.zeros_like(q)
    qz_scr[0:tq, :] = jnp.where(lane < dh, q, zero)
    qz_scr[tq:2 * tq, :] = jnp.where(lane >= dh, q, zero)
    m_scr[...] = jnp.full(m_scr.shape, NEG_INF, F32)
    l_scr[...] = jnp.zeros(l_scr.shape, F32)
    acc_scr[...] = jnp.zeros(acc_scr.shape, F32)

    s_bufs, p_bufs, a_bufs = (s0_scr, s1_scr), (p0_scr, p1_scr), (a0_scr, a1_scr)
    p1_scr[...] = jnp.zeros(p1_scr.shape, BF16)
    a1_scr[...] = jnp.ones(a1_scr.shape, F32)

    def issue_scores(j, slot):
        rows = (slice(j * tk, (j + 1) * tk) if isinstance(j, int)
                else pl.ds(pl.multiple_of(j * tk, tk), tk))
        s_bufs[slot][...] = lax.dot_general(k_ref[0, rows, :], qz_scr[...], (((1,), (1,)), ((), ())),
                                            preferred_element_type=F32)

    def accumulate(j_prev, slot_prev):
        pv = jnp.dot(vt_scr[j_prev], p_bufs[slot_prev][...], preferred_element_type=F32)
        acc3 = acc_scr[...].reshape(hw // SUBLANES, SUBLANES, 2 * tq)
        return (a_bufs[slot_prev][...][None] * acc3).reshape(hw, 2 * tq) + pv

    def step(j, bias_idx, slot, issue_next=True):
        if issue_next:
            issue_scores(j + 1, 1 - slot)
        acc_new = accumulate(jnp.maximum(j - 1, 0), 1 - slot)
        for c in range(n_sub):
            cs = slice(c * sw, (c + 1) * sw)
            s = s_bufs[slot][:, cs]
            if bias_idx is not None:
                qs = (c * sw) % tq
                s = s + bias_scr[bias_idx, :, qs:qs + sw]
            s3 = s.reshape(tk // SUBLANES, SUBLANES, sw)
            m_prev = m_scr[:, cs]
            m_new = jnp.maximum(m_prev, _sublane_all(jnp.max(s3, axis=0), jnp.maximum))
            alpha = jnp.exp2(m_prev - m_new)
            p3 = jnp.exp2(s3 - m_new[None])
            l_scr[:, cs] = alpha * l_scr[:, cs] + jnp.sum(p3, axis=0)
            p_bufs[slot][:, cs] = p3.reshape(tk, sw).astype(BF16)
            a_bufs[slot][:, cs] = alpha
            m_scr[:, cs] = m_new
        acc_scr[...] = acc_new

    issue_scores(0, 0)

    @pl.when(qi == 0)
    def _first_query_tile():
        step(0, 1, 0)
        step(1, 2, 1, issue_next=False)

    @pl.when(qi >= 1)
    def _later_query_tiles():
        def far_pair(i, carry):
            step(2 * i, None, 0)
            step(2 * i + 1, None, 1)
            return carry

        lax.fori_loop(0, qi - 1, far_pair, 0)
        step(2 * qi - 2, None, 0)
        step(2 * qi - 1, 0, 1)
        step(2 * qi, 1, 0)
        step(2 * qi + 1, 2, 1, issue_next=False)

    acc_scr[...] = accumulate(2 * qi + 1, 1)

    lv = lam_ref[...]
    lam = (jnp.exp(jnp.sum(lv[0:1] * lv[1:2], axis=-1, keepdims=True))
           - jnp.exp(jnp.sum(lv[2:3] * lv[3:4], axis=-1, keepdims=True)) + lam_init)
    inv_l = 1.0 / _sublane_all(l_scr[...], jnp.add)
    acc3 = acc_scr[...].reshape(hw // SUBLANES, SUBLANES, 2 * tq)
    o3 = acc3[:, :, 0:tq] * inv_l[None, :, 0:tq] - lam * (acc3[:, :, tq:2 * tq] * inv_l[None, :, tq:2 * tq])
    ms = _sublane_all(jnp.sum(o3 * o3, axis=0), jnp.add) * (1.0 / hw)
    y = (o3 * lax.rsqrt(ms + LN_EPS)[None]).reshape(hw, tq) * (g_ref[...] * (1.0 - lam_init))
    o_ref[0] = y.T.astype(o_ref.dtype)


def _rel_bucket(dist):
    max_exact = REL_BUCKETS // 2
    d = jnp.maximum(dist, 1).astype(F32)
    large = max_exact + (jnp.log(d / max_exact) / math.log(REL_MAX_DIST / max_exact)
                         * (REL_BUCKETS - max_exact)).astype(I32)
    large = jnp.minimum(large, REL_BUCKETS - 1)
    return jnp.where(dist < max_exact, dist, large)


def _near_buckets(tq, tk):
    tiles = []
    for rel_tile in (-1, 0, 1):
        dist = (jnp.arange(tq, dtype=I32)[None, :]
                - (rel_tile * tk + jnp.arange(tk, dtype=I32))[:, None])
        tiles.append(jnp.where(dist >= 0, _rel_bucket(jnp.maximum(dist, 0)), -1))
    return jnp.stack(tiles).astype(I32)


def _attention(q, k, v, rel_bias, diff_lambda, head_norm_g, *, tk, lam_init):
    bsz, s, width = q.shape
    hw = width // N_HEADS
    tq = 2 * tk
    assert hw == 2 * HEAD_DIM and hw % LANES == 0 and s % tq == 0
    assert tk + 1 >= REL_MAX_DIST
    kernel = functools.partial(_attn_kernel, tq=tq, tk=tk, dh=HEAD_DIM, lam_init=lam_init)
    stat = pltpu.VMEM((SUBLANES, 2 * tq), F32)
    return pl.pallas_call(
        kernel,
        grid=(N_HEADS, bsz, s // tq),
        in_specs=[pl.BlockSpec(memory_space=pltpu.SMEM),
                  pl.BlockSpec((3, tk, tq), lambda h, b, i: (0, 0, 0)),
                  pl.BlockSpec((4, HEAD_DIM), lambda h, b, i: (0, 0)),
                  pl.BlockSpec((1, tq, hw), lambda h, b, i: (b, i, h)),
                  pl.BlockSpec((1, s, hw), lambda h, b, i: (b, 0, h)),
                  pl.BlockSpec((1, s, hw), lambda h, b, i: (b, 0, h)),
                  pl.BlockSpec((hw, 1), lambda h, b, i: (0, 0))],
        out_specs=pl.BlockSpec((1, tq, hw), lambda h, b, i: (b, i, h)),
        out_shape=jax.ShapeDtypeStruct((bsz, s, width), BF16),
        scratch_shapes=[pltpu.VMEM((3, tk, tq), F32),
                        pltpu.VMEM((2 * tq, hw), BF16),
                        pltpu.VMEM((s // tk, hw, tk), BF16),
                        pltpu.VMEM((tk, 2 * tq), F32),
                        pltpu.VMEM((tk, 2 * tq), F32),
                        pltpu.VMEM((tk, 2 * tq), BF16),
                        pltpu.VMEM((tk, 2 * tq), BF16),
                        stat, stat,
                        stat, stat,
                        pltpu.VMEM((hw, 2 * tq), F32)],
        compiler_params=_cparams(3),
        name="attn",
    )(rel_bias, _near_buckets(tq, tk), diff_lambda, q, k, v, head_norm_g.reshape(hw, 1))


def _conv_kernel(h_ref, halo_ref, w_ref, b_ref, g_ref, bb_ref, o_ref, sh, *, ts, rc):
    i = pl.program_id(1)
    n_ext = CONV_HALO + ts
    n_sh = n_ext - SUBLANES
    halo = halo_ref[0].astype(F32)
    sh[0, 0:CONV_HALO, :] = jnp.where(i > 0, halo, jnp.zeros_like(halo))
    sh[0, CONV_HALO:n_ext, :] = h_ref[0].astype(F32)
    for k in range(1, SUBLANES):
        for r0 in range(0, n_sh, CONV_SHIFT_ROWS):
            n = min(CONV_SHIFT_ROWS, n_sh - r0)
            sh[k, r0:r0 + n, :] = sh[0, r0 + k:r0 + k + n, :]
    off = CONV_HALO - (CONV_TAPS - 1)
    for r0 in range(0, ts, rc):
        acc = None
        for t in range(CONV_TAPS):
            k, base = (off + t) % SUBLANES, (off + t) // SUBLANES * SUBLANES
            slab = sh[k, r0 + base:r0 + base + rc, :].reshape(rc // SUBLANES, SUBLANES, -1)
            term = slab * w_ref[t][None]
            acc = term if acc is None else acc + term
        y = _layer_norm(acc.reshape(rc, -1) + b_ref[...], g_ref[...], bb_ref[...])
        o_ref[0, r0:r0 + rc, :] = (y * _sigmoid(y)).astype(o_ref.dtype)


def _conv(hglu, conv_w, conv_b, ln_g, ln_b, *, ts, rc):
    bsz, s, c = hglu.shape
    assert s % ts == 0 and ts % CONV_HALO == 0 and CONV_HALO >= CONV_TAPS - 1
    blocks_per_tile = ts // CONV_HALO
    vec = pl.BlockSpec((1, c), lambda b, i: (0, 0))
    return pl.pallas_call(
        functools.partial(_conv_kernel, ts=ts, rc=rc),
        grid=(bsz, s // ts),
        in_specs=[pl.BlockSpec((1, ts, c), lambda b, i: (b, i, 0)),
                  pl.BlockSpec((1, CONV_HALO, c),
                               lambda b, i: (b, jnp.maximum(i * blocks_per_tile - 1, 0), 0)),
                  pl.BlockSpec((CONV_TAPS, SUBLANES, c), lambda b, i: (0, 0, 0)),
                  vec, vec, vec],
        out_specs=pl.BlockSpec((1, ts, c), lambda b, i: (b, i, 0)),
        out_shape=jax.ShapeDtypeStruct((bsz, s, c), BF16),
        scratch_shapes=[pltpu.VMEM((SUBLANES, CONV_HALO + ts, c), F32)],
        compiler_params=_cparams(2),
        name="conv",
    )(hglu, hglu, jnp.broadcast_to(conv_w[:, None, :], (CONV_TAPS, SUBLANES, c)),
      conv_b.reshape(1, c), ln_g.reshape(1, c), ln_b.reshape(1, c))


def _mix_kernel(o_ref, hc_ref, ga_ref, gb_ref, x_ref, woa_ref, wco_ref, bco_ref, wout_ref,
                g1_ref, b1_ref, wr_ref, br_ref, tri_ref,
                x1_ref, x1p_ref, mi_ref, mf_ref, cnt_ref, carry_scr, *, tm):
    i = pl.program_id(0)

    @pl.when(i == 0)
    def _init():
        carry_scr[...] = jnp.zeros(carry_scr.shape, F32)

    y_a = jnp.dot(o_ref[...], woa_ref[...], preferred_element_type=F32)
    y_b = jnp.dot(hc_ref[...], wco_ref[...], preferred_element_type=F32) + bco_ref[...]
    merged = ga_ref[...].astype(F32) * y_a + gb_ref[...].astype(F32) * y_b
    mixed = jnp.dot(merged.astype(BF16), wout_ref[...], preferred_element_type=F32)
    x1 = _layer_norm(DN_ALPHA * x_ref[...] + mixed, g1_ref[...], b1_ref[...])
    x1_ref[...] = x1
    x1p_ref[...] = _pack_halves(x1)

    logits = jnp.dot(x1.astype(BF16), wr_ref[...], preferred_element_type=F32) + br_ref[...]
    lane = lax.broadcasted_iota(I32, (tm, LANES), 1)
    g_lane = lane < N_GROUPS
    gl = jnp.where(g_lane, logits, NEG_INF)
    g_max = jnp.max(gl, axis=-1, keepdims=True)
    g_idx = jnp.min(jnp.where(gl == g_max, lane, LANES), axis=-1, keepdims=True)
    p_g = 1.0 / jnp.sum(jnp.where(g_lane, jnp.exp(gl - g_max), 0.0), axis=-1, keepdims=True)
    lo = N_GROUPS + g_idx * EXPERTS_PER_GROUP
    el = jnp.where(lane >= lo, jnp.where(lane < lo + EXPERTS_PER_GROUP, logits, NEG_INF), NEG_INF)
    v1 = jnp.max(el, axis=-1, keepdims=True)
    i1 = jnp.min(jnp.where(el == v1, lane, LANES), axis=-1, keepdims=True)
    el2 = jnp.where(lane == i1, NEG_INF, el)
    v2 = jnp.max(el2, axis=-1, keepdims=True)
    i2 = jnp.min(jnp.where(lane == i1, LANES, jnp.where(el2 == v2, lane, LANES)), axis=-1, keepdims=True)
    ex = jnp.exp(v2 - v1)
    w1 = 1.0 / (1.0 + ex)
    gate1 = p_g * w1
    gate2 = p_g * (ex * w1)
    e1 = i1 - N_GROUPS
    e2 = i2 - N_GROUPS

    carry = carry_scr[...]
    tri = tri_ref[...]
    ranks = []
    for e in (e1, e2):
        hit = lane == e
        oh = jnp.where(hit, 1.0, 0.0)
        before = jnp.dot(tri, oh.astype(BF16), preferred_element_type=F32) + carry
        ranks.append(jnp.sum(jnp.where(hit, before, 0.0), axis=-1, keepdims=True).astype(I32))
        carry = carry + jnp.sum(oh, axis=0, keepdims=True)
    carry_scr[...] = carry
    cnt_ref[...] = carry

    mi_ref[...] = jnp.where(lane == 0, e1, jnp.where(lane == 1, e2,
                            jnp.where(lane == 2, ranks[0], jnp.where(lane == 3, ranks[1], 0))))
    mf_ref[...] = jnp.where(lane == 0, gate1, jnp.where(lane == 1, gate2, 0.0))


def _mix(o_n, hc, ga, gb, x2, w_oa, w_co, b_co, w_out, ln_g, ln_b, w_r, b_r, *, tm):
    t, d = x2.shape
    row = lambda w: pl.BlockSpec((tm, w), lambda i: (i, 0))
    full = lambda a: pl.BlockSpec(a.shape, lambda i: (0,) * a.ndim, pipeline_mode=pl.Buffered(1))
    tri = (jnp.arange(tm)[:, None] > jnp.arange(tm)[None, :]).astype(BF16)
    b_co, ln_g, ln_b = b_co.reshape(1, d), ln_g.reshape(1, d), ln_b.reshape(1, d)
    return pl.pallas_call(
        functools.partial(_mix_kernel, tm=tm),
        grid=(t // tm,),
        in_specs=[row(d), row(d), row(d), row(d), row(d),
                  full(w_oa), full(w_co), full(b_co), full(w_out), full(ln_g), full(ln_b),
                  full(w_r), full(b_r), full(tri)],
        out_specs=[row(d), row(d // 2), row(LANES), row(LANES),
                   pl.BlockSpec((1, LANES), lambda i: (0, 0))],
        out_shape=[jax.ShapeDtypeStruct((t, d), F32),
                   jax.ShapeDtypeStruct((t, d // 2), U32),
                   jax.ShapeDtypeStruct((t, LANES), I32),
                   jax.ShapeDtypeStruct((t, LANES), F32),
                   jax.ShapeDtypeStruct((1, LANES), F32)],
        scratch_shapes=[pltpu.VMEM((1, LANES), F32)],
        compiler_params=_cparams(1),
        name="mix",
    )(o_n, hc, ga, gb, x2, w_oa, w_co, b_co, w_out, ln_g, ln_b, w_r, b_r, tri)


def _row_copy(src, src_row, dst, dst_row, sem):
    return pltpu.make_async_copy(src.at[pl.ds(src_row, 1), :], dst.at[pl.ds(dst_row, 1), :], sem)


def _dispatch_kernel(dest_ref, x_ref, xs_in_ref, xs_ref, sem, *, tm):
    del xs_in_ref
    base = pl.program_id(0) * tm

    def issue(r, carry):
        for c in range(TOP_K):
            _row_copy(x_ref, r, xs_ref, dest_ref[TOP_K * (base + r) + c], sem).start()
        return carry

    lax.fori_loop(0, tm, issue, 0)
    for _ in range(TOP_K):
        pltpu.make_async_copy(x_ref, xs_ref.at[pl.ds(0, tm), :], sem).wait()


def _dispatch(dest_flat, x1p, n_rows, *, tm):
    t, w = x1p.shape
    return pl.pallas_call(
        functools.partial(_dispatch_kernel, tm=tm),
        grid_spec=pltpu.PrefetchScalarGridSpec(
            num_scalar_prefetch=1,
            grid=(t // tm,),
            in_specs=[pl.BlockSpec((tm, w), lambda i, dest: (i, 0)),
                      pl.BlockSpec(memory_space=pl.ANY)],
            out_specs=pl.BlockSpec(memory_space=pl.ANY),
            scratch_shapes=[pltpu.SemaphoreType.DMA(())]),
        out_shape=jax.ShapeDtypeStruct((n_rows, w), U32),
        input_output_aliases={2: 0},
        compiler_params=_cparams(1),
        name="dispatch",
    )(dest_flat, x1p, jnp.zeros((n_rows, w), U32))


def _expert_kernel(blk_e_ref, n_used_ref, xs_ref, wg_ref, wu_ref, wd_ref, y_ref):
    del blk_e_ref
    active = pl.program_id(0) < n_used_ref[0]

    @pl.when(jnp.logical_not(active))
    def _unused_block():
        y_ref[...] = jnp.zeros(y_ref.shape, y_ref.dtype)

    @pl.when(active)
    def _():
        xb = _unpack_halves(xs_ref[...]).astype(BF16)
        g = jnp.dot(xb, wg_ref[0].astype(BF16), preferred_element_type=F32)
        u = jnp.dot(xb, wu_ref[0].astype(BF16), preferred_element_type=F32)
        hid = (g * _sigmoid(g) * u).astype(BF16)
        y_ref[...] = _pack_halves(jnp.dot(hid, wd_ref[0].astype(BF16), preferred_element_type=F32))


def _experts(blk_e, n_used, xs, w_gate, w_up, w_down, *, tm):
    n_rows, w = xs.shape
    _, d, de = w_gate.shape
    blk = lambda i, blk_e, n_used: jnp.minimum(i, n_used[0] - 1)
    row_map = lambda i, blk_e, n_used: (blk(i, blk_e, n_used), 0)
    w_map = lambda i, blk_e, n_used: (blk_e[blk(i, blk_e, n_used)], 0, 0)
    return pl.pallas_call(
        _expert_kernel,
        grid_spec=pltpu.PrefetchScalarGridSpec(
            num_scalar_prefetch=2,
            grid=(n_rows // tm,),
            in_specs=[pl.BlockSpec((tm, w), row_map),
                      pl.BlockSpec((1, d, de), w_map),
                      pl.BlockSpec((1, d, de), w_map),
                      pl.BlockSpec((1, de, d), w_map)],
            out_specs=pl.BlockSpec((tm, w), lambda i, blk_e, n_used: (i, 0))),
        out_shape=jax.ShapeDtypeStruct((n_rows, w), U32),
        compiler_params=_cparams(1),
        name="experts",
    )(blk_e, n_used, xs, w_gate, w_up, w_down)


def _combine_kernel(dest_ref, x1_ref, mf_ref, g_ref, b_ref, yp_ref, o_ref, buf, sems, *, tm, n_tiles):
    i = pl.program_id(0)

    def issue(tile, slot):
        base = tile * tm

        def body(r, carry):
            for c in range(TOP_K):
                _row_copy(yp_ref, dest_ref[TOP_K * (base + r) + c], buf.at[slot, c], r, sems.at[slot]).start()
            return carry

        lax.fori_loop(0, tm, body, 0)

    @pl.when(i == 0)
    def _first():
        issue(0, 0)

    @pl.when(i + 1 < n_tiles)
    def _next():
        issue(i + 1, (i + 1) % 2)

    slot = i % 2
    for c in range(TOP_K):
        pltpu.make_async_copy(yp_ref.at[pl.ds(0, tm), :], buf.at[slot, c], sems.at[slot]).wait()

    gates = mf_ref[...]
    ffn = (gates[:, 0:1] * _unpack_halves(buf[slot, 0]) + gates[:, 1:2] * _unpack_halves(buf[slot, 1]))
    o_ref[...] = _layer_norm(DN_ALPHA * x1_ref[...] + ffn, g_ref[...], b_ref[...])


def _combine(dest_flat, x1, mf, ln_g, ln_b, yp, *, tm):
    t, d = x1.shape
    n_tiles = t // tm
    vec = pl.BlockSpec((1, d), lambda i, dest: (0, 0))
    return pl.pallas_call(
        functools.partial(_combine_kernel, tm=tm, n_tiles=n_tiles),
        grid_spec=pltpu.PrefetchScalarGridSpec(
            num_scalar_prefetch=1,
            grid=(n_tiles,),
            in_specs=[pl.BlockSpec((tm, d), lambda i, dest: (i, 0)),
                      pl.BlockSpec((tm, LANES), lambda i, dest: (i, 0)),
                      vec, vec,
                      pl.BlockSpec(memory_space=pl.ANY)],
            out_specs=pl.BlockSpec((tm, d), lambda i, dest: (i, 0)),
            scratch_shapes=[pltpu.VMEM((2, TOP_K, tm, d // 2), U32),
                            pltpu.SemaphoreType.DMA((2,))]),
        out_shape=jax.ShapeDtypeStruct((t, d), F32),
        compiler_params=_cparams(1),
        name="combine",
    )(dest_flat, x1, mf, ln_g.reshape(1, d), ln_b.reshape(1, d), yp)


def _tile(n, pref):
    return pref if n % pref == 0 else n


def kernel(x, w_in, b_in, diff_lambda, head_norm_g, w_o_attn, rel_bias, conv_w, conv_b, conv_ln_g,
           conv_ln_b, w_conv_out, b_conv_out, w_out, ln1_g, ln1_b, router_g_w, router_g_b,
           router_e_w, router_e_b, expert_w_gate, expert_w_up, expert_w_down, ln2_g, ln2_b):
    bsz, s, d = x.shape
    t = bsz * s
    width = N_HEADS * 2 * HEAD_DIM
    assert w_in.shape[0] == DEPTH and w_in.shape[2] == 7 * width and width == d
    assert conv_w.shape[2] == width and TOP_K == 2
    tm_moe = _tile(t, 256)

    for li in range(DEPTH):
        lam_init = 0.8 - 0.6 * math.exp(-0.3 * li)
        x2 = x.reshape(t, d)
        q, k, v, hglu, ga, gb = _proj(
            x2, w_in[li].astype(BF16), b_in[li].reshape(1, -1), width=width, tm=_tile(t, 512),
            chunk=_tile(width, 512), q_scale=HEAD_DIM ** -0.5 * LOG2E)
        o_n = _attention(q.reshape(bsz, s, width), k.reshape(bsz, s, width), v.reshape(bsz, s, width),
                         rel_bias, diff_lambda[li], head_norm_g[li], tk=256 if s % 512 == 0 else 128,
                         lam_init=lam_init)
        hc = _conv(hglu.reshape(bsz, s, width), conv_w[li], conv_b[li], conv_ln_g[li], conv_ln_b[li],
                   ts=_tile(s, 256), rc=16)

        n_r = N_GROUPS + N_EXPERTS
        w_r = jnp.pad(jnp.concatenate([router_g_w[li], router_e_w[li]], axis=1),
                      ((0, 0), (0, LANES - n_r))).astype(BF16)
        b_r = jnp.pad(jnp.concatenate([router_g_b[li], router_e_b[li]]), (0, LANES - n_r)).reshape(1, LANES)
        x1, x1p, mi, mf, cnt = _mix(
            o_n.reshape(t, width), hc.reshape(t, width), ga, gb, x2,
            w_o_attn[li].astype(BF16), w_conv_out[li].astype(BF16), b_conv_out[li],
            w_out[li].astype(BF16), ln1_g[li], ln1_b[li], w_r, b_r, tm=_tile(t, 512))

        counts = cnt[0, :N_EXPERTS].astype(I32)
        padded = (counts + tm_moe - 1) // tm_moe * tm_moe
        pad_ends = jnp.cumsum(padded)
        pad_starts = pad_ends - padded
        chosen = mi[:, 0:TOP_K, None] == jnp.arange(N_EXPERTS, dtype=I32)[None, None, :]
        dest_flat = (jnp.sum(jnp.where(chosen, pad_starts[None, None, :], 0), axis=-1)
                     + mi[:, TOP_K:2 * TOP_K]).reshape(t * TOP_K)
        n_rows = t * TOP_K + N_EXPERTS * tm_moe
        n_blocks = n_rows // tm_moe
        blk_start = jnp.arange(n_blocks, dtype=I32) * tm_moe
        blk_e = jnp.minimum(jnp.sum(blk_start[:, None] >= pad_ends[None, :], axis=1),
                            N_EXPERTS - 1).astype(I32)
        n_used = (pad_ends[-1:] // tm_moe).astype(I32)

        xs = _dispatch(dest_flat, x1p, n_rows, tm=_tile(t, 512))
        yp = _experts(blk_e, n_used, xs, expert_w_gate[li], expert_w_up[li], expert_w_down[li], tm=tm_moe)
        x = _combine(dest_flat, x1, mf, ln2_g[li], ln2_b[li], yp, tm=_tile(t, 256)).reshape(bsz, s, d)
    return x
```

```python
import functools
import math

import jax
import jax.numpy as jnp
from jax import lax
from jax.experimental import pallas as pl
from jax.experimental.pallas import tpu as pltpu

F32 = jnp.float32
BF16 = jnp.bfloat16
U32 = jnp.uint32
I32 = jnp.int32

N_HEADS = 8
HEAD_DIM = 64
CONV_TAPS = 31
REL_BUCKETS = 32
REL_MAX_DIST = 128
N_GROUPS = 4
EXPERTS_PER_GROUP = 8
N_EXPERTS = N_GROUPS * EXPERTS_PER_GROUP
TOP_K = 2
DEPTH = 1
DN_ALPHA = (2.0 * DEPTH) ** 0.25
LN_EPS = 1e-5
NEG_INF = -1e30
LOG2E = 1.4426950408889634

LANES = 128
SUBLANES = 8
VMEM_LIMIT = 56 * 1024 * 1024
CONV_HALO = 32
CONV_SHIFT_ROWS = 56


def _cparams(n_axes, flags=None):
    return pltpu.CompilerParams(dimension_semantics=("arbitrary",) * n_axes,
                                vmem_limit_bytes=VMEM_LIMIT, flags=flags)


def _sigmoid(x):
    return 1.0 / (1.0 + jnp.exp(-x))


def _layer_norm(z, g, b):
    mu = jnp.mean(z, axis=-1, keepdims=True)
    zc = z - mu
    var = jnp.mean(zc * zc, axis=-1, keepdims=True)
    return zc * lax.rsqrt(var + LN_EPS) * g + b


def _pack_halves(y):
    n = y.shape[1] // 2
    bits = pltpu.bitcast(y.astype(BF16).astype(F32), U32)
    return (bits[:, :n] >> 16) | (bits[:, n:] & jnp.uint32(0xFFFF0000))


def _unpack_halves(w):
    lo = pltpu.bitcast(w << 16, F32)
    hi = pltpu.bitcast(w & jnp.uint32(0xFFFF0000), F32)
    return jnp.concatenate([lo, hi], axis=1)


def _proj_kernel(x_ref, w_ref, b_ref, q_ref, k_ref, v_ref, h_ref, ga_ref, gb_ref, *, width, chunk, q_scale):
    xb = x_ref[...].astype(BF16)

    def lin(col0):
        return (jnp.dot(xb, w_ref[:, col0:col0 + chunk], preferred_element_type=F32)
                + b_ref[:, col0:col0 + chunk])

    for c in range(0, width, chunk):
        sl = slice(c, c + chunk)
        q_ref[:, sl] = (lin(c) * q_scale).astype(BF16)
        k_ref[:, sl] = lin(width + c).astype(BF16)
        v_ref[:, sl] = lin(2 * width + c).astype(BF16)
        h_ref[:, sl] = (lin(3 * width + c) * _sigmoid(lin(4 * width + c))).astype(BF16)
        ga_ref[:, sl] = _sigmoid(lin(5 * width + c)).astype(BF16)
        gb_ref[:, sl] = _sigmoid(lin(6 * width + c)).astype(BF16)


def _proj(x2, w_in, b_in, *, width, tm, chunk, q_scale):
    t, d = x2.shape
    n_cols = w_in.shape[1]
    out = jax.ShapeDtypeStruct((t, width), BF16)
    row_spec = pl.BlockSpec((tm, width), lambda i: (i, 0))
    return pl.pallas_call(
        functools.partial(_proj_kernel, width=width, chunk=chunk, q_scale=q_scale),
        grid=(t // tm,),
        in_specs=[pl.BlockSpec((tm, d), lambda i: (i, 0)),
                  pl.BlockSpec((d, n_cols), lambda i: (0, 0), pipeline_mode=pl.Buffered(1)),
                  pl.BlockSpec((1, n_cols), lambda i: (0, 0))],
        out_specs=[row_spec] * 6,
        out_shape=[out] * 6,
        compiler_params=_cparams(1),
        name="proj",
    )(x2, w_in, b_in)


def _sublane_all(x, op):
    for shift in (4, 2, 1):
        x = op(x, pltpu.roll(x, shift, 0))
    return x


def _attn_kernel(tab_ref, bucket_ref, lam_ref, q_ref, k_ref, v_ref, g_ref, o_ref,
                 bias_scr, qz_scr, vt_scr, s0_scr, s1_scr, p0_scr, p1_scr, a0_scr, a1_scr,
                 m_scr, l_scr, acc_scr, *, tq, tk, dh, lam_init):
    h = pl.program_id(0)
    b = pl.program_id(1)
    qi = pl.program_id(2)
    hw = 2 * dh
    n_kv = vt_scr.shape[0]
    sw = LANES
    n_sub = 2 * tq // sw

    @pl.when((b == 0) & (qi == 0))
    def _build_bias():
        far = tab_ref[REL_BUCKETS - 1, h]
        for t in range(3):
            for c in range(tq // sw):
                bk = bucket_ref[t, :, c * sw:(c + 1) * sw]
                acc = jnp.full((tk, sw), NEG_INF, F32)
                for r in range(REL_BUCKETS):
                    acc = jnp.where(bk == r, (tab_ref[r, h] - far) * LOG2E, acc)
                bias_scr[t, :, c * sw:(c + 1) * sw] = acc

    @pl.when(qi == 0)
    def _transpose_values():
        for jj in range(n_kv):
            vt_scr[jj] = v_ref[0, jj * tk:(jj + 1) * tk, :].astype(F32).T.astype(BF16)

    q = q_ref[0]
    lane = lax.broadcasted_iota(I32, (tq, hw), 1)
    zero = jnp.zeros_like(q)
    qz_scr[0:tq, :] = jnp.where(lane < dh, q, zero)
    qz_scr[tq:2 * tq, :] = jnp.where(lane >= dh, q, zero)
    m_scr[...] = jnp.full(m_scr.shape, NEG_INF, F32)
    l_scr[...] = jnp.zeros(l_scr.shape, F32)
    acc_scr[...] = jnp.zeros(acc_scr.shape, F32)

    s_bufs, p_bufs, a_bufs = (s0_scr, s1_scr), (p0_scr, p1_scr), (a0_scr, a1_scr)
    for p_buf, a_buf in zip(p_bufs, a_bufs):
        p_buf[...] = jnp.zeros(p_buf.shape, BF16)
        a_buf[...] = jnp.ones(a_buf.shape, F32)

    def issue_scores(j, slot):
        rows = (slice(j * tk, (j + 1) * tk) if isinstance(j, int)
                else pl.ds(pl.multiple_of(j * tk, tk), tk))
        s_bufs[slot][...] = lax.dot_general(k_ref[0, rows, :], qz_scr[...], (((1,), (1,)), ((), ())),
                                            preferred_element_type=F32)

    def accumulate(j_done, slot):
        pv = jnp.dot(vt_scr[j_done], p_bufs[slot][...], preferred_element_type=F32)
        acc3 = acc_scr[...].reshape(hw // SUBLANES, SUBLANES, 2 * tq)
        return (a_bufs[slot][...][None] * acc3).reshape(hw, 2 * tq) + pv

    def step(j, bias_idx, slot, issue_next=True):
        if issue_next:
            issue_scores(j + 1, 1 - slot)
        acc_new = accumulate(jnp.maximum(j - 2, 0), slot)
        for c in range(n_sub):
            cs = slice(c * sw, (c + 1) * sw)
            s = s_bufs[slot][:, cs]
            if bias_idx is not None:
                qs = (c * sw) % tq
                s = s + bias_scr[bias_idx, :, qs:qs + sw]
            s3 = s.reshape(tk // SUBLANES, SUBLANES, sw)
            m_prev = m_scr[:, cs]
            m_new = jnp.maximum(m_prev, _sublane_all(jnp.max(s3, axis=0), jnp.maximum))
            alpha = jnp.exp2(m_prev - m_new)
            p3 = jnp.exp2(s3 - m_new[None])
            l_scr[:, cs] = alpha * l_scr[:, cs] + jnp.sum(p3, axis=0)
            p_bufs[slot][:, cs] = p3.reshape(tk, sw).astype(BF16)
            a_bufs[slot][:, cs] = alpha
            m_scr[:, cs] = m_new
        acc_scr[...] = acc_new

    issue_scores(0, 0)

    @pl.when(qi == 0)
    def _first_query_tile():
        step(0, 1, 0)
        step(1, 2, 1, issue_next=False)

    @pl.when(qi >= 1)
    def _later_query_tiles():
        def far_pair(i, carry):
            step(2 * i, None, 0)
            step(2 * i + 1, None, 1)
            return carry

        lax.fori_loop(0, qi - 1, far_pair, 0)
        step(2 * qi - 2, None, 0)
        step(2 * qi - 1, 0, 1)
        step(2 * qi, 1, 0)
        step(2 * qi + 1, 2, 1, issue_next=False)

    acc_scr[...] = accumulate(2 * qi, 0)
    acc_scr[...] = accumulate(2 * qi + 1, 1)

    lv = lam_ref[...]
    lam = (jnp.exp(jnp.sum(lv[0:1] * lv[1:2], axis=-1, keepdims=True))
           - jnp.exp(jnp.sum(lv[2:3] * lv[3:4], axis=-1, keepdims=True)) + lam_init)
    inv_l = 1.0 / _sublane_all(l_scr[...], jnp.add)
    acc3 = acc_scr[...].reshape(hw // SUBLANES, SUBLANES, 2 * tq)
    o3 = acc3[:, :, 0:tq] * inv_l[None, :, 0:tq] - lam * (acc3[:, :, tq:2 * tq] * inv_l[None, :, tq:2 * tq])
    ms = _sublane_all(jnp.sum(o3 * o3, axis=0), jnp.add) * (1.0 / hw)
    y = (o3 * lax.rsqrt(ms + LN_EPS)[None]).reshape(hw, tq) * (g_ref[...] * (1.0 - lam_init))
    o_ref[0] = y.T.astype(o_ref.dtype)


def _rel_bucket(dist):
    max_exact = REL_BUCKETS // 2
    d = jnp.maximum(dist, 1).astype(F32)
    large = max_exact + (jnp.log(d / max_exact) / math.log(REL_MAX_DIST / max_exact)
                         * (REL_BUCKETS - max_exact)).astype(I32)
    large = jnp.minimum(large, REL_BUCKETS - 1)
    return jnp.where(dist < max_exact, dist, large)


def _near_buckets(tq, tk):
    tiles = []
    for rel_tile in (-1, 0, 1):
        dist = (jnp.arange(tq, dtype=I32)[None, :]
                - (rel_tile * tk + jnp.arange(tk, dtype=I32))[:, None])
        tiles.append(jnp.where(dist >= 0, _rel_bucket(jnp.maximum(dist, 0)), -1))
    return jnp.stack(tiles).astype(I32)


def _attention(q, k, v, rel_bias, diff_lambda, head_norm_g, *, tk, lam_init):
    bsz, s, width = q.shape
    hw = width // N_HEADS
    tq = 2 * tk
    assert hw == 2 * HEAD_DIM and hw % LANES == 0 and s % tq == 0
    assert tk + 1 >= REL_MAX_DIST
    kernel = functools.partial(_attn_kernel, tq=tq, tk=tk, dh=HEAD_DIM, lam_init=lam_init)
    stat = pltpu.VMEM((SUBLANES, 2 * tq), F32)
    return pl.pallas_call(
        kernel,
        grid=(N_HEADS, bsz, s // tq),
        in_specs=[pl.BlockSpec(memory_space=pltpu.SMEM),
                  pl.BlockSpec((3, tk, tq), lambda h, b, i: (0, 0, 0)),
                  pl.BlockSpec((4, HEAD_DIM), lambda h, b, i: (0, 0)),
                  pl.BlockSpec((1, tq, hw), lambda h, b, i: (b, i, h)),
                  pl.BlockSpec((1, s, hw), lambda h, b, i: (b, 0, h)),
                  pl.BlockSpec((1, s, hw), lambda h, b, i: (b, 0, h)),
                  pl.BlockSpec((hw, 1), lambda h, b, i: (0, 0))],
        out_specs=pl.BlockSpec((1, tq, hw), lambda h, b, i: (b, i, h)),
        out_shape=jax.ShapeDtypeStruct((bsz, s, width), BF16),
        scratch_shapes=[pltpu.VMEM((3, tk, tq), F32),
                        pltpu.VMEM((2 * tq, hw), BF16),
                        pltpu.VMEM((s // tk, hw, tk), BF16),
                        pltpu.VMEM((tk, 2 * tq), F32),
                        pltpu.VMEM((tk, 2 * tq), F32),
                        pltpu.VMEM((tk, 2 * tq), BF16),
                        pltpu.VMEM((tk, 2 * tq), BF16),
                        stat, stat,
                        stat, stat,
                        pltpu.VMEM((hw, 2 * tq), F32)],
        compiler_params=_cparams(3),
        name="attn",
    )(rel_bias, _near_buckets(tq, tk), diff_lambda, q, k, v, head_norm_g.reshape(hw, 1))


def _conv_kernel(h_ref, halo_ref, w_ref, b_ref, g_ref, bb_ref, o_ref, sh, *, ts, rc):
    i = pl.program_id(1)
    n_ext = CONV_HALO + ts
    n_sh = n_ext - SUBLANES
    halo = halo_ref[0].astype(F32)
    sh[0, 0:CONV_HALO, :] = jnp.where(i > 0, halo, jnp.zeros_like(halo))
    sh[0, CONV_HALO:n_ext, :] = h_ref[0].astype(F32)
    for k in range(1, SUBLANES):
        for r0 in range(0, n_sh, CONV_SHIFT_ROWS):
            n = min(CONV_SHIFT_ROWS, n_sh - r0)
            sh[k, r0:r0 + n, :] = sh[0, r0 + k:r0 + k + n, :]
    off = CONV_HALO - (CONV_TAPS - 1)
    for r0 in range(0, ts, rc):
        acc = None
        for t in range(CONV_TAPS):
            k, base = (off + t) % SUBLANES, (off + t) // SUBLANES * SUBLANES
            slab = sh[k, r0 + base:r0 + base + rc, :].reshape(rc // SUBLANES, SUBLANES, -1)
            term = slab * w_ref[t][None]
            acc = term if acc is None else acc + term
        y = _layer_norm(acc.reshape(rc, -1) + b_ref[...], g_ref[...], bb_ref[...])
        o_ref[0, r0:r0 + rc, :] = (y * _sigmoid(y)).astype(o_ref.dtype)


def _conv(hglu, conv_w, conv_b, ln_g, ln_b, *, ts, rc):
    bsz, s, c = hglu.shape
    assert s % ts == 0 and ts % CONV_HALO == 0 and CONV_HALO >= CONV_TAPS - 1
    blocks_per_tile = ts // CONV_HALO
    vec = pl.BlockSpec((1, c), lambda b, i: (0, 0))
    return pl.pallas_call(
        functools.partial(_conv_kernel, ts=ts, rc=rc),
        grid=(bsz, s // ts),
        in_specs=[pl.BlockSpec((1, ts, c), lambda b, i: (b, i, 0)),
                  pl.BlockSpec((1, CONV_HALO, c),
                               lambda b, i: (b, jnp.maximum(i * blocks_per_tile - 1, 0), 0)),
                  pl.BlockSpec((CONV_TAPS, SUBLANES, c), lambda b, i: (0, 0, 0)),
                  vec, vec, vec],
        out_specs=pl.BlockSpec((1, ts, c), lambda b, i: (b, i, 0)),
        out_shape=jax.ShapeDtypeStruct((bsz, s, c), BF16),
        scratch_shapes=[pltpu.VMEM((SUBLANES, CONV_HALO + ts, c), F32)],
        compiler_params=_cparams(2),
        name="conv",
    )(hglu, hglu, jnp.broadcast_to(conv_w[:, None, :], (CONV_TAPS, SUBLANES, c)),
      conv_b.reshape(1, c), ln_g.reshape(1, c), ln_b.reshape(1, c))


def _mix_kernel(o_ref, hc_ref, ga_ref, gb_ref, x_ref, woa_ref, wco_ref, bco_ref, wout_ref,
                g1_ref, b1_ref, wr_ref, br_ref, tri_ref,
                x1_ref, x1p_ref, mi_ref, mf_ref, cnt_ref, carry_scr, *, tm):
    i = pl.program_id(0)

    @pl.when(i == 0)
    def _init():
        carry_scr[...] = jnp.zeros(carry_scr.shape, F32)

    y_a = jnp.dot(o_ref[...], woa_ref[...], preferred_element_type=F32)
    y_b = jnp.dot(hc_ref[...], wco_ref[...], preferred_element_type=F32) + bco_ref[...]
    merged = ga_ref[...].astype(F32) * y_a + gb_ref[...].astype(F32) * y_b
    mixed = jnp.dot(merged.astype(BF16), wout_ref[...], preferred_element_type=F32)
    x1 = _layer_norm(DN_ALPHA * x_ref[...] + mixed, g1_ref[...], b1_ref[...])
    x1_ref[...] = x1
    x1p_ref[...] = _pack_halves(x1)

    logits = jnp.dot(x1.astype(BF16), wr_ref[...], preferred_element_type=F32) + br_ref[...]
    lane = lax.broadcasted_iota(I32, (tm, LANES), 1)
    g_lane = lane < N_GROUPS
    gl = jnp.where(g_lane, logits, NEG_INF)
    g_max = jnp.max(gl, axis=-1, keepdims=True)
    g_idx = jnp.min(jnp.where(gl == g_max, lane, LANES), axis=-1, keepdims=True)
    p_g = 1.0 / jnp.sum(jnp.where(g_lane, jnp.exp(gl - g_max), 0.0), axis=-1, keepdims=True)
    lo = N_GROUPS + g_idx * EXPERTS_PER_GROUP
    el = jnp.where(lane >= lo, jnp.where(lane < lo + EXPERTS_PER_GROUP, logits, NEG_INF), NEG_INF)
    v1 = jnp.max(el, axis=-1, keepdims=True)
    i1 = jnp.min(jnp.where(el == v1, lane, LANES), axis=-1, keepdims=True)
    el2 = jnp.where(lane == i1, NEG_INF, el)
    v2 = jnp.max(el2, axis=-1, keepdims=True)
    i2 = jnp.min(jnp.where(lane == i1, LANES, jnp.where(el2 == v2, lane, LANES)), axis=-1, keepdims=True)
    ex = jnp.exp(v2 - v1)
    w1 = 1.0 / (1.0 + ex)
    gate1 = p_g * w1
    gate2 = p_g * (ex * w1)
    e1 = i1 - N_GROUPS
    e2 = i2 - N_GROUPS

    carry = carry_scr[...]
    tri = tri_ref[...]
    ranks = []
    for e in (e1, e2):
        hit = lane == e
        oh = jnp.where(hit, 1.0, 0.0)
        before = jnp.dot(tri, oh.astype(BF16), preferred_element_type=F32) + carry
        ranks.append(jnp.sum(jnp.where(hit, before, 0.0), axis=-1, keepdims=True).astype(I32))
        carry = carry + jnp.sum(oh, axis=0, keepdims=True)
    carry_scr[...] = carry
    cnt_ref[...] = carry

    mi_ref[...] = jnp.where(lane == 0, e1, jnp.where(lane == 1, e2,
                            jnp.where(lane == 2, ranks[0], jnp.where(lane == 3, ranks[1], 0))))
    mf_ref[...] = jnp.where(lane == 0, gate1, jnp.where(lane == 1, gate2, 0.0))


def _mix(o_n, hc, ga, gb, x2, w_oa, w_co, b_co, w_out, ln_g, ln_b, w_r, b_r, *, tm):
    t, d = x2.shape
    row = lambda w: pl.BlockSpec((tm, w), lambda i: (i, 0))
    full = lambda a: pl.BlockSpec(a.shape, lambda i: (0,) * a.ndim, pipeline_mode=pl.Buffered(1))
    tri = (jnp.arange(tm)[:, None] > jnp.arange(tm)[None, :]).astype(BF16)
    b_co, ln_g, ln_b = b_co.reshape(1, d), ln_g.reshape(1, d), ln_b.reshape(1, d)
    return pl.pallas_call(
        functools.partial(_mix_kernel, tm=tm),
        grid=(t // tm,),
        in_specs=[row(d), row(d), row(d), row(d), row(d),
                  full(w_oa), full(w_co), full(b_co), full(w_out), full(ln_g), full(ln_b),
                  full(w_r), full(b_r), full(tri)],
        out_specs=[row(d), row(d // 2), row(LANES), row(LANES),
                   pl.BlockSpec((1, LANES), lambda i: (0, 0))],
        out_shape=[jax.ShapeDtypeStruct((t, d), F32),
                   jax.ShapeDtypeStruct((t, d // 2), U32),
                   jax.ShapeDtypeStruct((t, LANES), I32),
                   jax.ShapeDtypeStruct((t, LANES), F32),
                   jax.ShapeDtypeStruct((1, LANES), F32)],
        scratch_shapes=[pltpu.VMEM((1, LANES), F32)],
        compiler_params=_cparams(1),
        name="mix",
    )(o_n, hc, ga, gb, x2, w_oa, w_co, b_co, w_out, ln_g, ln_b, w_r, b_r, tri)


def _row_copy(src, src_row, dst, dst_row, sem):
    return pltpu.make_async_copy(src.at[pl.ds(src_row, 1), :], dst.at[pl.ds(dst_row, 1), :], sem)


def _dispatch_kernel(dest_ref, x_ref, xs_in_ref, xs_ref, sem, *, tm):
    del xs_in_ref
    base = pl.program_id(0) * tm

    def issue(r, carry):
        for c in range(TOP_K):
            _row_copy(x_ref, r, xs_ref, dest_ref[TOP_K * (base + r) + c], sem).start()
        return carry

    lax.fori_loop(0, tm, issue, 0)
    for _ in range(TOP_K):
        pltpu.make_async_copy(x_ref, xs_ref.at[pl.ds(0, tm), :], sem).wait()


def _dispatch(dest_flat, x1p, n_rows, *, tm):
    t, w = x1p.shape
    return pl.pallas_call(
        functools.partial(_dispatch_kernel, tm=tm),
        grid_spec=pltpu.PrefetchScalarGridSpec(
            num_scalar_prefetch=1,
            grid=(t // tm,),
            in_specs=[pl.BlockSpec((tm, w), lambda i, dest: (i, 0)),
                      pl.BlockSpec(memory_space=pl.ANY)],
            out_specs=pl.BlockSpec(memory_space=pl.ANY),
            scratch_shapes=[pltpu.SemaphoreType.DMA(())]),
        out_shape=jax.ShapeDtypeStruct((n_rows, w), U32),
        input_output_aliases={2: 0},
        compiler_params=_cparams(1),
        name="dispatch",
    )(dest_flat, x1p, jnp.zeros((n_rows, w), U32))


def _expert_kernel(blk_e_ref, n_used_ref, xs_ref, wg_ref, wu_ref, wd_ref, y_ref):
    del blk_e_ref
    active = pl.program_id(0) < n_used_ref[0]

    @pl.when(jnp.logical_not(active))
    def _unused_block():
        y_ref[...] = jnp.zeros(y_ref.shape, y_ref.dtype)

    @pl.when(active)
    def _():
        xb = _unpack_halves(xs_ref[...]).astype(BF16)
        g = jnp.dot(xb, wg_ref[0].astype(BF16), preferred_element_type=F32)
        u = jnp.dot(xb, wu_ref[0].astype(BF16), preferred_element_type=F32)
        hid = (g * _sigmoid(g) * u).astype(BF16)
        y_ref[...] = _pack_halves(jnp.dot(hid, wd_ref[0].astype(BF16), preferred_element_type=F32))


def _experts(blk_e, n_used, xs, w_gate, w_up, w_down, *, tm):
    n_rows, w = xs.shape
    _, d, de = w_gate.shape
    blk = lambda i, blk_e, n_used: jnp.minimum(i, n_used[0] - 1)
    row_map = lambda i, blk_e, n_used: (blk(i, blk_e, n_used), 0)
    w_map = lambda i, blk_e, n_used: (blk_e[blk(i, blk_e, n_used)], 0, 0)
    return pl.pallas_call(
        _expert_kernel,
        grid_spec=pltpu.PrefetchScalarGridSpec(
            num_scalar_prefetch=2,
            grid=(n_rows // tm,),
            in_specs=[pl.BlockSpec((tm, w), row_map),
                      pl.BlockSpec((1, d, de), w_map),
                      pl.BlockSpec((1, d, de), w_map),
                      pl.BlockSpec((1, de, d), w_map)],
            out_specs=pl.BlockSpec((tm, w), lambda i, blk_e, n_used: (i, 0))),
        out_shape=jax.ShapeDtypeStruct((n_rows, w), U32),
        compiler_params=_cparams(1),
        name="experts",
    )(blk_e, n_used, xs, w_gate, w_up, w_down)


def _combine_kernel(dest_ref, x1_ref, mf_ref, g_ref, b_ref, yp_ref, o_ref, buf, sems, *, tm, n_tiles):
    i = pl.program_id(0)

    def issue(tile, slot):
        base = tile * tm

        def body(r, carry):
            for c in range(TOP_K):
                _row_copy(yp_ref, dest_ref[TOP_K * (base + r) + c], buf.at[slot, c], r, sems.at[slot]).start()
            return carry

        lax.fori_loop(0, tm, body, 0)

    @pl.when(i == 0)
    def _first():
        issue(0, 0)

    @pl.when(i + 1 < n_tiles)
    def _next():
        issue(i + 1, (i + 1) % 2)

    slot = i % 2
    for c in range(TOP_K):
        pltpu.make_async_copy(yp_ref.at[pl.ds(0, tm), :], buf.at[slot, c], sems.at[slot]).wait()

    gates = mf_ref[...]
    ffn = (gates[:, 0:1] * _unpack_halves(buf[slot, 0]) + gates[:, 1:2] * _unpack_halves(buf[slot, 1]))
    o_ref[...] = _layer_norm(DN_ALPHA * x1_ref[...] + ffn, g_ref[...], b_ref[...])


def _combine(dest_flat, x1, mf, ln_g, ln_b, yp, *, tm):
    t, d = x1.shape
    n_tiles = t // tm
    vec = pl.BlockSpec((1, d), lambda i, dest: (0, 0))
    return pl.pallas_call(
        functools.partial(_combine_kernel, tm=tm, n_tiles=n_tiles),
        grid_spec=pltpu.PrefetchScalarGridSpec(
            num_scalar_prefetch=1,
            grid=(n_tiles,),
            in_specs=[pl.BlockSpec((tm, d), lambda i, dest: (i, 0)),
                      pl.BlockSpec((tm, LANES), lambda i, dest: (i, 0)),
                      vec, vec,
                      pl.BlockSpec(memory_space=pl.ANY)],
            out_specs=pl.BlockSpec((tm, d), lambda i, dest: (i, 0)),
            scratch_shapes=[pltpu.VMEM((2, TOP_K, tm, d // 2), U32),
                            pltpu.SemaphoreType.DMA((2,))]),
        out_shape=jax.ShapeDtypeStruct((t, d), F32),
        compiler_params=_cparams(1),
        name="combine",
    )(dest_flat, x1, mf, ln_g.reshape(1, d), ln_b.reshape(1, d), yp)


def _tile(n, pref):
    return pref if n % pref == 0 else n


def kernel(x, w_in, b_in, diff_lambda, head_norm_g, w_o_attn, rel_bias, conv_w, conv_b, conv_ln_g,
           conv_ln_b, w_conv_out, b_conv_out, w_out, ln1_g, ln1_b, router_g_w, router_g_b,
           router_e_w, router_e_b, expert_w_gate, expert_w_up, expert_w_down, ln2_g, ln2_b):
    bsz, s, d = x.shape
    t = bsz * s
    width = N_HEADS * 2 * HEAD_DIM
    assert w_in.shape[0] == DEPTH and w_in.shape[2] == 7 * width and width == d
    assert conv_w.shape[2] == width and TOP_K == 2
    tm_moe = _tile(t, 256)

    for li in range(DEPTH):
        lam_init = 0.8 - 0.6 * math.exp(-0.3 * li)
        x2 = x.reshape(t, d)
        q, k, v, hglu, ga, gb = _proj(
            x2, w_in[li].astype(BF16), b_in[li].reshape(1, -1), width=width, tm=_tile(t, 512),
            chunk=_tile(width, 512), q_scale=HEAD_DIM ** -0.5 * LOG2E)
        o_n = _attention(q.reshape(bsz, s, width), k.reshape(bsz, s, width), v.reshape(bsz, s, width),
                         rel_bias, diff_lambda[li], head_norm_g[li], tk=256 if s % 512 == 0 else 128,
                         lam_init=lam_init)
        hc = _conv(hglu.reshape(bsz, s, width), conv_w[li], conv_b[li], conv_ln_g[li], conv_ln_b[li],
                   ts=_tile(s, 256), rc=16)

        n_r = N_GROUPS + N_EXPERTS
        w_r = jnp.pad(jnp.concatenate([router_g_w[li], router_e_w[li]], axis=1),
                      ((0, 0), (0, LANES - n_r))).astype(BF16)
        b_r = jnp.pad(jnp.concatenate([router_g_b[li], router_e_b[li]]), (0, LANES - n_r)).reshape(1, LANES)
        x1, x1p, mi, mf, cnt = _mix(
            o_n.reshape(t, width), hc.reshape(t, width), ga, gb, x2,
            w_o_attn[li].astype(BF16), w_conv_out[li].astype(BF16), b_conv_out[li],
            w_out[li].astype(BF16), ln1_g[li], ln1_b[li], w_r, b_r, tm=_tile(t, 512))

        counts = cnt[0, :N_EXPERTS].astype(I32)
        padded = (counts + tm_moe - 1) // tm_moe * tm_moe
        pad_ends = jnp.cumsum(padded)
        pad_starts = pad_ends - padded
        chosen = mi[:, 0:TOP_K, None] == jnp.arange(N_EXPERTS, dtype=I32)[None, None, :]
        dest_flat = (jnp.sum(jnp.where(chosen, pad_starts[None, None, :], 0), axis=-1)
                     + mi[:, TOP_K:2 * TOP_K]).reshape(t * TOP_K)
        n_rows = t * TOP_K + N_EXPERTS * tm_moe
        n_blocks = n_rows // tm_moe
        blk_start = jnp.arange(n_blocks, dtype=I32) * tm_moe
        blk_e = jnp.minimum(jnp.sum(blk_start[:, None] >= pad_ends[None, :], axis=1),
                            N_EXPERTS - 1).astype(I32)
        n_used = (pad_ends[-1:] // tm_moe).astype(I32)

        xs = _dispatch(dest_flat, x1p, n_rows, tm=_tile(t, 512))
        yp = _experts(blk_e, n_used, xs, expert_w_gate[li], expert_w_up[li], expert_w_down[li], tm=tm_moe)
        x = _combine(dest_flat, x1, mf, ln2_g[li], ln2_b[li], yp, tm=_tile(t, 256)).reshape(bsz, s, d)
    return x
```

```python
import functools
import math

import jax
import jax.numpy as jnp
from jax import lax
from jax.experimental import pallas as pl
from jax.experimental.pallas import tpu as pltpu

F32 = jnp.float32
BF16 = jnp.bfloat16
U32 = jnp.uint32
I32 = jnp.int32

N_HEADS = 8
HEAD_DIM = 64
CONV_TAPS = 31
REL_BUCKETS = 32
REL_MAX_DIST = 128
N_GROUPS = 4
EXPERTS_PER_GROUP = 8
N_EXPERTS = N_GROUPS * EXPERTS_PER_GROUP
TOP_K = 2
DEPTH = 1
DN_ALPHA = (2.0 * DEPTH) ** 0.25
LN_EPS = 1e-5
NEG_INF = -1e30
LOG2E = 1.4426950408889634

LANES = 128
SUBLANES = 8
VMEM_LIMIT = 56 * 1024 * 1024
CONV_HALO = 32
CONV_SHIFT_ROWS = 56
ROW_CHUNKS = 4
ISSUE_UNROLL = 8


def _cparams(n_axes, flags=None):
    return pltpu.CompilerParams(dimension_semantics=("arbitrary",) * n_axes,
                                vmem_limit_bytes=VMEM_LIMIT, flags=flags)


def _sigmoid(x):
    return 1.0 / (1.0 + jnp.exp(-x))


def _layer_norm(z, g, b):
    mu = jnp.mean(z, axis=-1, keepdims=True)
    zc = z - mu
    var = jnp.mean(zc * zc, axis=-1, keepdims=True)
    return zc * lax.rsqrt(var + LN_EPS) * g + b


def _pack_halves(y):
    n = y.shape[1] // 2
    bits = pltpu.bitcast(y.astype(BF16).astype(F32), U32)
    return (bits[:, :n] >> 16) | (bits[:, n:] & jnp.uint32(0xFFFF0000))


def _unpack_halves(w):
    lo = pltpu.bitcast(w << 16, F32)
    hi = pltpu.bitcast(w & jnp.uint32(0xFFFF0000), F32)
    return jnp.concatenate([lo, hi], axis=1)


def _store_row_chunks(ref, prefix, packed, m):
    for q in range(ROW_CHUNKS):
        ref[prefix + (pl.ds(q, m, stride=ROW_CHUNKS), slice(None))] = packed[:, q * LANES:(q + 1) * LANES]


def _load_row_chunks(ref, prefix, m):
    return jnp.concatenate([ref[prefix + (pl.ds(q, m, stride=ROW_CHUNKS), slice(None))]
                            for q in range(ROW_CHUNKS)], axis=1)


def _proj_kernel(x_ref, w_ref, b_ref, q_ref, k_ref, v_ref, h_ref, ga_ref, gb_ref, *, width, chunk, q_scale):
    xb = x_ref[...].astype(BF16)

    def lin(col0):
        return (jnp.dot(xb, w_ref[:, col0:col0 + chunk], preferred_element_type=F32)
                + b_ref[:, col0:col0 + chunk])

    for c in range(0, width, chunk):
        sl = slice(c, c + chunk)
        q_ref[:, sl] = (lin(c) * q_scale).astype(BF16)
        k_ref[:, sl] = lin(width + c).astype(BF16)
        v_ref[:, sl] = lin(2 * width + c).astype(BF16)
        h_ref[:, sl] = (lin(3 * width + c) * _sigmoid(lin(4 * width + c))).astype(BF16)
        ga_ref[:, sl] = _sigmoid(lin(5 * width + c)).astype(BF16)
        gb_ref[:, sl] = _sigmoid(lin(6 * width + c)).astype(BF16)


def _proj(x2, w_in, b_in, *, width, tm, chunk, q_scale):
    t, d = x2.shape
    n_cols = w_in.shape[1]
    out = jax.ShapeDtypeStruct((t, width), BF16)
    row_spec = pl.BlockSpec((tm, width), lambda i: (i, 0))
    return pl.pallas_call(
        functools.partial(_proj_kernel, width=width, chunk=chunk, q_scale=q_scale),
        grid=(t // tm,),
        in_specs=[pl.BlockSpec((tm, d), lambda i: (i, 0)),
                  pl.BlockSpec((d, n_cols), lambda i: (0, 0), pipeline_mode=pl.Buffered(1)),
                  pl.BlockSpec((1, n_cols), lambda i: (0, 0))],
        out_specs=[row_spec] * 6,
        out_shape=[out] * 6,
        compiler_params=_cparams(1),
        name="proj",
    )(x2, w_in, b_in)


def _sublane_all(x, op):
    for shift in (4, 2, 1):
        x = op(x, pltpu.roll(x, shift, 0))
    return x


def _attn_kernel(tab_ref, bucket_ref, lam_ref, q_ref, k_ref, v_ref, g_ref, o_ref,
                 bias_scr, qz_scr, vt_scr, s0_scr, s1_scr, p0_scr, p1_scr, a0_scr, a1_scr,
                 m_scr, l_scr, acc_scr, *, tq, tk, dh, lam_init):
    h = pl.program_id(0)
    b = pl.program_id(1)
    qi = pl.program_id(2)
    hw = 2 * dh
    n_kv = vt_scr.shape[0]
    sw = tk
    n_sub = 2 * tq // sw

    @pl.when((b == 0) & (qi == 0))
    def _build_bias():
        far = tab_ref[REL_BUCKETS - 1, h]
        for t in range(3):
            for c in range(tq // sw):
                bk = bucket_ref[t, :, c * sw:(c + 1) * sw]
                acc = jnp.full((tk, sw), NEG_INF, F32)
                for r in range(REL_BUCKETS):
                    acc = jnp.where(bk == r, (tab_ref[r, h] - far) * LOG2E, acc)
                bias_scr[t, :, c * sw:(c + 1) * sw] = acc

    @pl.when(qi == 0)
    def _transpose_values():
        for jj in range(n_kv):
            vt_scr[jj] = v_ref[0, jj * tk:(jj + 1) * tk, :].astype(F32).T.astype(BF16)

    q = q_ref[0]
    lane = lax.broadcasted_iota(I32, (tq, hw), 1)
    zero = jnp.zeros_like(q)
    qz_scr[0:tq, :] = jnp.where(lane < dh, q, zero)
    qz_scr[tq:2 * tq, :] = jnp.where(lane >= dh, q, zero)
    m_scr[...] = jnp.full(m_scr.shape, NEG_INF, F32)
    l_scr[...] = jnp.zeros(l_scr.shape, F32)
    acc_scr[...] = jnp.zeros(acc_scr.shape, F32)

    s_bufs, p_bufs, a_bufs = (s0_scr, s1_scr), (p0_scr, p1_scr), (a0_scr, a1_scr)
    p1_scr[...] = jnp.zeros(p1_scr.shape, BF16)
    a1_scr[...] = jnp.ones(a1_scr.shape, F32)

    def issue_scores(j, slot):
        rows = (slice(j * tk, (j + 1) * tk) if isinstance(j, int)
                else pl.ds(pl.multiple_of(j * tk, tk), tk))
        s_bufs[slot][...] = lax.dot_general(k_ref[0, rows, :], qz_scr[...], (((1,), (1,)), ((), ())),
                                            preferred_element_type=F32)

    def accumulate(j_prev, slot_prev):
        pv = jnp.dot(vt_scr[j_prev], p_bufs[slot_prev][...], preferred_element_type=F32)
        acc3 = acc_scr[...].reshape(hw // SUBLANES, SUBLANES, 2 * tq)
        return (a_bufs[slot_prev][...][None] * acc3).reshape(hw, 2 * tq) + pv

    def step(j, bias_idx, slot, issue_next=True):
        if issue_next:
            issue_scores(j + 1, 1 - slot)
        acc_new = accumulate(jnp.maximum(j - 1, 0), 1 - slot)
        for c in range(n_sub):
            cs = slice(c * sw, (c + 1) * sw)
            s = s_bufs[slot][:, cs]
            if bias_idx is not None:
                qs = (c * sw) % tq
                s = s + bias_scr[bias_idx, :, qs:qs + sw]
            s3 = s.reshape(tk // SUBLANES, SUBLANES, sw)
            m_prev = m_scr[:, cs]
            m_new = jnp.maximum(m_prev, _sublane_all(jnp.max(s3, axis=0), jnp.maximum))
            alpha = jnp.exp2(m_prev - m_new)
            p3 = jnp.exp2(s3 - m_new[None])
            l_scr[:, cs] = alpha * l_scr[:, cs] + jnp.sum(p3, axis=0)
            p_bufs[slot][:, cs] = p3.reshape(tk, sw).astype(BF16)
            a_bufs[slot][:, cs] = alpha
            m_scr[:, cs] = m_new
        acc_scr[...] = acc_new

    issue_scores(0, 0)

    @pl.when(qi == 0)
    def _first_query_tile():
        step(0, 1, 0)
        step(1, 2, 1, issue_next=False)

    @pl.when(qi >= 1)
    def _later_query_tiles():
        def far_pair(i, carry):
            step(2 * i, None, 0)
            step(2 * i + 1, None, 1)
            return carry

        lax.fori_loop(0, qi - 1, far_pair, 0)
        step(2 * qi - 2, None, 0)
        step(2 * qi - 1, 0, 1)
        step(2 * qi, 1, 0)
        step(2 * qi + 1, 2, 1, issue_next=False)

    acc_scr[...] = accumulate(2 * qi + 1, 1)

    lv = lam_ref[...]
    lam = (jnp.exp(jnp.sum(lv[0:1] * lv[1:2], axis=-1, keepdims=True))
           - jnp.exp(jnp.sum(lv[2:3] * lv[3:4], axis=-1, keepdims=True)) + lam_init)
    inv_l = 1.0 / _sublane_all(l_scr[...], jnp.add)
    acc3 = acc_scr[...].reshape(hw // SUBLANES, SUBLANES, 2 * tq)
    o3 = acc3[:, :, 0:tq] * inv_l[None, :, 0:tq] - lam * (acc3[:, :, tq:2 * tq] * inv_l[None, :, tq:2 * tq])
    ms = _sublane_all(jnp.sum(o3 * o3, axis=0), jnp.add) * (1.0 / hw)
    y = (o3 * lax.rsqrt(ms + LN_EPS)[None]).reshape(hw, tq) * (g_ref[...] * (1.0 - lam_init))
    o_ref[0] = y.T.astype(o_ref.dtype)


def _rel_bucket(dist):
    max_exact = REL_BUCKETS // 2
    d = jnp.maximum(dist, 1).astype(F32)
    large = max_exact + (jnp.log(d / max_exact) / math.log(REL_MAX_DIST / max_exact)
                         * (REL_BUCKETS - max_exact)).astype(I32)
    large = jnp.minimum(large, REL_BUCKETS - 1)
    return jnp.where(dist < max_exact, dist, large)


def _near_buckets(tq, tk):
    tiles = []
    for rel_tile in (-1, 0, 1):
        dist = (jnp.arange(tq, dtype=I32)[None, :]
                - (rel_tile * tk + jnp.arange(tk, dtype=I32))[:, None])
        tiles.append(jnp.where(dist >= 0, _rel_bucket(jnp.maximum(dist, 0)), -1))
    return jnp.stack(tiles).astype(I32)


def _attention(q, k, v, rel_bias, diff_lambda, head_norm_g, *, tk, lam_init):
    bsz, s, width = q.shape
    hw = width // N_HEADS
    tq = 2 * tk
    assert hw == 2 * HEAD_DIM and hw % LANES == 0 and s % tq == 0
    assert tk + 1 >= REL_MAX_DIST
    kernel = functools.partial(_attn_kernel, tq=tq, tk=tk, dh=HEAD_DIM, lam_init=lam_init)
    stat = pltpu.VMEM((SUBLANES, 2 * tq), F32)
    return pl.pallas_call(
        kernel,
        grid=(N_HEADS, bsz, s // tq),
        in_specs=[pl.BlockSpec(memory_space=pltpu.SMEM),
                  pl.BlockSpec((3, tk, tq), lambda h, b, i: (0, 0, 0)),
                  pl.BlockSpec((4, HEAD_DIM), lambda h, b, i: (0, 0)),
                  pl.BlockSpec((1, tq, hw), lambda h, b, i: (b, i, h)),
                  pl.BlockSpec((1, s, hw), lambda h, b, i: (b, 0, h)),
                  pl.BlockSpec((1, s, hw), lambda h, b, i: (b, 0, h)),
                  pl.BlockSpec((hw, 1), lambda h, b, i: (0, 0))],
        out_specs=pl.BlockSpec((1, tq, hw), lambda h, b, i: (b, i, h)),
        out_shape=jax.ShapeDtypeStruct((bsz, s, width), BF16),
        scratch_shapes=[pltpu.VMEM((3, tk, tq), F32),
                        pltpu.VMEM((2 * tq, hw), BF16),
                        pltpu.VMEM((s // tk, hw, tk), BF16),
                        pltpu.VMEM((tk, 2 * tq), F32),
                        pltpu.VMEM((tk, 2 * tq), F32),
                        pltpu.VMEM((tk, 2 * tq), BF16),
                        pltpu.VMEM((tk, 2 * tq), BF16),
                        stat, stat,
                        stat, stat,
                        pltpu.VMEM((hw, 2 * tq), F32)],
        compiler_params=_cparams(3),
        name="attn",
    )(rel_bias, _near_buckets(tq, tk), diff_lambda, q, k, v, head_norm_g.reshape(hw, 1))


def _conv_kernel(h_ref, halo_ref, w_ref, b_ref, g_ref, bb_ref, o_ref, sh, *, ts, rc):
    i = pl.program_id(1)
    n_ext = CONV_HALO + ts
    n_sh = n_ext - SUBLANES
    halo = halo_ref[0].astype(F32)
    sh[0, 0:CONV_HALO, :] = jnp.where(i > 0, halo, jnp.zeros_like(halo))
    sh[0, CONV_HALO:n_ext, :] = h_ref[0].astype(F32)
    for k in range(1, SUBLANES):
        for r0 in range(0, n_sh, CONV_SHIFT_ROWS):
            n = min(CONV_SHIFT_ROWS, n_sh - r0)
            sh[k, r0:r0 + n, :] = sh[0, r0 + k:r0 + k + n, :]
    off = CONV_HALO - (CONV_TAPS - 1)
    for r0 in range(0, ts, rc):
        acc = None
        for t in range(CONV_TAPS):
            k, base = (off + t) % SUBLANES, (off + t) // SUBLANES * SUBLANES
            slab = sh[k, r0 + base:r0 + base + rc, :].reshape(rc // SUBLANES, SUBLANES, -1)
            term = slab * w_ref[t][None]
            acc = term if acc is None else acc + term
        y = _layer_norm(acc.reshape(rc, -1) + b_ref[...], g_ref[...], bb_ref[...])
        o_ref[0, r0:r0 + rc, :] = (y * _sigmoid(y)).astype(o_ref.dtype)


def _conv(hglu, conv_w, conv_b, ln_g, ln_b, *, ts, rc):
    bsz, s, c = hglu.shape
    assert s % ts == 0 and ts % CONV_HALO == 0 and CONV_HALO >= CONV_TAPS - 1
    blocks_per_tile = ts // CONV_HALO
    vec = pl.BlockSpec((1, c), lambda b, i: (0, 0))
    return pl.pallas_call(
        functools.partial(_conv_kernel, ts=ts, rc=rc),
        grid=(bsz, s // ts),
        in_specs=[pl.BlockSpec((1, ts, c), lambda b, i: (b, i, 0)),
                  pl.BlockSpec((1, CONV_HALO, c),
                               lambda b, i: (b, jnp.maximum(i * blocks_per_tile - 1, 0), 0)),
                  pl.BlockSpec((CONV_TAPS, SUBLANES, c), lambda b, i: (0, 0, 0)),
                  vec, vec, vec],
        out_specs=pl.BlockSpec((1, ts, c), lambda b, i: (b, i, 0)),
        out_shape=jax.ShapeDtypeStruct((bsz, s, c), BF16),
        scratch_shapes=[pltpu.VMEM((SUBLANES, CONV_HALO + ts, c), F32)],
        compiler_params=_cparams(2),
        name="conv",
    )(hglu, hglu, jnp.broadcast_to(conv_w[:, None, :], (CONV_TAPS, SUBLANES, c)),
      conv_b.reshape(1, c), ln_g.reshape(1, c), ln_b.reshape(1, c))


def _mix_kernel(o_ref, hc_ref, ga_ref, gb_ref, x_ref, woa_ref, wco_ref, bco_ref, wout_ref,
                g1_ref, b1_ref, wr_ref, br_ref, tri_ref,
                x1_ref, x1p_ref, mi_ref, mf_ref, cnt_ref, carry_scr, *, tm):
    i = pl.program_id(0)

    @pl.when(i == 0)
    def _init():
        carry_scr[...] = jnp.zeros(carry_scr.shape, F32)

    y_a = jnp.dot(o_ref[...], woa_ref[...], preferred_element_type=F32)
    y_b = jnp.dot(hc_ref[...], wco_ref[...], preferred_element_type=F32) + bco_ref[...]
    merged = ga_ref[...].astype(F32) * y_a + gb_ref[...].astype(F32) * y_b
    mixed = jnp.dot(merged.astype(BF16), wout_ref[...], preferred_element_type=F32)
    x1 = _layer_norm(DN_ALPHA * x_ref[...] + mixed, g1_ref[...], b1_ref[...])
    x1_ref[...] = x1
    _store_row_chunks(x1p_ref, (), _pack_halves(x1), tm)

    logits = jnp.dot(x1.astype(BF16), wr_ref[...], preferred_element_type=F32) + br_ref[...]
    lane = lax.broadcasted_iota(I32, (tm, LANES), 1)
    g_lane = lane < N_GROUPS
    gl = jnp.where(g_lane, logits, NEG_INF)
    g_max = jnp.max(gl, axis=-1, keepdims=True)
    g_idx = jnp.min(jnp.where(gl == g_max, lane, LANES), axis=-1, keepdims=True)
    p_g = 1.0 / jnp.sum(jnp.where(g_lane, jnp.exp(gl - g_max), 0.0), axis=-1, keepdims=True)
    lo = N_GROUPS + g_idx * EXPERTS_PER_GROUP
    el = jnp.where(lane >= lo, jnp.where(lane < lo + EXPERTS_PER_GROUP, logits, NEG_INF), NEG_INF)
    v1 = jnp.max(el, axis=-1, keepdims=True)
    i1 = jnp.min(jnp.where(el == v1, lane, LANES), axis=-1, keepdims=True)
    el2 = jnp.where(lane == i1, NEG_INF, el)
    v2 = jnp.max(el2, axis=-1, keepdims=True)
    i2 = jnp.min(jnp.where(lane == i1, LANES, jnp.where(el2 == v2, lane, LANES)), axis=-1, keepdims=True)
    ex = jnp.exp(v2 - v1)
    w1 = 1.0 / (1.0 + ex)
    gate1 = p_g * w1
    gate2 = p_g * (ex * w1)
    e1 = i1 - N_GROUPS
    e2 = i2 - N_GROUPS

    carry = carry_scr[...]
    tri = tri_ref[...]
    ranks = []
    for e in (e1, e2):
        hit = lane == e
        oh = jnp.where(hit, 1.0, 0.0)
        before = jnp.dot(tri, oh.astype(BF16), preferred_element_type=F32) + carry
        ranks.append(jnp.sum(jnp.where(hit, before, 0.0), axis=-1, keepdims=True).astype(I32))
        carry = carry + jnp.sum(oh, axis=0, keepdims=True)
    carry_scr[...] = carry
    cnt_ref[...] = carry

    mi_ref[...] = jnp.where(lane == 0, e1, jnp.where(lane == 1, e2,
                            jnp.where(lane == 2, ranks[0], jnp.where(lane == 3, ranks[1], 0))))
    mf_ref[...] = jnp.where(lane == 0, gate1, jnp.where(lane == 1, gate2, 0.0))


def _mix(o_n, hc, ga, gb, x2, w_oa, w_co, b_co, w_out, ln_g, ln_b, w_r, b_r, *, tm):
    t, d = x2.shape
    assert d // 2 == ROW_CHUNKS * LANES
    row = lambda w: pl.BlockSpec((tm, w), lambda i: (i, 0))
    full = lambda a: pl.BlockSpec(a.shape, lambda i: (0,) * a.ndim, pipeline_mode=pl.Buffered(1))
    tri = (jnp.arange(tm)[:, None] > jnp.arange(tm)[None, :]).astype(BF16)
    b_co, ln_g, ln_b = b_co.reshape(1, d), ln_g.reshape(1, d), ln_b.reshape(1, d)
    return pl.pallas_call(
        functools.partial(_mix_kernel, tm=tm),
        grid=(t // tm,),
        in_specs=[row(d), row(d), row(d), row(d), row(d),
                  full(w_oa), full(w_co), full(b_co), full(w_out), full(ln_g), full(ln_b),
                  full(w_r), full(b_r), full(tri)],
        out_specs=[row(d), pl.BlockSpec((tm * ROW_CHUNKS, LANES), lambda i: (i, 0)), row(LANES), row(LANES),
                   pl.BlockSpec((1, LANES), lambda i: (0, 0))],
        out_shape=[jax.ShapeDtypeStruct((t, d), F32),
                   jax.ShapeDtypeStruct((t * ROW_CHUNKS, LANES), U32),
                   jax.ShapeDtypeStruct((t, LANES), I32),
                   jax.ShapeDtypeStruct((t, LANES), F32),
                   jax.ShapeDtypeStruct((1, LANES), F32)],
        scratch_shapes=[pltpu.VMEM((1, LANES), F32)],
        compiler_params=_cparams(1),
        name="mix",
    )(o_n, hc, ga, gb, x2, w_oa, w_co, b_co, w_out, ln_g, ln_b, w_r, b_r, tri)


def _row_copy(src, src_row, dst, dst_row, sem):
    def piece(row):
        return pl.ds(pl.multiple_of(row * ROW_CHUNKS, ROW_CHUNKS), ROW_CHUNKS)
    return pltpu.make_async_copy(src.at[piece(src_row), :], dst.at[piece(dst_row), :], sem)


def _dispatch_kernel(dest_ref, x_ref, xs_in_ref, xs_ref, sem, *, tm):
    del xs_in_ref
    base = pl.program_id(0) * tm

    def issue(g, carry):
        for u in range(ISSUE_UNROLL):
            r = g * ISSUE_UNROLL + u
            for c in range(TOP_K):
                _row_copy(x_ref, r, xs_ref, dest_ref[TOP_K * (base + r) + c], sem).start()
        return carry

    lax.fori_loop(0, tm // ISSUE_UNROLL, issue, 0)
    for _ in range(TOP_K):
        pltpu.make_async_copy(x_ref, xs_ref.at[pl.ds(0, tm * ROW_CHUNKS), :], sem).wait()


def _dispatch(dest_flat, x1p, n_rows, *, tm):
    w = x1p.shape[1]
    t = x1p.shape[0] // ROW_CHUNKS
    assert tm % ISSUE_UNROLL == 0
    return pl.pallas_call(
        functools.partial(_dispatch_kernel, tm=tm),
        grid_spec=pltpu.PrefetchScalarGridSpec(
            num_scalar_prefetch=1,
            grid=(t // tm,),
            in_specs=[pl.BlockSpec((tm * ROW_CHUNKS, w), lambda i, dest: (i, 0)),
                      pl.BlockSpec(memory_space=pl.ANY)],
            out_specs=pl.BlockSpec(memory_space=pl.ANY),
            scratch_shapes=[pltpu.SemaphoreType.DMA(())]),
        out_shape=jax.ShapeDtypeStruct((n_rows * ROW_CHUNKS, w), U32),
        input_output_aliases={2: 0},
        compiler_params=_cparams(1),
        name="dispatch",
    )(dest_flat, x1p, jnp.zeros((n_rows * ROW_CHUNKS, w), U32))


def _expert_kernel(blk_e_ref, n_used_ref, xs_ref, wg_ref, wu_ref, wd_ref, y_ref):
    del blk_e_ref
    active = pl.program_id(0) < n_used_ref[0]
    tm = xs_ref.shape[0] // ROW_CHUNKS

    @pl.when(jnp.logical_not(active))
    def _unused_block():
        y_ref[...] = jnp.zeros(y_ref.shape, y_ref.dtype)

    @pl.when(active)
    def _():
        xb = _unpack_halves(_load_row_chunks(xs_ref, (), tm)).astype(BF16)
        g = jnp.dot(xb, wg_ref[0].astype(BF16), preferred_element_type=F32)
        u = jnp.dot(xb, wu_ref[0].astype(BF16), preferred_element_type=F32)
        hid = (g * _sigmoid(g) * u).astype(BF16)
        y = jnp.dot(hid, wd_ref[0].astype(BF16), preferred_element_type=F32)
        _store_row_chunks(y_ref, (), _pack_halves(y), tm)


def _experts(blk_e, n_used, xs, w_gate, w_up, w_down, *, tm):
    w = xs.shape[1]
    n_rows = xs.shape[0] // ROW_CHUNKS
    _, d, de = w_gate.shape
    blk = lambda i, blk_e, n_used: jnp.minimum(i, n_used[0] - 1)
    row_map = lambda i, blk_e, n_used: (blk(i, blk_e, n_used), 0)
    w_map = lambda i, blk_e, n_used: (blk_e[blk(i, blk_e, n_used)], 0, 0)
    return pl.pallas_call(
        _expert_kernel,
        grid_spec=pltpu.PrefetchScalarGridSpec(
            num_scalar_prefetch=2,
            grid=(n_rows // tm,),
            in_specs=[pl.BlockSpec((tm * ROW_CHUNKS, w), row_map),
                      pl.BlockSpec((1, d, de), w_map),
                      pl.BlockSpec((1, d, de), w_map),
                      pl.BlockSpec((1, de, d), w_map)],
            out_specs=pl.BlockSpec((tm * ROW_CHUNKS, w), lambda i, blk_e, n_used: (i, 0))),
        out_shape=jax.ShapeDtypeStruct((n_rows * ROW_CHUNKS, w), U32),
        compiler_params=_cparams(1),
        name="experts",
    )(blk_e, n_used, xs, w_gate, w_up, w_down)


def _combine_kernel(dest_ref, x1_ref, mf_ref, g_ref, b_ref, yp_ref, o_ref, buf, sems, *, tm, n_tiles):
    i = pl.program_id(0)

    def issue(tile, slot):
        base = tile * tm

        def body(g, carry):
            for u in range(ISSUE_UNROLL):
                r = g * ISSUE_UNROLL + u
                for c in range(TOP_K):
                    _row_copy(yp_ref, dest_ref[TOP_K * (base + r) + c], buf.at[slot, c], r, sems.at[slot]).start()
            return carry

        lax.fori_loop(0, tm // ISSUE_UNROLL, body, 0)

    @pl.when(i == 0)
    def _first():
        issue(0, 0)

    @pl.when(i + 1 < n_tiles)
    def _next():
        issue(i + 1, (i + 1) % 2)

    slot = i % 2
    for c in range(TOP_K):
        pltpu.make_async_copy(yp_ref.at[pl.ds(0, tm * ROW_CHUNKS), :], buf.at[slot, c], sems.at[slot]).wait()

    gates = mf_ref[...]
    ffn = (gates[:, 0:1] * _unpack_halves(_load_row_chunks(buf, (slot, 0), tm))
           + gates[:, 1:2] * _unpack_halves(_load_row_chunks(buf, (slot, 1), tm)))
    o_ref[...] = _layer_norm(DN_ALPHA * x1_ref[...] + ffn, g_ref[...], b_ref[...])


def _combine(dest_flat, x1, mf, ln_g, ln_b, yp, *, tm):
    t, d = x1.shape
    n_tiles = t // tm
    assert tm % ISSUE_UNROLL == 0
    vec = pl.BlockSpec((1, d), lambda i, dest: (0, 0))
    return pl.pallas_call(
        functools.partial(_combine_kernel, tm=tm, n_tiles=n_tiles),
        grid_spec=pltpu.PrefetchScalarGridSpec(
            num_scalar_prefetch=1,
            grid=(n_tiles,),
            in_specs=[pl.BlockSpec((tm, d), lambda i, dest: (i, 0)),
                      pl.BlockSpec((tm, LANES), lambda i, dest: (i, 0)),
                      vec, vec,
                      pl.BlockSpec(memory_space=pl.ANY)],
            out_specs=pl.BlockSpec((tm, d), lambda i, dest: (i, 0)),
            scratch_shapes=[pltpu.VMEM((2, TOP_K, tm * ROW_CHUNKS, LANES), U32),
                            pltpu.SemaphoreType.DMA((2,))]),
        out_shape=jax.ShapeDtypeStruct((t, d), F32),
        compiler_params=_cparams(1),
        name="combine",
    )(dest_flat, x1, mf, ln_g.reshape(1, d), ln_b.reshape(1, d), yp)


def _tile(n, pref):
    return pref if n % pref == 0 else n


def kernel(x, w_in, b_in, diff_lambda, head_norm_g, w_o_attn, rel_bias, conv_w, conv_b, conv_ln_g,
           conv_ln_b, w_conv_out, b_conv_out, w_out, ln1_g, ln1_b, router_g_w, router_g_b,
           router_e_w, router_e_b, expert_w_gate, expert_w_up, expert_w_down, ln2_g, ln2_b):
    bsz, s, d = x.shape
    t = bsz * s
    width = N_HEADS * 2 * HEAD_DIM
    assert w_in.shape[0] == DEPTH and w_in.shape[2] == 7 * width and width == d
    assert conv_w.shape[2] == width and TOP_K == 2
    tm_moe = _tile(t, 256)

    for li in range(DEPTH):
        lam_init = 0.8 - 0.6 * math.exp(-0.3 * li)
        x2 = x.reshape(t, d)
        q, k, v, hglu, ga, gb = _proj(
            x2, w_in[li].astype(BF16), b_in[li].reshape(1, -1), width=width, tm=_tile(t, 512),
            chunk=_tile(width, 512), q_scale=HEAD_DIM ** -0.5 * LOG2E)
        o_n = _attention(q.reshape(bsz, s, width), k.reshape(bsz, s, width), v.reshape(bsz, s, width),
                         rel_bias, diff_lambda[li], head_norm_g[li], tk=256 if s % 512 == 0 else 128,
                         lam_init=lam_init)
        hc = _conv(hglu.reshape(bsz, s, width), conv_w[li], conv_b[li], conv_ln_g[li], conv_ln_b[li],
                   ts=_tile(s, 256), rc=16)

        n_r = N_GROUPS + N_EXPERTS
        w_r = jnp.pad(jnp.concatenate([router_g_w[li], router_e_w[li]], axis=1),
                      ((0, 0), (0, LANES - n_r))).astype(BF16)
        b_r = jnp.pad(jnp.concatenate([router_g_b[li], router_e_b[li]]), (0, LANES - n_r)).reshape(1, LANES)
        x1, x1p, mi, mf, cnt = _mix(
            o_n.reshape(t, width), hc.reshape(t, width), ga, gb, x2,
            w_o_attn[li].astype(BF16), w_conv_out[li].astype(BF16), b_conv_out[li],
            w_out[li].astype(BF16), ln1_g[li], ln1_b[li], w_r, b_r, tm=_tile(t, 512))

        counts = cnt[0, :N_EXPERTS].astype(I32)
        padded = (counts + tm_moe - 1) // tm_moe * tm_moe
        pad_ends = jnp.cumsum(padded)
        pad_starts = pad_ends - padded
        chosen = mi[:, 0:TOP_K, None] == jnp.arange(N_EXPERTS, dtype=I32)[None, None, :]
        dest_flat = (jnp.sum(jnp.where(chosen, pad_starts[None, None, :], 0), axis=-1)
                     + mi[:, TOP_K:2 * TOP_K]).reshape(t * TOP_K)
        n_rows = t * TOP_K + N_EXPERTS * tm_moe
        n_blocks = n_rows // tm_moe
        blk_start = jnp.arange(n_blocks, dtype=I32) * tm_moe
        blk_e = jnp.minimum(jnp.sum(blk_start[:, None] >= pad_ends[None, :], axis=1),
                            N_EXPERTS - 1).astype(I32)
        n_used = (pad_ends[-1:] // tm_moe).astype(I32)

        xs = _dispatch(dest_flat, x1p, n_rows, tm=_tile(t, 512))
        yp = _experts(blk_e, n_used, xs, expert_w_gate[li], expert_w_up[li], expert_w_down[li], tm=tm_moe)
        x = _combine(dest_flat, x1, mf, ln2_g[li], ln2_b[li], yp, tm=_tile(t, 256)).reshape(bsz, s, d)
    return x
```

```python
import functools
import math

import jax
import jax.numpy as jnp
from jax import lax
from jax.experimental import pallas as pl
from jax.experimental.pallas import tpu as pltpu

F32 = jnp.float32
BF16 = jnp.bfloat16
U32 = jnp.uint32
I32 = jnp.int32

N_HEADS = 8
HEAD_DIM = 64
CONV_TAPS = 31
REL_BUCKETS = 32
REL_MAX_DIST = 128
N_GROUPS = 4
EXPERTS_PER_GROUP = 8
N_EXPERTS = N_GROUPS * EXPERTS_PER_GROUP
TOP_K = 2
DEPTH = 1
DN_ALPHA = (2.0 * DEPTH) ** 0.25
LN_EPS = 1e-5
NEG_INF = -1e30
LOG2E = 1.4426950408889634

LANES = 128
SUBLANES = 8
VMEM_LIMIT = 56 * 1024 * 1024
CONV_HALO = 32
CONV_SHIFT_ROWS = 56
ROW_CHUNKS = 4
ISSUE_UNROLL = 8


def _cparams(n_axes, flags=None):
    return pltpu.CompilerParams(dimension_semantics=("arbitrary",) * n_axes,
                                vmem_limit_bytes=VMEM_LIMIT, flags=flags)


def _sigmoid(x):
    return 1.0 / (1.0 + jnp.exp(-x))


def _layer_norm(z, g, b):
    mu = jnp.mean(z, axis=-1, keepdims=True)
    zc = z - mu
    var = jnp.mean(zc * zc, axis=-1, keepdims=True)
    return zc * lax.rsqrt(var + LN_EPS) * g + b


def _pack_halves(y):
    n = y.shape[1] // 2
    bits = pltpu.bitcast(y.astype(BF16).astype(F32), U32)
    return (bits[:, :n] >> 16) | (bits[:, n:] & jnp.uint32(0xFFFF0000))


def _unpack_halves(w):
    lo = pltpu.bitcast(w << 16, F32)
    hi = pltpu.bitcast(w & jnp.uint32(0xFFFF0000), F32)
    return jnp.concatenate([lo, hi], axis=1)


def _store_row_chunks(ref, prefix, packed, m):
    for q in range(ROW_CHUNKS):
        ref[prefix + (pl.ds(q, m, stride=ROW_CHUNKS), slice(None))] = packed[:, q * LANES:(q + 1) * LANES]


def _load_row_chunks(ref, prefix, m):
    return jnp.concatenate([ref[prefix + (pl.ds(q, m, stride=ROW_CHUNKS), slice(None))]
                            for q in range(ROW_CHUNKS)], axis=1)


def _proj_kernel(x_ref, w_ref, b_ref, q_ref, k_ref, v_ref, h_ref, ga_ref, gb_ref, *, width, chunk, q_scale):
    xb = x_ref[...].astype(BF16)

    def lin(col0):
        return (jnp.dot(xb, w_ref[:, col0:col0 + chunk], preferred_element_type=F32)
                + b_ref[:, col0:col0 + chunk])

    for c in range(0, width, chunk):
        sl = slice(c, c + chunk)
        q_ref[:, sl] = (lin(c) * q_scale).astype(BF16)
        k_ref[:, sl] = lin(width + c).astype(BF16)
        v_ref[:, sl] = lin(2 * width + c).astype(BF16)
        h_ref[:, sl] = (lin(3 * width + c) * _sigmoid(lin(4 * width + c))).astype(BF16)
        ga_ref[:, sl] = _sigmoid(lin(5 * width + c)).astype(BF16)
        gb_ref[:, sl] = _sigmoid(lin(6 * width + c)).astype(BF16)


def _proj(x2, w_in, b_in, *, width, tm, chunk, q_scale):
    t, d = x2.shape
    n_cols = w_in.shape[1]
    out = jax.ShapeDtypeStruct((t, width), BF16)
    row_spec = pl.BlockSpec((tm, width), lambda i: (i, 0))
    return pl.pallas_call(
        functools.partial(_proj_kernel, width=width, chunk=chunk, q_scale=q_scale),
        grid=(t // tm,),
        in_specs=[pl.BlockSpec((tm, d), lambda i: (i, 0)),
                  pl.BlockSpec((d, n_cols), lambda i: (0, 0), pipeline_mode=pl.Buffered(1)),
                  pl.BlockSpec((1, n_cols), lambda i: (0, 0))],
        out_specs=[row_spec] * 6,
        out_shape=[out] * 6,
        compiler_params=_cparams(1),
        name="proj",
    )(x2, w_in, b_in)


def _sublane_all(x, op):
    for shift in (4, 2, 1):
        x = op(x, pltpu.roll(x, shift, 0))
    return x


def _attn_kernel(tab_ref, bucket_ref, lam_ref, q_ref, k_ref, v_ref, g_ref, o_ref,
                 bias_scr, qz_scr, vt_scr, s0_scr, s1_scr, p0_scr, p1_scr, a0_scr, a1_scr,
                 m_scr, l_scr, acc_scr, *, tq, tk, dh, lam_init):
    h = pl.program_id(0)
    b = pl.program_id(1)
    qi = pl.program_id(2)
    hw = 2 * dh
    n_kv = vt_scr.shape[0]
    sw = tk
    n_sub = 2 * tq // sw

    @pl.when((b == 0) & (qi == 0))
    def _build_bias():
        far = tab_ref[REL_BUCKETS - 1, h]
        for t in range(3):
            for c in range(tq // sw):
                bk = bucket_ref[t, :, c * sw:(c + 1) * sw]
                acc = jnp.full((tk, sw), NEG_INF, F32)
                for r in range(REL_BUCKETS):
                    acc = jnp.where(bk == r, (tab_ref[r, h] - far) * LOG2E, acc)
                bias_scr[t, :, c * sw:(c + 1) * sw] = acc

    @pl.when(qi == 0)
    def _transpose_values():
        for jj in range(n_kv):
            vt_scr[jj] = v_ref[0, jj * tk:(jj + 1) * tk, :].astype(F32).T.astype(BF16)

    q = q_ref[0]
    lane = lax.broadcasted_iota(I32, (tq, hw), 1)
    zero = jnp.zeros_like(q)
    qz_scr[0:tq, :] = jnp.where(lane < dh, q, zero)
    qz_scr[tq:2 * tq, :] = jnp.where(lane >= dh, q, zero)
    m_scr[...] = jnp.full(m_scr.shape, NEG_INF, F32)
    l_scr[...] = jnp.zeros(l_scr.shape, F32)
    acc_scr[...] = jnp.zeros(acc_scr.shape, F32)

    s_bufs, p_bufs, a_bufs = (s0_scr, s1_scr), (p0_scr, p1_scr), (a0_scr, a1_scr)
    p1_scr[...] = jnp.zeros(p1_scr.shape, BF16)
    a1_scr[...] = jnp.ones(a1_scr.shape, F32)

    def issue_scores(j, slot):
        rows = (slice(j * tk, (j + 1) * tk) if isinstance(j, int)
                else pl.ds(pl.multiple_of(j * tk, tk), tk))
        s_bufs[slot][...] = lax.dot_general(k_ref[0, rows, :], qz_scr[...], (((1,), (1,)), ((), ())),
                                            preferred_element_type=F32)

    def accumulate(j_prev, slot_prev):
        pv = jnp.dot(vt_scr[j_prev], p_bufs[slot_prev][...], preferred_element_type=F32)
        acc3 = acc_scr[...].reshape(hw // SUBLANES, SUBLANES, 2 * tq)
        return (a_bufs[slot_prev][...][None] * acc3).reshape(hw, 2 * tq) + pv

    def step(j, bias_idx, slot, issue_next=True):
        if issue_next:
            issue_scores(j + 1, 1 - slot)
        acc_new = accumulate(jnp.maximum(j - 1, 0), 1 - slot)
        for c in range(n_sub):
            cs = slice(c * sw, (c + 1) * sw)
            s = s_bufs[slot][:, cs]
            if bias_idx is not None:
                qs = (c * sw) % tq
                s = s + bias_scr[bias_idx, :, qs:qs + sw]
            s3 = s.reshape(tk // SUBLANES, SUBLANES, sw)
            m_prev = m_scr[:, cs]
            m_new = jnp.maximum(m_prev, _sublane_all(jnp.max(s3, axis=0), jnp.maximum))
            alpha = jnp.exp2(m_prev - m_new)
            p3 = jnp.exp2(s3 - m_new[None])
            l_scr[:, cs] = alpha * l_scr[:, cs] + jnp.sum(p3, axis=0)
            p_bufs[slot][:, cs] = p3.reshape(tk, sw).astype(BF16)
            a_bufs[slot][:, cs] = alpha
            m_scr[:, cs] = m_new
        acc_scr[...] = acc_new

    issue_scores(0, 0)

    @pl.when(qi == 0)
    def _first_query_tile():
        step(0, 1, 0)
        step(1, 2, 1, issue_next=False)

    @pl.when(qi >= 1)
    def _later_query_tiles():
        def far_pair(i, carry):
            step(2 * i, None, 0)
            step(2 * i + 1, None, 1)
            return carry

        lax.fori_loop(0, qi - 1, far_pair, 0)
        step(2 * qi - 2, None, 0)
        step(2 * qi - 1, 0, 1)
        step(2 * qi, 1, 0)
        step(2 * qi + 1, 2, 1, issue_next=False)

    acc_scr[...] = accumulate(2 * qi + 1, 1)

    lv = lam_ref[...]
    lam = (jnp.exp(jnp.sum(lv[0:1] * lv[1:2], axis=-1, keepdims=True))
           - jnp.exp(jnp.sum(lv[2:3] * lv[3:4], axis=-1, keepdims=True)) + lam_init)
    inv_l = 1.0 / _sublane_all(l_scr[...], jnp.add)
    acc3 = acc_scr[...].reshape(hw // SUBLANES, SUBLANES, 2 * tq)
    o3 = acc3[:, :, 0:tq] * inv_l[None, :, 0:tq] - lam * (acc3[:, :, tq:2 * tq] * inv_l[None, :, tq:2 * tq])
    ms = _sublane_all(jnp.sum(o3 * o3, axis=0), jnp.add) * (1.0 / hw)
    y = (o3 * lax.rsqrt(ms + LN_EPS)[None]).reshape(hw, tq) * (g_ref[...] * (1.0 - lam_init))
    o_ref[0] = y.T.astype(o_ref.dtype)


def _rel_bucket(dist):
    max_exact = REL_BUCKETS // 2
    d = jnp.maximum(dist, 1).astype(F32)
    large = max_exact + (jnp.log(d / max_exact) / math.log(REL_MAX_DIST / max_exact)
                         * (REL_BUCKETS - max_exact)).astype(I32)
    large = jnp.minimum(large, REL_BUCKETS - 1)
    return jnp.where(dist < max_exact, dist, large)


def _near_buckets(tq, tk):
    tiles = []
    for rel_tile in (-1, 0, 1):
        dist = (jnp.arange(tq, dtype=I32)[None, :]
                - (rel_tile * tk + jnp.arange(tk, dtype=I32))[:, None])
        tiles.append(jnp.where(dist >= 0, _rel_bucket(jnp.maximum(dist, 0)), -1))
    return jnp.stack(tiles).astype(I32)


def _attention(q, k, v, rel_bias, diff_lambda, head_norm_g, *, tk, lam_init):
    bsz, s, width = q.shape
    hw = width // N_HEADS
    tq = 2 * tk
    assert hw == 2 * HEAD_DIM and hw % LANES == 0 and s % tq == 0
    assert tk + 1 >= REL_MAX_DIST
    kernel = functools.partial(_attn_kernel, tq=tq, tk=tk, dh=HEAD_DIM, lam_init=lam_init)
    stat = pltpu.VMEM((SUBLANES, 2 * tq), F32)
    return pl.pallas_call(
        kernel,
        grid=(N_HEADS, bsz, s // tq),
        in_specs=[pl.BlockSpec(memory_space=pltpu.SMEM),
                  pl.BlockSpec((3, tk, tq), lambda h, b, i: (0, 0, 0)),
                  pl.BlockSpec((4, HEAD_DIM), lambda h, b, i: (0, 0)),
                  pl.BlockSpec((1, tq, hw), lambda h, b, i: (b, i, h)),
                  pl.BlockSpec((1, s, hw), lambda h, b, i: (b, 0, h)),
                  pl.BlockSpec((1, s, hw), lambda h, b, i: (b, 0, h)),
                  pl.BlockSpec((hw, 1), lambda h, b, i: (0, 0))],
        out_specs=pl.BlockSpec((1, tq, hw), lambda h, b, i: (b, i, h)),
        out_shape=jax.ShapeDtypeStruct((bsz, s, width), BF16),
        scratch_shapes=[pltpu.VMEM((3, tk, tq), F32),
                        pltpu.VMEM((2 * tq, hw), BF16),
                        pltpu.VMEM((s // tk, hw, tk), BF16),
                        pltpu.VMEM((tk, 2 * tq), F32),
                        pltpu.VMEM((tk, 2 * tq), F32),
                        pltpu.VMEM((tk, 2 * tq), BF16),
                        pltpu.VMEM((tk, 2 * tq), BF16),
                        stat, stat,
                        stat, stat,
                        pltpu.VMEM((hw, 2 * tq), F32)],
        compiler_params=_cparams(3),
        name="attn",
    )(rel_bias, _near_buckets(tq, tk), diff_lambda, q, k, v, head_norm_g.reshape(hw, 1))


def _conv_kernel(h_ref, halo_ref, w_ref, b_ref, g_ref, bb_ref, o_ref, sh, *, ts, rc):
    i = pl.program_id(1)
    n_ext = CONV_HALO + ts
    n_sh = n_ext - SUBLANES
    halo = halo_ref[0].astype(F32)
    sh[0, 0:CONV_HALO, :] = jnp.where(i > 0, halo, jnp.zeros_like(halo))
    sh[0, CONV_HALO:n_ext, :] = h_ref[0].astype(F32)
    for k in range(1, SUBLANES):
        for r0 in range(0, n_sh, CONV_SHIFT_ROWS):
            n = min(CONV_SHIFT_ROWS, n_sh - r0)
            sh[k, r0:r0 + n, :] = sh[0, r0 + k:r0 + k + n, :]
    off = CONV_HALO - (CONV_TAPS - 1)
    for r0 in range(0, ts, rc):
        acc = None
        for t in range(CONV_TAPS):
            k, base = (off + t) % SUBLANES, (off + t) // SUBLANES * SUBLANES
            slab = sh[k, r0 + base:r0 + base + rc, :].reshape(rc // SUBLANES, SUBLANES, -1)
            term = slab * w_ref[t][None]
            acc = term if acc is None else acc + term
        y = _layer_norm(acc.reshape(rc, -1) + b_ref[...], g_ref[...], bb_ref[...])
        o_ref[0, r0:r0 + rc, :] = (y * _sigmoid(y)).astype(o_ref.dtype)


def _conv(hglu, conv_w, conv_b, ln_g, ln_b, *, ts, rc):
    bsz, s, c = hglu.shape
    assert s % ts == 0 and ts % CONV_HALO == 0 and CONV_HALO >= CONV_TAPS - 1
    blocks_per_tile = ts // CONV_HALO
    vec = pl.BlockSpec((1, c), lambda b, i: (0, 0))
    return pl.pallas_call(
        functools.partial(_conv_kernel, ts=ts, rc=rc),
        grid=(bsz, s // ts),
        in_specs=[pl.BlockSpec((1, ts, c), lambda b, i: (b, i, 0)),
                  pl.BlockSpec((1, CONV_HALO, c),
                               lambda b, i: (b, jnp.maximum(i * blocks_per_tile - 1, 0), 0)),
                  pl.BlockSpec((CONV_TAPS, SUBLANES, c), lambda b, i: (0, 0, 0)),
                  vec, vec, vec],
        out_specs=pl.BlockSpec((1, ts, c), lambda b, i: (b, i, 0)),
        out_shape=jax.ShapeDtypeStruct((bsz, s, c), BF16),
        scratch_shapes=[pltpu.VMEM((SUBLANES, CONV_HALO + ts, c), F32)],
        compiler_params=_cparams(2),
        name="conv",
    )(hglu, hglu, jnp.broadcast_to(conv_w[:, None, :], (CONV_TAPS, SUBLANES, c)),
      conv_b.reshape(1, c), ln_g.reshape(1, c), ln_b.reshape(1, c))


def _mix_kernel(o_ref, hc_ref, ga_ref, gb_ref, x_ref, woa_ref, wco_ref, bco_ref, wout_ref,
                g1_ref, b1_ref, wr_ref, br_ref, tri_ref,
                x1_ref, x1p_ref, mi_ref, mf_ref, cnt_ref, carry_scr, *, tm):
    i = pl.program_id(0)

    @pl.when(i == 0)
    def _init():
        carry_scr[...] = jnp.zeros(carry_scr.shape, F32)

    halves = [slice(0, tm // 2), slice(tm // 2, tm)]
    branch = [(jnp.dot(o_ref[h, :], woa_ref[...], preferred_element_type=F32),
               jnp.dot(hc_ref[h, :], wco_ref[...], preferred_element_type=F32) + bco_ref[...]) for h in halves]
    mixed = [jnp.dot((ga_ref[h, :].astype(F32) * y_a + gb_ref[h, :].astype(F32) * y_b).astype(BF16),
                     wout_ref[...], preferred_element_type=F32) for h, (y_a, y_b) in zip(halves, branch)]
    x1_halves = [_layer_norm(DN_ALPHA * x_ref[h, :] + mx, g1_ref[...], b1_ref[...]) for h, mx in zip(halves, mixed)]
    logits = jnp.concatenate([jnp.dot(xh.astype(BF16), wr_ref[...], preferred_element_type=F32)
                              for xh in x1_halves], axis=0) + br_ref[...]
    x1 = jnp.concatenate(x1_halves, axis=0)
    x1_ref[...] = x1
    _store_row_chunks(x1p_ref, (), _pack_halves(x1), tm)

    lane = lax.broadcasted_iota(I32, (tm, LANES), 1)
    g_lane = lane < N_GROUPS
    gl = jnp.where(g_lane, logits, NEG_INF)
    g_max = jnp.max(gl, axis=-1, keepdims=True)
    g_idx = jnp.min(jnp.where(gl == g_max, lane, LANES), axis=-1, keepdims=True)
    p_g = 1.0 / jnp.sum(jnp.where(g_lane, jnp.exp(gl - g_max), 0.0), axis=-1, keepdims=True)
    lo = N_GROUPS + g_idx * EXPERTS_PER_GROUP
    el = jnp.where(lane >= lo, jnp.where(lane < lo + EXPERTS_PER_GROUP, logits, NEG_INF), NEG_INF)
    v1 = jnp.max(el, axis=-1, keepdims=True)
    i1 = jnp.min(jnp.where(el == v1, lane, LANES), axis=-1, keepdims=True)
    el2 = jnp.where(lane == i1, NEG_INF, el)
    v2 = jnp.max(el2, axis=-1, keepdims=True)
    i2 = jnp.min(jnp.where(lane == i1, LANES, jnp.where(el2 == v2, lane, LANES)), axis=-1, keepdims=True)
    ex = jnp.exp(v2 - v1)
    w1 = 1.0 / (1.0 + ex)
    gate1 = p_g * w1
    gate2 = p_g * (ex * w1)
    e1 = i1 - N_GROUPS
    e2 = i2 - N_GROUPS

    carry = carry_scr[...]
    tri = tri_ref[...]
    ranks = []
    for e in (e1, e2):
        hit = lane == e
        oh = jnp.where(hit, 1.0, 0.0)
        before = jnp.dot(tri, oh.astype(BF16), preferred_element_type=F32) + carry
        ranks.append(jnp.sum(jnp.where(hit, before, 0.0), axis=-1, keepdims=True).astype(I32))
        carry = carry + jnp.sum(oh, axis=0, keepdims=True)
    carry_scr[...] = carry
    cnt_ref[...] = carry

    mi_ref[...] = jnp.where(lane == 0, e1, jnp.where(lane == 1, e2,
                            jnp.where(lane == 2, ranks[0], jnp.where(lane == 3, ranks[1], 0))))
    mf_ref[...] = jnp.where(lane == 0, gate1, jnp.where(lane == 1, gate2, 0.0))


def _mix(o_n, hc, ga, gb, x2, w_oa, w_co, b_co, w_out, ln_g, ln_b, w_r, b_r, *, tm):
    t, d = x2.shape
    assert d // 2 == ROW_CHUNKS * LANES
    row = lambda w: pl.BlockSpec((tm, w), lambda i: (i, 0))
    full = lambda a: pl.BlockSpec(a.shape, lambda i: (0,) * a.ndim, pipeline_mode=pl.Buffered(1))
    tri = (jnp.arange(tm)[:, None] > jnp.arange(tm)[None, :]).astype(BF16)
    b_co, ln_g, ln_b = b_co.reshape(1, d), ln_g.reshape(1, d), ln_b.reshape(1, d)
    return pl.pallas_call(
        functools.partial(_mix_kernel, tm=tm),
        grid=(t // tm,),
        in_specs=[row(d), row(d), row(d), row(d), row(d),
                  full(w_oa), full(w_co), full(b_co), full(w_out), full(ln_g), full(ln_b),
                  full(w_r), full(b_r), full(tri)],
        out_specs=[row(d), pl.BlockSpec((tm * ROW_CHUNKS, LANES), lambda i: (i, 0)), row(LANES), row(LANES),
                   pl.BlockSpec((1, LANES), lambda i: (0, 0))],
        out_shape=[jax.ShapeDtypeStruct((t, d), F32),
                   jax.ShapeDtypeStruct((t * ROW_CHUNKS, LANES), U32),
                   jax.ShapeDtypeStruct((t, LANES), I32),
                   jax.ShapeDtypeStruct((t, LANES), F32),
                   jax.ShapeDtypeStruct((1, LANES), F32)],
        scratch_shapes=[pltpu.VMEM((1, LANES), F32)],
        compiler_params=_cparams(1),
        name="mix",
    )(o_n, hc, ga, gb, x2, w_oa, w_co, b_co, w_out, ln_g, ln_b, w_r, b_r, tri)


def _row_copy(src, src_row, dst, dst_row, sem):
    def piece(row):
        return pl.ds(pl.multiple_of(row * ROW_CHUNKS, ROW_CHUNKS), ROW_CHUNKS)
    return pltpu.make_async_copy(src.at[piece(src_row), :], dst.at[piece(dst_row), :], sem)


def _dispatch_kernel(dest_ref, x_ref, xs_in_ref, xs_ref, sem, *, tm):
    del xs_in_ref
    base = pl.program_id(0) * tm

    def issue(g, carry):
        for u in range(ISSUE_UNROLL):
            r = g * ISSUE_UNROLL + u
            for c in range(TOP_K):
                _row_copy(x_ref, r, xs_ref, dest_ref[TOP_K * (base + r) + c], sem).start(priority=c)
        return carry

    lax.fori_loop(0, tm // ISSUE_UNROLL, issue, 0)
    for _ in range(TOP_K):
        pltpu.make_async_copy(x_ref, xs_ref.at[pl.ds(0, tm * ROW_CHUNKS), :], sem).wait()


def _dispatch(dest_flat, x1p, n_rows, *, tm):
    w = x1p.shape[1]
    t = x1p.shape[0] // ROW_CHUNKS
    assert tm % ISSUE_UNROLL == 0
    return pl.pallas_call(
        functools.partial(_dispatch_kernel, tm=tm),
        grid_spec=pltpu.PrefetchScalarGridSpec(
            num_scalar_prefetch=1,
            grid=(t // tm,),
            in_specs=[pl.BlockSpec((tm * ROW_CHUNKS, w), lambda i, dest: (i, 0)),
                      pl.BlockSpec(memory_space=pl.ANY)],
            out_specs=pl.BlockSpec(memory_space=pl.ANY),
            scratch_shapes=[pltpu.SemaphoreType.DMA(())]),
        out_shape=jax.ShapeDtypeStruct((n_rows * ROW_CHUNKS, w), U32),
        input_output_aliases={2: 0},
        compiler_params=_cparams(1),
        name="dispatch",
    )(dest_flat, x1p, jnp.zeros((n_rows * ROW_CHUNKS, w), U32))


def _expert_kernel(blk_e_ref, n_used_ref, xs_ref, wg_ref, wu_ref, wd_ref, y_ref, wg_b, wu_b, wd_b):
    i = pl.program_id(0)
    active = i < n_used_ref[0]
    tm = xs_ref.shape[0] // ROW_CHUNKS

    @pl.when(jnp.logical_not(active))
    def _unused_block():
        y_ref[...] = jnp.zeros(y_ref.shape, y_ref.dtype)

    @pl.when(active & ((i == 0) | (blk_e_ref[i] != blk_e_ref[jnp.maximum(i - 1, 0)])))
    def _new_expert():
        wg_b[...] = wg_ref[0].astype(BF16)
        wu_b[...] = wu_ref[0].astype(BF16)
        wd_b[...] = wd_ref[0].astype(BF16)

    @pl.when(active)
    def _():
        xb = _unpack_halves(_load_row_chunks(xs_ref, (), tm)).astype(BF16)
        g = jnp.dot(xb, wg_b[...], preferred_element_type=F32)
        u = jnp.dot(xb, wu_b[...], preferred_element_type=F32)
        hid = (g * _sigmoid(g) * u).astype(BF16)
        y = jnp.dot(hid, wd_b[...], preferred_element_type=F32)
        _store_row_chunks(y_ref, (), _pack_halves(y), tm)


def _experts(blk_e, n_used, xs, w_gate, w_up, w_down, *, tm):
    w = xs.shape[1]
    n_rows = xs.shape[0] // ROW_CHUNKS
    _, d, de = w_gate.shape
    blk = lambda i, blk_e, n_used: jnp.minimum(i, n_used[0] - 1)
    row_map = lambda i, blk_e, n_used: (blk(i, blk_e, n_used), 0)
    w_map = lambda i, blk_e, n_used: (blk_e[blk(i, blk_e, n_used)], 0, 0)
    return pl.pallas_call(
        _expert_kernel,
        grid_spec=pltpu.PrefetchScalarGridSpec(
            num_scalar_prefetch=2,
            grid=(n_rows // tm,),
            in_specs=[pl.BlockSpec((tm * ROW_CHUNKS, w), row_map),
                      pl.BlockSpec((1, d, de), w_map),
                      pl.BlockSpec((1, d, de), w_map),
                      pl.BlockSpec((1, de, d), w_map)],
            out_specs=pl.BlockSpec((tm * ROW_CHUNKS, w), lambda i, blk_e, n_used: (i, 0)),
            scratch_shapes=[pltpu.VMEM((d, de), BF16), pltpu.VMEM((d, de), BF16), pltpu.VMEM((de, d), BF16)]),
        out_shape=jax.ShapeDtypeStruct((n_rows * ROW_CHUNKS, w), U32),
        compiler_params=_cparams(1),
        name="experts",
    )(blk_e, n_used, xs, w_gate, w_up, w_down)


def _combine_kernel(dest_ref, x1_ref, mf_ref, g_ref, b_ref, yp_ref, o_ref, buf, sems, *, tm, n_tiles):
    i = pl.program_id(0)

    def issue(tile, slot):
        base = tile * tm

        def body(g, carry):
            for u in range(ISSUE_UNROLL):
                r = g * ISSUE_UNROLL + u
                for c in range(TOP_K):
                    _row_copy(yp_ref, dest_ref[TOP_K * (base + r) + c], buf.at[slot, c], r,
                              sems.at[slot]).start(priority=c)
            return carry

        lax.fori_loop(0, tm // ISSUE_UNROLL, body, 0)

    @pl.when(i == 0)
    def _first():
        issue(0, 0)

    @pl.when(i + 1 < n_tiles)
    def _next():
        issue(i + 1, (i + 1) % 2)

    slot = i % 2
    for c in range(TOP_K):
        pltpu.make_async_copy(yp_ref.at[pl.ds(0, tm * ROW_CHUNKS), :], buf.at[slot, c], sems.at[slot]).wait()

    gates = mf_ref[...]
    ffn = (gates[:, 0:1] * _unpack_halves(_load_row_chunks(buf, (slot, 0), tm))
           + gates[:, 1:2] * _unpack_halves(_load_row_chunks(buf, (slot, 1), tm)))
    o_ref[...] = _layer_norm(DN_ALPHA * x1_ref[...] + ffn, g_ref[...], b_ref[...])


def _combine(dest_flat, x1, mf, ln_g, ln_b, yp, *, tm):
    t, d = x1.shape
    n_tiles = t // tm
    assert tm % ISSUE_UNROLL == 0
    vec = pl.BlockSpec((1, d), lambda i, dest: (0, 0))
    return pl.pallas_call(
        functools.partial(_combine_kernel, tm=tm, n_tiles=n_tiles),
        grid_spec=pltpu.PrefetchScalarGridSpec(
            num_scalar_prefetch=1,
            grid=(n_tiles,),
            in_specs=[pl.BlockSpec((tm, d), lambda i, dest: (i, 0)),
                      pl.BlockSpec((tm, LANES), lambda i, dest: (i, 0)),
                      vec, vec,
                      pl.BlockSpec(memory_space=pl.ANY)],
            out_specs=pl.BlockSpec((tm, d), lambda i, dest: (i, 0)),
            scratch_shapes=[pltpu.VMEM((2, TOP_K, tm * ROW_CHUNKS, LANES), U32),
                            pltpu.SemaphoreType.DMA((2,))]),
        out_shape=jax.ShapeDtypeStruct((t, d), F32),
        compiler_params=_cparams(1),
        name="combine",
    )(dest_flat, x1, mf, ln_g.reshape(1, d), ln_b.reshape(1, d), yp)


def _tile(n, pref):
    return pref if n % pref == 0 else n


def kernel(x, w_in, b_in, diff_lambda, head_norm_g, w_o_attn, rel_bias, conv_w, conv_b, conv_ln_g,
           conv_ln_b, w_conv_out, b_conv_out, w_out, ln1_g, ln1_b, router_g_w, router_g_b,
           router_e_w, router_e_b, expert_w_gate, expert_w_up, expert_w_down, ln2_g, ln2_b):
    bsz, s, d = x.shape
    t = bsz * s
    width = N_HEADS * 2 * HEAD_DIM
    assert w_in.shape[0] == DEPTH and w_in.shape[2] == 7 * width and width == d
    assert conv_w.shape[2] == width and TOP_K == 2
    tm_moe = _tile(t, 512)

    for li in range(DEPTH):
        lam_init = 0.8 - 0.6 * math.exp(-0.3 * li)
        x2 = x.reshape(t, d)
        q, k, v, hglu, ga, gb = _proj(
            x2, w_in[li].astype(BF16), b_in[li].reshape(1, -1), width=width, tm=_tile(t, 512),
            chunk=_tile(width, 512), q_scale=HEAD_DIM ** -0.5 * LOG2E)
        o_n = _attention(q.reshape(bsz, s, width), k.reshape(bsz, s, width), v.reshape(bsz, s, width),
                         rel_bias, diff_lambda[li], head_norm_g[li], tk=256 if s % 512 == 0 else 128,
                         lam_init=lam_init)
        hc = _conv(hglu.reshape(bsz, s, width), conv_w[li], conv_b[li], conv_ln_g[li], conv_ln_b[li],
                   ts=_tile(s, 256), rc=16)

        n_r = N_GROUPS + N_EXPERTS
        w_r = jnp.pad(jnp.concatenate([router_g_w[li], router_e_w[li]], axis=1),
                      ((0, 0), (0, LANES - n_r))).astype(BF16)
        b_r = jnp.pad(jnp.concatenate([router_g_b[li], router_e_b[li]]), (0, LANES - n_r)).reshape(1, LANES)
        x1, x1p, mi, mf, cnt = _mix(
            o_n.reshape(t, width), hc.reshape(t, width), ga, gb, x2,
            w_o_attn[li].astype(BF16), w_conv_out[li].astype(BF16), b_conv_out[li],
            w_out[li].astype(BF16), ln1_g[li], ln1_b[li], w_r, b_r, tm=_tile(t, 512))

        counts = cnt[0, :N_EXPERTS].astype(I32)
        padded = (counts + tm_moe - 1) // tm_moe * tm_moe
        pad_ends = jnp.cumsum(padded)
        pad_starts = pad_ends - padded
        chosen = mi[:, 0:TOP_K, None] == jnp.arange(N_EXPERTS, dtype=I32)[None, None, :]
        dest_flat = (jnp.sum(jnp.where(chosen, pad_starts[None, None, :], 0), axis=-1)
                     + mi[:, TOP_K:2 * TOP_K]).reshape(t * TOP_K)
        n_rows = t * TOP_K + N_EXPERTS * tm_moe
        n_blocks = n_rows // tm_moe
        blk_start = jnp.arange(n_blocks, dtype=I32) * tm_moe
        blk_e = jnp.minimum(jnp.sum(blk_start[:, None] >= pad_ends[None, :], axis=1),
                            N_EXPERTS - 1).astype(I32)
        n_used = (pad_ends[-1:] // tm_moe).astype(I32)

        xs = _dispatch(dest_flat, x1p, n_rows, tm=_tile(t, 512))
        yp = _experts(blk_e, n_used, xs, expert_w_gate[li], expert_w_up[li], expert_w_down[li], tm=tm_moe)
        x = _combine(dest_flat, x1, mf, ln2_g[li], ln2_b[li], yp, tm=_tile(t, 256)).reshape(bsz, s, d)
    return x
```

```python
import functools
import math

import jax
import jax.numpy as jnp
from jax import lax
from jax.experimental import pallas as pl
from jax.experimental.pallas import tpu as pltpu

F32 = jnp.float32
BF16 = jnp.bfloat16
U32 = jnp.uint32
I32 = jnp.int32

N_HEADS = 8
HEAD_DIM = 64
CONV_TAPS = 31
REL_BUCKETS = 32
REL_MAX_DIST = 128
N_GROUPS = 4
EXPERTS_PER_GROUP = 8
N_EXPERTS = N_GROUPS * EXPERTS_PER_GROUP
TOP_K = 2
DEPTH = 1
DN_ALPHA = (2.0 * DEPTH) ** 0.25
LN_EPS = 1e-5
NEG_INF = -1e30
LOG2E = 1.4426950408889634

LANES = 128
SUBLANES = 8
VMEM_LIMIT = 56 * 1024 * 1024
CONV_HALO = 32
CONV_SHIFT_ROWS = 56
ROW_CHUNKS = 4
ISSUE_UNROLL = 8


def _cparams(n_axes, flags=None):
    return pltpu.CompilerParams(dimension_semantics=("arbitrary",) * n_axes,
                                vmem_limit_bytes=VMEM_LIMIT, flags=flags)


def _sigmoid(x):
    return 1.0 / (1.0 + jnp.exp(-x))


def _layer_norm(z, g, b):
    mu = jnp.mean(z, axis=-1, keepdims=True)
    zc = z - mu
    var = jnp.mean(zc * zc, axis=-1, keepdims=True)
    return zc * lax.rsqrt(var + LN_EPS) * g + b


def _pack_halves(y):
    n = y.shape[1] // 2
    bits = pltpu.bitcast(y.astype(BF16).astype(F32), U32)
    return (bits[:, :n] >> 16) | (bits[:, n:] & jnp.uint32(0xFFFF0000))


def _unpack_halves(w):
    lo = pltpu.bitcast(w << 16, F32)
    hi = pltpu.bitcast(w & jnp.uint32(0xFFFF0000), F32)
    return jnp.concatenate([lo, hi], axis=1)


def _store_row_chunks(ref, prefix, packed, m):
    for q in range(ROW_CHUNKS):
        ref[prefix + (pl.ds(q, m, stride=ROW_CHUNKS), slice(None))] = packed[:, q * LANES:(q + 1) * LANES]


def _load_row_chunks(ref, prefix, m):
    return jnp.concatenate([ref[prefix + (pl.ds(q, m, stride=ROW_CHUNKS), slice(None))]
                            for q in range(ROW_CHUNKS)], axis=1)


def _proj_kernel(x_ref, w_ref, b_ref, q_ref, k_ref, v_ref, h_ref, ga_ref, gb_ref, *, width, chunk, q_scale):
    xb = x_ref[...].astype(BF16)

    def lin(col0):
        return (jnp.dot(xb, w_ref[:, col0:col0 + chunk], preferred_element_type=F32)
                + b_ref[:, col0:col0 + chunk])

    for c in range(0, width, chunk):
        sl = slice(c, c + chunk)
        q_ref[:, sl] = (lin(c) * q_scale).astype(BF16)
        k_ref[:, sl] = lin(width + c).astype(BF16)
        v_ref[:, sl] = lin(2 * width + c).astype(BF16)
        h_ref[:, sl] = (lin(3 * width + c) * _sigmoid(lin(4 * width + c))).astype(BF16)
        ga_ref[:, sl] = _sigmoid(lin(5 * width + c)).astype(BF16)
        gb_ref[:, sl] = _sigmoid(lin(6 * width + c)).astype(BF16)


def _proj(x2, w_in, b_in, *, width, tm, chunk, q_scale):
    t, d = x2.shape
    n_cols = w_in.shape[1]
    out = jax.ShapeDtypeStruct((t, width), BF16)
    row_spec = pl.BlockSpec((tm, width), lambda i: (i, 0))
    return pl.pallas_call(
        functools.partial(_proj_kernel, width=width, chunk=chunk, q_scale=q_scale),
        grid=(t // tm,),
        in_specs=[pl.BlockSpec((tm, d), lambda i: (i, 0)),
                  pl.BlockSpec((d, n_cols), lambda i: (0, 0), pipeline_mode=pl.Buffered(1)),
                  pl.BlockSpec((1, n_cols), lambda i: (0, 0))],
        out_specs=[row_spec] * 6,
        out_shape=[out] * 6,
        compiler_params=_cparams(1),
        name="proj",
    )(x2, w_in, b_in)


def _sublane_all(x, op):
    for shift in (4, 2, 1):
        x = op(x, pltpu.roll(x, shift, 0))
    return x


def _attn_kernel(tab_ref, bucket_ref, lam_ref, q_ref, k_ref, v_ref, g_ref, o_ref,
                 bias_scr, qz_scr, vt_scr, s0_scr, s1_scr, p0_scr, p1_scr, a0_scr, a1_scr,
                 m_scr, l_scr, acc_scr, *, tq, tk, dh, lam_init):
    h = pl.program_id(0)
    b = pl.program_id(1)
    qi = pl.program_id(2)
    hw = 2 * dh
    n_kv = vt_scr.shape[0]
    sw = tk
    n_sub = 2 * tq // sw

    @pl.when((b == 0) & (qi == 0))
    def _build_bias():
        far = tab_ref[REL_BUCKETS - 1, h]
        for t in range(3):
            for c in range(tq // sw):
                bk = bucket_ref[t, :, c * sw:(c + 1) * sw]
                acc = jnp.full((tk, sw), NEG_INF, F32)
                for r in range(REL_BUCKETS):
                    acc = jnp.where(bk == r, (tab_ref[r, h] - far) * LOG2E, acc)
                bias_scr[t, :, c * sw:(c + 1) * sw] = acc

    @pl.when(qi == 0)
    def _transpose_values():
        for jj in range(n_kv):
            vt_scr[jj] = v_ref[0, jj * tk:(jj + 1) * tk, :].astype(F32).T.astype(BF16)

    q = q_ref[0]
    lane = lax.broadcasted_iota(I32, (tq, hw), 1)
    zero = jnp.zeros_like(q)
    qz_scr[0:tq, :] = jnp.where(lane < dh, q, zero)
    qz_scr[tq:2 * tq, :] = jnp.where(lane >= dh, q, zero)
    m_scr[...] = jnp.full(m_scr.shape, NEG_INF, F32)
    l_scr[...] = jnp.zeros(l_scr.shape, F32)
    acc_scr[...] = jnp.zeros(acc_scr.shape, F32)

    s_bufs, p_bufs, a_bufs = (s0_scr, s1_scr), (p0_scr, p1_scr), (a0_scr, a1_scr)
    p1_scr[...] = jnp.zeros(p1_scr.shape, BF16)
    a1_scr[...] = jnp.ones(a1_scr.shape, F32)

    def issue_scores(j, slot):
        rows = (slice(j * tk, (j + 1) * tk) if isinstance(j, int)
                else pl.ds(pl.multiple_of(j * tk, tk), tk))
        s_bufs[slot][:, 0:2 * tq] = lax.dot_general(k_ref[0, rows, :], qz_scr[...], (((1,), (1,)), ((), ())),
                                                    preferred_element_type=F32)

    def accumulate(j_prev, slot_prev):
        pv = jnp.dot(vt_scr[j_prev], p_bufs[slot_prev][:, 0:2 * tq], preferred_element_type=F32)
        acc3 = acc_scr[:, 0:2 * tq].reshape(hw // SUBLANES, SUBLANES, 2 * tq)
        return (a_bufs[slot_prev][...][None] * acc3).reshape(hw, 2 * tq) + pv

    def step(j, bias_idx, slot, issue_next=True):
        if issue_next:
            issue_scores(j + 1, 1 - slot)
        acc_new = accumulate(jnp.maximum(j - 1, 0), 1 - slot)
        for c in range(n_sub):
            cs = slice(c * sw, (c + 1) * sw)
            s = s_bufs[slot][:, cs]
            if bias_idx is not None:
                qs = (c * sw) % tq
                s = s + bias_scr[bias_idx, :, qs:qs + sw]
            s3 = s.reshape(tk // SUBLANES, SUBLANES, sw)
            m_prev = m_scr[:, cs]
            m_new = jnp.maximum(m_prev, _sublane_all(jnp.max(s3, axis=0), jnp.maximum))
            alpha = jnp.exp2(m_prev - m_new)
            p3 = jnp.exp2(s3 - m_new[None])
            l_scr[:, cs] = alpha * l_scr[:, cs] + jnp.sum(p3, axis=0)
            p_bufs[slot][:, cs] = p3.reshape(tk, sw).astype(BF16)
            a_bufs[slot][:, cs] = alpha
            m_scr[:, cs] = m_new
        acc_scr[:, 0:2 * tq] = acc_new

    issue_scores(0, 0)

    @pl.when(qi == 0)
    def _first_query_tile():
        step(0, 1, 0)
        step(1, 2, 1, issue_next=False)

    @pl.when(qi >= 1)
    def _later_query_tiles():
        def far_pair(i, carry):
            step(2 * i, None, 0)
            step(2 * i + 1, None, 1)
            return carry

        lax.fori_loop(0, qi - 1, far_pair, 0)
        step(2 * qi - 2, None, 0)
        step(2 * qi - 1, 0, 1)
        step(2 * qi, 1, 0)
        step(2 * qi + 1, 2, 1, issue_next=False)

    acc_scr[:, 0:2 * tq] = accumulate(2 * qi + 1, 1)

    lv = lam_ref[...]
    lam = (jnp.exp(jnp.sum(lv[0:1] * lv[1:2], axis=-1, keepdims=True))
           - jnp.exp(jnp.sum(lv[2:3] * lv[3:4], axis=-1, keepdims=True)) + lam_init)
    inv_l = 1.0 / _sublane_all(l_scr[...], jnp.add)
    acc3 = acc_scr[:, 0:2 * tq].reshape(hw // SUBLANES, SUBLANES, 2 * tq)
    o3 = acc3[:, :, 0:tq] * inv_l[None, :, 0:tq] - lam * (acc3[:, :, tq:2 * tq] * inv_l[None, :, tq:2 * tq])
    ms = _sublane_all(jnp.sum(o3 * o3, axis=0), jnp.add) * (1.0 / hw)
    y = (o3 * lax.rsqrt(ms + LN_EPS)[None]).reshape(hw, tq) * (g_ref[...] * (1.0 - lam_init))
    o_ref[0] = y.T.astype(o_ref.dtype)


def _rel_bucket(dist):
    max_exact = REL_BUCKETS // 2
    d = jnp.maximum(dist, 1).astype(F32)
    large = max_exact + (jnp.log(d / max_exact) / math.log(REL_MAX_DIST / max_exact)
                         * (REL_BUCKETS - max_exact)).astype(I32)
    large = jnp.minimum(large, REL_BUCKETS - 1)
    return jnp.where(dist < max_exact, dist, large)


def _near_buckets(tq, tk):
    tiles = []
    for rel_tile in (-1, 0, 1):
        dist = (jnp.arange(tq, dtype=I32)[None, :]
                - (rel_tile * tk + jnp.arange(tk, dtype=I32))[:, None])
        tiles.append(jnp.where(dist >= 0, _rel_bucket(jnp.maximum(dist, 0)), -1))
    return jnp.stack(tiles).astype(I32)


def _attention(q, k, v, rel_bias, diff_lambda, head_norm_g, *, tk, lam_init):
    bsz, s, width = q.shape
    hw = width // N_HEADS
    tq = 2 * tk
    assert hw == 2 * HEAD_DIM and hw % LANES == 0 and s % tq == 0
    assert tk + 1 >= REL_MAX_DIST
    kernel = functools.partial(_attn_kernel, tq=tq, tk=tk, dh=HEAD_DIM, lam_init=lam_init)
    stat = pltpu.VMEM((SUBLANES, 2 * tq), F32)
    return pl.pallas_call(
        kernel,
        grid=(N_HEADS, bsz, s // tq),
        in_specs=[pl.BlockSpec(memory_space=pltpu.SMEM),
                  pl.BlockSpec((3, tk, tq), lambda h, b, i: (0, 0, 0)),
                  pl.BlockSpec((4, HEAD_DIM), lambda h, b, i: (0, 0)),
                  pl.BlockSpec((1, tq, hw), lambda h, b, i: (b, i, h)),
                  pl.BlockSpec((1, s, hw), lambda h, b, i: (b, 0, h)),
                  pl.BlockSpec((1, s, hw), lambda h, b, i: (b, 0, h)),
                  pl.BlockSpec((hw, 1), lambda h, b, i: (0, 0))],
        out_specs=pl.BlockSpec((1, tq, hw), lambda h, b, i: (b, i, h)),
        out_shape=jax.ShapeDtypeStruct((bsz, s, width), BF16),
        scratch_shapes=[pltpu.VMEM((3, tk, tq), F32),
                        pltpu.VMEM((2 * tq, hw), BF16),
                        pltpu.VMEM((s // tk, hw, tk), BF16),
                        pltpu.VMEM((tk, 2 * tq + LANES), F32),
                        pltpu.VMEM((tk, 2 * tq + LANES), F32),
                        pltpu.VMEM((tk, 2 * tq + LANES), BF16),
                        pltpu.VMEM((tk, 2 * tq + LANES), BF16),
                        stat, stat,
                        stat, stat,
                        pltpu.VMEM((hw, 2 * tq + LANES), F32)],
        compiler_params=_cparams(3),
        name="attn",
    )(rel_bias, _near_buckets(tq, tk), diff_lambda, q, k, v, head_norm_g.reshape(hw, 1))


def _conv_kernel(h_ref, halo_ref, w_ref, b_ref, g_ref, bb_ref, o_ref, sh, *, ts, rc):
    i = pl.program_id(1)
    n_ext = CONV_HALO + ts
    n_sh = n_ext - SUBLANES
    halo = halo_ref[0].astype(F32)
    sh[0, 0:CONV_HALO, :] = jnp.where(i > 0, halo, jnp.zeros_like(halo))
    sh[0, CONV_HALO:n_ext, :] = h_ref[0].astype(F32)
    for k in range(1, SUBLANES):
        for r0 in range(0, n_sh, CONV_SHIFT_ROWS):
            n = min(CONV_SHIFT_ROWS, n_sh - r0)
            sh[k, r0:r0 + n, :] = sh[0, r0 + k:r0 + k + n, :]
    off = CONV_HALO - (CONV_TAPS - 1)
    for r0 in range(0, ts, rc):
        acc = None
        for t in range(CONV_TAPS):
            k, base = (off + t) % SUBLANES, (off + t) // SUBLANES * SUBLANES
            slab = sh[k, r0 + base:r0 + base + rc, :].reshape(rc // SUBLANES, SUBLANES, -1)
            term = slab * w_ref[t][None]
            acc = term if acc is None else acc + term
        y = _layer_norm(acc.reshape(rc, -1) + b_ref[...], g_ref[...], bb_ref[...])
        o_ref[0, r0:r0 + rc, :] = (y * _sigmoid(y)).astype(o_ref.dtype)


def _conv(hglu, conv_w, conv_b, ln_g, ln_b, *, ts, rc):
    bsz, s, c = hglu.shape
    assert s % ts == 0 and ts % CONV_HALO == 0 and CONV_HALO >= CONV_TAPS - 1
    blocks_per_tile = ts // CONV_HALO
    vec = pl.BlockSpec((1, c), lambda b, i: (0, 0))
    return pl.pallas_call(
        functools.partial(_conv_kernel, ts=ts, rc=rc),
        grid=(bsz, s // ts),
        in_specs=[pl.BlockSpec((1, ts, c), lambda b, i: (b, i, 0)),
                  pl.BlockSpec((1, CONV_HALO, c),
                               lambda b, i: (b, jnp.maximum(i * blocks_per_tile - 1, 0), 0)),
                  pl.BlockSpec((CONV_TAPS, SUBLANES, c), lambda b, i: (0, 0, 0)),
                  vec, vec, vec],
        out_specs=pl.BlockSpec((1, ts, c), lambda b, i: (b, i, 0)),
        out_shape=jax.ShapeDtypeStruct((bsz, s, c), BF16),
        scratch_shapes=[pltpu.VMEM((SUBLANES, CONV_HALO + ts, c), F32)],
        compiler_params=_cparams(2),
        name="conv",
    )(hglu, hglu, jnp.broadcast_to(conv_w[:, None, :], (CONV_TAPS, SUBLANES, c)),
      conv_b.reshape(1, c), ln_g.reshape(1, c), ln_b.reshape(1, c))


def _mix_kernel(o_ref, hc_ref, ga_ref, gb_ref, x_ref, woa_ref, wco_ref, bco_ref, wout_ref,
                g1_ref, b1_ref, wr_ref, br_ref, tri_ref,
                x1_ref, x1p_ref, mi_ref, mf_ref, cnt_ref, carry_scr, *, tm):
    i = pl.program_id(0)

    @pl.when(i == 0)
    def _init():
        carry_scr[...] = jnp.zeros(carry_scr.shape, F32)

    halves = [slice(0, tm // 2), slice(tm // 2, tm)]
    branch = [(jnp.dot(o_ref[h, :], woa_ref[...], preferred_element_type=F32),
               jnp.dot(hc_ref[h, :], wco_ref[...], preferred_element_type=F32) + bco_ref[...]) for h in halves]
    mixed = [jnp.dot((ga_ref[h, :].astype(F32) * y_a + gb_ref[h, :].astype(F32) * y_b).astype(BF16),
                     wout_ref[...], preferred_element_type=F32) for h, (y_a, y_b) in zip(halves, branch)]
    x1_halves = [_layer_norm(DN_ALPHA * x_ref[h, :] + mx, g1_ref[...], b1_ref[...]) for h, mx in zip(halves, mixed)]
    logits = jnp.concatenate([jnp.dot(xh.astype(BF16), wr_ref[...], preferred_element_type=F32)
                              for xh in x1_halves], axis=0) + br_ref[...]
    x1 = jnp.concatenate(x1_halves, axis=0)
    x1_ref[...] = x1
    _store_row_chunks(x1p_ref, (), _pack_halves(x1), tm)

    lane = lax.broadcasted_iota(I32, (tm, LANES), 1)
    g_lane = lane < N_GROUPS
    gl = jnp.where(g_lane, logits, NEG_INF)
    g_max = jnp.max(gl, axis=-1, keepdims=True)
    g_idx = jnp.min(jnp.where(gl == g_max, lane, LANES), axis=-1, keepdims=True)
    p_g = 1.0 / jnp.sum(jnp.where(g_lane, jnp.exp(gl - g_max), 0.0), axis=-1, keepdims=True)
    lo = N_GROUPS + g_idx * EXPERTS_PER_GROUP
    el = jnp.where(lane >= lo, jnp.where(lane < lo + EXPERTS_PER_GROUP, logits, NEG_INF), NEG_INF)
    v1 = jnp.max(el, axis=-1, keepdims=True)
    i1 = jnp.min(jnp.where(el == v1, lane, LANES), axis=-1, keepdims=True)
    el2 = jnp.where(lane == i1, NEG_INF, el)
    v2 = jnp.max(el2, axis=-1, keepdims=True)
    i2 = jnp.min(jnp.where(lane == i1, LANES, jnp.where(el2 == v2, lane, LANES)), axis=-1, keepdims=True)
    ex = jnp.exp(v2 - v1)
    w1 = 1.0 / (1.0 + ex)
    gate1 = p_g * w1
    gate2 = p_g * (ex * w1)
    e1 = i1 - N_GROUPS
    e2 = i2 - N_GROUPS

    carry = carry_scr[...]
    tri = tri_ref[...]
    ranks = []
    for e in (e1, e2):
        hit = lane == e
        oh = jnp.where(hit, 1.0, 0.0)
        before = jnp.dot(tri, oh.astype(BF16), preferred_element_type=F32) + carry
        ranks.append(jnp.sum(jnp.where(hit, before, 0.0), axis=-1, keepdims=True).astype(I32))
        carry = carry + jnp.sum(oh, axis=0, keepdims=True)
    carry_scr[...] = carry
    cnt_ref[...] = carry

    mi_ref[...] = jnp.where(lane == 0, e1, jnp.where(lane == 1, e2,
                            jnp.where(lane == 2, ranks[0], jnp.where(lane == 3, ranks[1], 0))))
    mf_ref[...] = jnp.where(lane == 0, gate1, jnp.where(lane == 1, gate2, 0.0))


def _mix(o_n, hc, ga, gb, x2, w_oa, w_co, b_co, w_out, ln_g, ln_b, w_r, b_r, *, tm):
    t, d = x2.shape
    assert d // 2 == ROW_CHUNKS * LANES
    row = lambda w: pl.BlockSpec((tm, w), lambda i: (i, 0))
    full = lambda a: pl.BlockSpec(a.shape, lambda i: (0,) * a.ndim, pipeline_mode=pl.Buffered(1))
    tri = (jnp.arange(tm)[:, None] > jnp.arange(tm)[None, :]).astype(BF16)
    b_co, ln_g, ln_b = b_co.reshape(1, d), ln_g.reshape(1, d), ln_b.reshape(1, d)
    return pl.pallas_call(
        functools.partial(_mix_kernel, tm=tm),
        grid=(t // tm,),
        in_specs=[row(d), row(d), row(d), row(d), row(d),
                  full(w_oa), full(w_co), full(b_co), full(w_out), full(ln_g), full(ln_b),
                  full(w_r), full(b_r), full(tri)],
        out_specs=[row(d), pl.BlockSpec((tm * ROW_CHUNKS, LANES), lambda i: (i, 0)), row(LANES), row(LANES),
                   pl.BlockSpec((1, LANES), lambda i: (0, 0))],
        out_shape=[jax.ShapeDtypeStruct((t, d), F32),
                   jax.ShapeDtypeStruct((t * ROW_CHUNKS, LANES), U32),
                   jax.ShapeDtypeStruct((t, LANES), I32),
                   jax.ShapeDtypeStruct((t, LANES), F32),
                   jax.ShapeDtypeStruct((1, LANES), F32)],
        scratch_shapes=[pltpu.VMEM((1, LANES), F32)],
        compiler_params=_cparams(1),
        name="mix",
    )(o_n, hc, ga, gb, x2, w_oa, w_co, b_co, w_out, ln_g, ln_b, w_r, b_r, tri)


def _row_copy(src, src_row, dst, dst_row, sem):
    def piece(row):
        return pl.ds(pl.multiple_of(row * ROW_CHUNKS, ROW_CHUNKS), ROW_CHUNKS)
    return pltpu.make_async_copy(src.at[piece(src_row), :], dst.at[piece(dst_row), :], sem)


def _dispatch_kernel(dest_ref, x_ref, xs_in_ref, xs_ref, sem, *, tm):
    del xs_in_ref
    base = pl.program_id(0) * tm

    def issue(g, carry):
        for u in range(ISSUE_UNROLL):
            r = g * ISSUE_UNROLL + u
            for c in range(TOP_K):
                _row_copy(x_ref, r, xs_ref, dest_ref[TOP_K * (base + r) + c], sem).start(priority=c)
        return carry

    lax.fori_loop(0, tm // ISSUE_UNROLL, issue, 0)
    for _ in range(TOP_K):
        pltpu.make_async_copy(x_ref, xs_ref.at[pl.ds(0, tm * ROW_CHUNKS), :], sem).wait()


def _dispatch(dest_flat, x1p, n_rows, *, tm):
    w = x1p.shape[1]
    t = x1p.shape[0] // ROW_CHUNKS
    assert tm % ISSUE_UNROLL == 0
    return pl.pallas_call(
        functools.partial(_dispatch_kernel, tm=tm),
        grid_spec=pltpu.PrefetchScalarGridSpec(
            num_scalar_prefetch=1,
            grid=(t // tm,),
            in_specs=[pl.BlockSpec((tm * ROW_CHUNKS, w), lambda i, dest: (i, 0)),
                      pl.BlockSpec(memory_space=pl.ANY)],
            out_specs=pl.BlockSpec(memory_space=pl.ANY),
            scratch_shapes=[pltpu.SemaphoreType.DMA(())]),
        out_shape=jax.ShapeDtypeStruct((n_rows * ROW_CHUNKS, w), U32),
        input_output_aliases={2: 0},
        compiler_params=_cparams(1),
        name="dispatch",
    )(dest_flat, x1p, jnp.zeros((n_rows * ROW_CHUNKS, w), U32))


def _expert_kernel(blk_e_ref, n_used_ref, xs_ref, wg_ref, wu_ref, wd_ref, y_ref, wg_b, wu_b, wd_b):
    i = pl.program_id(0)
    active = i < n_used_ref[0]
    tm = xs_ref.shape[0] // ROW_CHUNKS

    @pl.when(jnp.logical_not(active))
    def _unused_block():
        y_ref[...] = jnp.zeros(y_ref.shape, y_ref.dtype)

    @pl.when(active & ((i == 0) | (blk_e_ref[i] != blk_e_ref[jnp.maximum(i - 1, 0)])))
    def _new_expert():
        wg_b[...] = wg_ref[0].astype(BF16)
        wu_b[...] = wu_ref[0].astype(BF16)
        wd_b[...] = wd_ref[0].astype(BF16)

    @pl.when(active)
    def _():
        xb = _unpack_halves(_load_row_chunks(xs_ref, (), tm)).astype(BF16)
        g = jnp.dot(xb, wg_b[...], preferred_element_type=F32)
        u = jnp.dot(xb, wu_b[...], preferred_element_type=F32)
        hid = (g * _sigmoid(g) * u).astype(BF16)
        y = jnp.dot(hid, wd_b[...], preferred_element_type=F32)
        _store_row_chunks(y_ref, (), _pack_halves(y), tm)


def _experts(blk_e, n_used, xs, w_gate, w_up, w_down, *, tm):
    w = xs.shape[1]
    n_rows = xs.shape[0] // ROW_CHUNKS
    _, d, de = w_gate.shape
    blk = lambda i, blk_e, n_used: jnp.minimum(i, n_used[0] - 1)
    row_map = lambda i, blk_e, n_used: (blk(i, blk_e, n_used), 0)
    w_map = lambda i, blk_e, n_used: (blk_e[blk(i, blk_e, n_used)], 0, 0)
    return pl.pallas_call(
        _expert_kernel,
        grid_spec=pltpu.PrefetchScalarGridSpec(
            num_scalar_prefetch=2,
            grid=(n_rows // tm,),
            in_specs=[pl.BlockSpec((tm * ROW_CHUNKS, w), row_map),
                      pl.BlockSpec((1, d, de), w_map),
                      pl.BlockSpec((1, d, de), w_map),
                      pl.BlockSpec((1, de, d), w_map)],
            out_specs=pl.BlockSpec((tm * ROW_CHUNKS, w), lambda i, blk_e, n_used: (i, 0)),
            scratch_shapes=[pltpu.VMEM((d, de), BF16), pltpu.VMEM((d, de), BF16), pltpu.VMEM((de, d), BF16)]),
        out_shape=jax.ShapeDtypeStruct((n_rows * ROW_CHUNKS, w), U32),
        compiler_params=_cparams(1),
        name="experts",
    )(blk_e, n_used, xs, w_gate, w_up, w_down)


def _combine_kernel(dest_ref, x1_ref, mf_ref, g_ref, b_ref, yp_ref, o_ref, buf, sems, *, tm, n_tiles):
    i = pl.program_id(0)

    def issue(tile, slot):
        base = tile * tm

        def body(g, carry):
            for u in range(ISSUE_UNROLL):
                r = g * ISSUE_UNROLL + u
                for c in range(TOP_K):
                    _row_copy(yp_ref, dest_ref[TOP_K * (base + r) + c], buf.at[slot, c], r,
                              sems.at[slot]).start(priority=c)
            return carry

        lax.fori_loop(0, tm // ISSUE_UNROLL, body, 0)

    @pl.when(i == 0)
    def _first():
        issue(0, 0)

    @pl.when(i + 1 < n_tiles)
    def _next():
        issue(i + 1, (i + 1) % 2)

    slot = i % 2
    for c in range(TOP_K):
        pltpu.make_async_copy(yp_ref.at[pl.ds(0, tm * ROW_CHUNKS), :], buf.at[slot, c], sems.at[slot]).wait()

    gates = mf_ref[...]
    ffn = (gates[:, 0:1] * _unpack_halves(_load_row_chunks(buf, (slot, 0), tm))
           + gates[:, 1:2] * _unpack_halves(_load_row_chunks(buf, (slot, 1), tm)))
    o_ref[...] = _layer_norm(DN_ALPHA * x1_ref[...] + ffn, g_ref[...], b_ref[...])


def _combine(dest_flat, x1, mf, ln_g, ln_b, yp, *, tm):
    t, d = x1.shape
    n_tiles = t // tm
    assert tm % ISSUE_UNROLL == 0
    vec = pl.BlockSpec((1, d), lambda i, dest: (0, 0))
    return pl.pallas_call(
        functools.partial(_combine_kernel, tm=tm, n_tiles=n_tiles),
        grid_spec=pltpu.PrefetchScalarGridSpec(
            num_scalar_prefetch=1,
            grid=(n_tiles,),
            in_specs=[pl.BlockSpec((tm, d), lambda i, dest: (i, 0)),
                      pl.BlockSpec((tm, LANES), lambda i, dest: (i, 0)),
                      vec, vec,
                      pl.BlockSpec(memory_space=pl.ANY)],
            out_specs=pl.BlockSpec((tm, d), lambda i, dest: (i, 0)),
            scratch_shapes=[pltpu.VMEM((2, TOP_K, tm * ROW_CHUNKS, LANES), U32),
                            pltpu.SemaphoreType.DMA((2,))]),
        out_shape=jax.ShapeDtypeStruct((t, d), F32),
        compiler_params=_cparams(1),
        name="combine",
    )(dest_flat, x1, mf, ln_g.reshape(1, d), ln_b.reshape(1, d), yp)


def _tile(n, pref):
    return pref if n % pref == 0 else n


def kernel(x, w_in, b_in, diff_lambda, head_norm_g, w_o_attn, rel_bias, conv_w, conv_b, conv_ln_g,
           conv_ln_b, w_conv_out, b_conv_out, w_out, ln1_g, ln1_b, router_g_w, router_g_b,
           router_e_w, router_e_b, expert_w_gate, expert_w_up, expert_w_down, ln2_g, ln2_b):
    bsz, s, d = x.shape
    t = bsz * s
    width = N_HEADS * 2 * HEAD_DIM
    assert w_in.shape[0] == DEPTH and w_in.shape[2] == 7 * width and width == d
    assert conv_w.shape[2] == width and TOP_K == 2
    tm_moe = _tile(t, 512)

    for li in range(DEPTH):
        lam_init = 0.8 - 0.6 * math.exp(-0.3 * li)
        x2 = x.reshape(t, d)
        q, k, v, hglu, ga, gb = _proj(
            x2, w_in[li].astype(BF16), b_in[li].reshape(1, -1), width=width, tm=_tile(t, 512),
            chunk=_tile(width, 512), q_scale=HEAD_DIM ** -0.5 * LOG2E)
        o_n = _attention(q.reshape(bsz, s, width), k.reshape(bsz, s, width), v.reshape(bsz, s, width),
                         rel_bias, diff_lambda[li], head_norm_g[li], tk=256 if s % 512 == 0 else 128,
                         lam_init=lam_init)
        hc = _conv(hglu.reshape(bsz, s, width), conv_w[li], conv_b[li], conv_ln_g[li], conv_ln_b[li],
                   ts=_tile(s, 256), rc=16)

        n_r = N_GROUPS + N_EXPERTS
        w_r = jnp.pad(jnp.concatenate([router_g_w[li], router_e_w[li]], axis=1),
                      ((0, 0), (0, LANES - n_r))).astype(BF16)
        b_r = jnp.pad(jnp.concatenate([router_g_b[li], router_e_b[li]]), (0, LANES - n_r)).reshape(1, LANES)
        x1, x1p, mi, mf, cnt = _mix(
            o_n.reshape(t, width), hc.reshape(t, width), ga, gb, x2,
            w_o_attn[li].astype(BF16), w_conv_out[li].astype(BF16), b_conv_out[li],
            w_out[li].astype(BF16), ln1_g[li], ln1_b[li], w_r, b_r, tm=_tile(t, 512))

        counts = cnt[0, :N_EXPERTS].astype(I32)
        padded = (counts + tm_moe - 1) // tm_moe * tm_moe
        pad_ends = jnp.cumsum(padded)
        pad_starts = pad_ends - padded
        chosen = mi[:, 0:TOP_K, None] == jnp.arange(N_EXPERTS, dtype=I32)[None, None, :]
        dest_flat = (jnp.sum(jnp.where(chosen, pad_starts[None, None, :], 0), axis=-1)
                     + mi[:, TOP_K:2 * TOP_K]).reshape(t * TOP_K)
        n_rows = t * TOP_K + N_EXPERTS * tm_moe
        n_blocks = n_rows // tm_moe
        blk_start = jnp.arange(n_blocks, dtype=I32) * tm_moe
        blk_e = jnp.minimum(jnp.sum(blk_start[:, None] >= pad_ends[None, :], axis=1),
                            N_EXPERTS - 1).astype(I32)
        n_used = (pad_ends[-1:] // tm_moe).astype(I32)

        xs = _dispatch(dest_flat, x1p, n_rows, tm=_tile(t, 512))
        yp = _experts(blk_e, n_used, xs, expert_w_gate[li], expert_w_up[li], expert_w_down[li], tm=tm_moe)
        x = _combine(dest_flat, x1, mf, ln2_g[li], ln2_b[li], yp, tm=_tile(t, 256)).reshape(bsz, s, d)
    return x
```

```python
import functools
import math

import jax
import jax.numpy as jnp
from jax import lax
from jax.experimental import pallas as pl
from jax.experimental.pallas import tpu as pltpu

F32 = jnp.float32
BF16 = jnp.bfloat16
U32 = jnp.uint32
I32 = jnp.int32

N_HEADS = 8
HEAD_DIM = 64
CONV_TAPS = 31
REL_BUCKETS = 32
REL_MAX_DIST = 128
N_GROUPS = 4
EXPERTS_PER_GROUP = 8
N_EXPERTS = N_GROUPS * EXPERTS_PER_GROUP
TOP_K = 2
DEPTH = 1
DN_ALPHA = (2.0 * DEPTH) ** 0.25
LN_EPS = 1e-5
NEG_INF = -1e30
LOG2E = 1.4426950408889634

LANES = 128
SUBLANES = 8
VMEM_LIMIT = 56 * 1024 * 1024
CONV_HALO = 32
CONV_SHIFT_ROWS = 56
HEADS_PER_STEP = 4
ONES_ROWS = 16
ROW_CHUNKS = 4
ISSUE_UNROLL = 8


def _cparams(n_axes, flags=None):
    return pltpu.CompilerParams(dimension_semantics=("arbitrary",) * n_axes,
                                vmem_limit_bytes=VMEM_LIMIT, flags=flags)


def _sigmoid(x):
    return 1.0 / (1.0 + jnp.exp(-x))


def _layer_norm(z, g, b):
    mu = jnp.mean(z, axis=-1, keepdims=True)
    zc = z - mu
    var = jnp.mean(zc * zc, axis=-1, keepdims=True)
    return zc * lax.rsqrt(var + LN_EPS) * g + b


def _pack_halves(y):
    n = y.shape[1] // 2
    bits = pltpu.bitcast(y.astype(BF16).astype(F32), U32)
    return (bits[:, :n] >> 16) | (bits[:, n:] & jnp.uint32(0xFFFF0000))


def _unpack_halves(w):
    lo = pltpu.bitcast(w << 16, F32)
    hi = pltpu.bitcast(w & jnp.uint32(0xFFFF0000), F32)
    return jnp.concatenate([lo, hi], axis=1)


def _store_row_chunks(ref, prefix, packed, m):
    for q in range(ROW_CHUNKS):
        ref[prefix + (pl.ds(q, m, stride=ROW_CHUNKS), slice(None))] = packed[:, q * LANES:(q + 1) * LANES]


def _load_row_chunks(ref, prefix, m):
    return jnp.concatenate([ref[prefix + (pl.ds(q, m, stride=ROW_CHUNKS), slice(None))]
                            for q in range(ROW_CHUNKS)], axis=1)


def _proj_kernel(x_ref, w_ref, b_ref, q_ref, k_ref, v_ref, h_ref, ga_ref, gb_ref, *, width, chunk, q_scale):
    xb = x_ref[...].astype(BF16)

    def lin(col0):
        return (jnp.dot(xb, w_ref[:, col0:col0 + chunk], preferred_element_type=F32)
                + b_ref[:, col0:col0 + chunk])

    for c in range(0, width, chunk):
        sl = slice(c, c + chunk)
        q_ref[:, sl] = (lin(c) * q_scale).astype(BF16)
        k_ref[:, sl] = lin(width + c).astype(BF16)
        v_ref[:, sl] = lin(2 * width + c).astype(BF16)
        h_ref[:, sl] = (lin(3 * width + c) * _sigmoid(lin(4 * width + c))).astype(BF16)
        ga_ref[:, sl] = _sigmoid(lin(5 * width + c)).astype(BF16)
        gb_ref[:, sl] = _sigmoid(lin(6 * width + c)).astype(BF16)


def _proj(x2, w_in, b_in, *, width, tm, chunk, q_scale):
    t, d = x2.shape
    n_cols = w_in.shape[1]
    out = jax.ShapeDtypeStruct((t, width), BF16)
    row_spec = pl.BlockSpec((tm, width), lambda i: (i, 0))
    return pl.pallas_call(
        functools.partial(_proj_kernel, width=width, chunk=chunk, q_scale=q_scale),
        grid=(t // tm,),
        in_specs=[pl.BlockSpec((tm, d), lambda i: (i, 0)),
                  pl.BlockSpec((d, n_cols), lambda i: (0, 0), pipeline_mode=pl.Buffered(1)),
                  pl.BlockSpec((1, n_cols), lambda i: (0, 0))],
        out_specs=[row_spec] * 6,
        out_shape=[out] * 6,
        compiler_params=_cparams(1),
        name="proj",
    )(x2, w_in, b_in)


def _sublane_all(x, op):
    for shift in (4, 2, 1):
        x = op(x, pltpu.roll(x, shift, 0))
    return x


def _attn_kernel(tab_ref, bucket_ref, lam_ref, q_ref, k_ref, v_ref, g_ref, o_ref,
                 bias_scr, qz_scr, vt_scr, s0_scr, s1_scr, p0_scr, p1_scr, a0_scr, a1_scr,
                 m_scr, acc_scr, *, tq, tk, dh, lam_init):
    hp = pl.program_id(0)
    b = pl.program_id(1)
    qi = pl.program_id(2)
    hw = 2 * dh
    acc_rows = hw + ONES_ROWS
    n_kv = vt_scr.shape[1]
    sw = tk
    n_sub = 2 * tq // sw
    heads = range(HEADS_PER_STEP)

    @pl.when((b == 0) & (qi == 0))
    def _build_bias():
        for hh in heads:
            h = hp * HEADS_PER_STEP + hh
            far = tab_ref[REL_BUCKETS - 1, h]
            for t in range(3):
                for c in range(tq // sw):
                    bk = bucket_ref[t, :, c * sw:(c + 1) * sw]
                    acc = jnp.full((tk, sw), NEG_INF, F32)
                    for r in range(REL_BUCKETS):
                        acc = jnp.where(bk == r, (tab_ref[r, h] - far) * LOG2E, acc)
                    bias_scr[hh, t, :, c * sw:(c + 1) * sw] = acc

    @pl.when(qi == 0)
    def _transpose_values():
        for hh in heads:
            for jj in range(n_kv):
                vt_scr[hh, jj, 0:hw, :] = (v_ref[0, jj * tk:(jj + 1) * tk, hh * hw:(hh + 1) * hw]
                                           .astype(F32).T.astype(BF16))
                vt_scr[hh, jj, hw:hw + ONES_ROWS, :] = jnp.ones((ONES_ROWS, tk), BF16)

    lane = lax.broadcasted_iota(I32, (tq, hw), 1)
    for hh in heads:
        q = q_ref[0, :, hh * hw:(hh + 1) * hw]
        zero = jnp.zeros_like(q)
        qz_scr[hh, 0:tq, :] = jnp.where(lane < dh, q, zero)
        qz_scr[hh, tq:2 * tq, :] = jnp.where(lane >= dh, q, zero)
    m_scr[...] = jnp.full(m_scr.shape, NEG_INF, F32)
    acc_scr[...] = jnp.zeros(acc_scr.shape, F32)

    s_bufs, p_bufs, a_bufs = (s0_scr, s1_scr), (p0_scr, p1_scr), (a0_scr, a1_scr)
    p1_scr[...] = jnp.zeros(p1_scr.shape, BF16)
    a1_scr[...] = jnp.ones(a1_scr.shape, F32)

    all_cols = (slice(0, 2 * tq),)
    late_cols = (slice(tk, tq), slice(tq + tk, 2 * tq))

    def issue_scores(j, slot, cols=all_cols):
        rows = (slice(j * tk, (j + 1) * tk) if isinstance(j, int)
                else pl.ds(pl.multiple_of(j * tk, tk), tk))
        for hh in heads:
            kj = k_ref[0, rows, hh * hw:(hh + 1) * hw]
            for cs in cols:
                s_bufs[slot][hh, :, cs] = lax.dot_general(kj, qz_scr[hh, cs, :], (((1,), (1,)), ((), ())),
                                                          preferred_element_type=F32)

    def accumulate(hh, j_prev, slot_prev, cs=slice(0, 2 * tq)):
        n = cs.stop - cs.start
        pv = jnp.dot(vt_scr[hh, j_prev], p_bufs[slot_prev][hh, :, cs], preferred_element_type=F32)
        acc3 = acc_scr[hh, :, cs].reshape(acc_rows // SUBLANES, SUBLANES, n)
        return (a_bufs[slot_prev][hh, :, cs][None] * acc3).reshape(acc_rows, n) + pv

    def step(j, bias_idx, slot, issue_next=True, next_is_last=False, last=False):
        if issue_next:
            issue_scores(j + 1, 1 - slot, late_cols if next_is_last else all_cols)
        acc_new = [accumulate(hh, jnp.maximum(j - 1, 0), 1 - slot) for hh in heads]
        for hh in heads:
            for c in range(n_sub):
                cs = slice(c * sw, (c + 1) * sw)
                qs = (c * sw) % tq
                if last and qs < tk:
                    continue
                s = s_bufs[slot][hh, :, cs]
                if bias_idx is not None:
                    s = s + bias_scr[hh, bias_idx, :, qs:qs + sw]
                s3 = s.reshape(tk // SUBLANES, SUBLANES, sw)
                m_prev = m_scr[hh, :, cs]
                m_new = jnp.maximum(m_prev, _sublane_all(jnp.max(s3, axis=0), jnp.maximum))
                p3 = jnp.exp2(s3 - m_new[None])
                p_bufs[slot][hh, :, cs] = p3.reshape(tk, sw).astype(BF16)
                a_bufs[slot][hh, :, cs] = jnp.exp2(m_prev - m_new)
                m_scr[hh, :, cs] = m_new
        for hh in heads:
            acc_scr[hh] = acc_new[hh]

    issue_scores(0, 0)

    @pl.when(qi == 0)
    def _first_query_tile():
        step(0, 1, 0, next_is_last=True)
        step(1, 2, 1, issue_next=False, last=True)

    @pl.when(qi >= 1)
    def _later_query_tiles():
        def far_pair(i, carry):
            step(2 * i, None, 0)
            step(2 * i + 1, None, 1)
            return carry

        lax.fori_loop(0, qi - 1, far_pair, 0)
        step(2 * qi - 2, None, 0)
        step(2 * qi - 1, 0, 1)
        step(2 * qi, 1, 0, next_is_last=True)
        step(2 * qi + 1, 2, 1, issue_next=False, last=True)

    drained = [[accumulate(hh, 2 * qi + 1, 1, cs) for cs in late_cols] for hh in heads]
    for hh in heads:
        for cs, val in zip(late_cols, drained[hh]):
            acc_scr[hh, :, cs] = val

    lv = lam_ref[...]
    lam = (jnp.exp(jnp.sum(lv[0:1] * lv[1:2], axis=-1, keepdims=True))
           - jnp.exp(jnp.sum(lv[2:3] * lv[3:4], axis=-1, keepdims=True)) + lam_init)
    gain = g_ref[...] * (1.0 - lam_init)
    for hh in heads:
        acc_all = acc_scr[hh].reshape(acc_rows // SUBLANES, SUBLANES, 2 * tq)
        inv_l = 1.0 / acc_all[hw // SUBLANES]
        acc3 = acc_all[0:hw // SUBLANES]
        o3 = acc3[:, :, 0:tq] * inv_l[None, :, 0:tq] - lam * (acc3[:, :, tq:2 * tq] * inv_l[None, :, tq:2 * tq])
        ms = _sublane_all(jnp.sum(o3 * o3, axis=0), jnp.add) * (1.0 / hw)
        y = (o3 * lax.rsqrt(ms + LN_EPS)[None]).reshape(hw, tq) * gain
        o_ref[0, :, hh * hw:(hh + 1) * hw] = y.T.astype(o_ref.dtype)


def _rel_bucket(dist):
    max_exact = REL_BUCKETS // 2
    d = jnp.maximum(dist, 1).astype(F32)
    large = max_exact + (jnp.log(d / max_exact) / math.log(REL_MAX_DIST / max_exact)
                         * (REL_BUCKETS - max_exact)).astype(I32)
    large = jnp.minimum(large, REL_BUCKETS - 1)
    return jnp.where(dist < max_exact, dist, large)


def _near_buckets(tq, tk):
    tiles = []
    for rel_tile in (-1, 0, 1):
        dist = (jnp.arange(tq, dtype=I32)[None, :]
                - (rel_tile * tk + jnp.arange(tk, dtype=I32))[:, None])
        tiles.append(jnp.where(dist >= 0, _rel_bucket(jnp.maximum(dist, 0)), -1))
    return jnp.stack(tiles).astype(I32)


def _attention(q, k, v, rel_bias, diff_lambda, head_norm_g, *, tk, lam_init):
    bsz, s, width = q.shape
    hw = width // N_HEADS
    tq = 2 * tk
    hps, pw = HEADS_PER_STEP, HEADS_PER_STEP * hw
    assert hw == 2 * HEAD_DIM and hw % LANES == 0 and s % tq == 0 and N_HEADS % hps == 0
    assert tk + 1 >= REL_MAX_DIST
    kernel = functools.partial(_attn_kernel, tq=tq, tk=tk, dh=HEAD_DIM, lam_init=lam_init)
    stat = pltpu.VMEM((hps, SUBLANES, 2 * tq), F32)
    return pl.pallas_call(
        kernel,
        grid=(N_HEADS // hps, bsz, s // tq),
        in_specs=[pl.BlockSpec(memory_space=pltpu.SMEM),
                  pl.BlockSpec((3, tk, tq), lambda h, b, i: (0, 0, 0)),
                  pl.BlockSpec((4, HEAD_DIM), lambda h, b, i: (0, 0)),
                  pl.BlockSpec((1, tq, pw), lambda h, b, i: (b, i, h)),
                  pl.BlockSpec((1, s, pw), lambda h, b, i: (b, 0, h)),
                  pl.BlockSpec((1, s, pw), lambda h, b, i: (b, 0, h)),
                  pl.BlockSpec((hw, 1), lambda h, b, i: (0, 0))],
        out_specs=pl.BlockSpec((1, tq, pw), lambda h, b, i: (b, i, h)),
        out_shape=jax.ShapeDtypeStruct((bsz, s, width), BF16),
        scratch_shapes=[pltpu.VMEM((hps, 3, tk, tq), F32),
                        pltpu.VMEM((hps, 2 * tq, hw), BF16),
                        pltpu.VMEM((hps, s // tk, hw + ONES_ROWS, tk), BF16),
                        pltpu.VMEM((hps, tk, 2 * tq), F32),
                        pltpu.VMEM((hps, tk, 2 * tq), F32),
                        pltpu.VMEM((hps, tk, 2 * tq), BF16),
                        pltpu.VMEM((hps, tk, 2 * tq), BF16),
                        stat, stat,
                        stat,
                        pltpu.VMEM((hps, hw + ONES_ROWS, 2 * tq), F32)],
        compiler_params=_cparams(3),
        name="attn",
    )(rel_bias, _near_buckets(tq, tk), diff_lambda, q, k, v, head_norm_g.reshape(hw, 1))


def _conv_kernel(h_ref, halo_ref, w_ref, b_ref, g_ref, bb_ref, o_ref, sh, *, ts, rc):
    i = pl.program_id(1)
    n_ext = CONV_HALO + ts
    n_sh = n_ext - SUBLANES
    halo = halo_ref[0].astype(F32)
    sh[0, 0:CONV_HALO, :] = jnp.where(i > 0, halo, jnp.zeros_like(halo))
    sh[0, CONV_HALO:n_ext, :] = h_ref[0].astype(F32)
    for k in range(1, SUBLANES):
        for r0 in range(0, n_sh, CONV_SHIFT_ROWS):
            n = min(CONV_SHIFT_ROWS, n_sh - r0)
            sh[k, r0:r0 + n, :] = sh[0, r0 + k:r0 + k + n, :]
    off = CONV_HALO - (CONV_TAPS - 1)
    for r0 in range(0, ts, rc):
        acc = None
        for t in range(CONV_TAPS):
            k, base = (off + t) % SUBLANES, (off + t) // SUBLANES * SUBLANES
            slab = sh[k, r0 + base:r0 + base + rc, :].reshape(rc // SUBLANES, SUBLANES, -1)
            term = slab * w_ref[t][None]
            acc = term if acc is None else acc + term
        y = _layer_norm(acc.reshape(rc, -1) + b_ref[...], g_ref[...], bb_ref[...])
        o_ref[0, r0:r0 + rc, :] = (y * _sigmoid(y)).astype(o_ref.dtype)


def _conv(hglu, conv_w, conv_b, ln_g, ln_b, *, ts, rc):
    bsz, s, c = hglu.shape
    assert s % ts == 0 and ts % CONV_HALO == 0 and CONV_HALO >= CONV_TAPS - 1
    blocks_per_tile = ts // CONV_HALO
    vec = pl.BlockSpec((1, c), lambda b, i: (0, 0))
    return pl.pallas_call(
        functools.partial(_conv_kernel, ts=ts, rc=rc),
        grid=(bsz, s // ts),
        in_specs=[pl.BlockSpec((1, ts, c), lambda b, i: (b, i, 0)),
                  pl.BlockSpec((1, CONV_HALO, c),
                               lambda b, i: (b, jnp.maximum(i * blocks_per_tile - 1, 0), 0)),
                  pl.BlockSpec((CONV_TAPS, SUBLANES, c), lambda b, i: (0, 0, 0)),
                  vec, vec, vec],
        out_specs=pl.BlockSpec((1, ts, c), lambda b, i: (b, i, 0)),
        out_shape=jax.ShapeDtypeStruct((bsz, s, c), BF16),
        scratch_shapes=[pltpu.VMEM((SUBLANES, CONV_HALO + ts, c), F32)],
        compiler_params=_cparams(2),
        name="conv",
    )(hglu, hglu, jnp.broadcast_to(conv_w[:, None, :], (CONV_TAPS, SUBLANES, c)),
      conv_b.reshape(1, c), ln_g.reshape(1, c), ln_b.reshape(1, c))


def _mix_kernel(o_ref, hc_ref, ga_ref, gb_ref, x_ref, woa_ref, wco_ref, bco_ref, wout_ref,
                g1_ref, b1_ref, wr_ref, br_ref, tri_ref,
                x1_ref, x1p_ref, mi_ref, mf_ref, cnt_ref, carry_scr, *, tm):
    i = pl.program_id(0)

    @pl.when(i == 0)
    def _init():
        carry_scr[...] = jnp.zeros(carry_scr.shape, F32)

    halves = [slice(0, tm // 2), slice(tm // 2, tm)]
    branch = [(jnp.dot(o_ref[h, :], woa_ref[...], preferred_element_type=F32),
               jnp.dot(hc_ref[h, :], wco_ref[...], preferred_element_type=F32) + bco_ref[...]) for h in halves]
    mixed = [jnp.dot((ga_ref[h, :].astype(F32) * y_a + gb_ref[h, :].astype(F32) * y_b).astype(BF16),
                     wout_ref[...], preferred_element_type=F32) for h, (y_a, y_b) in zip(halves, branch)]
    x1_halves = [_layer_norm(DN_ALPHA * x_ref[h, :] + mx, g1_ref[...], b1_ref[...]) for h, mx in zip(halves, mixed)]
    logits = jnp.concatenate([jnp.dot(xh.astype(BF16), wr_ref[...], preferred_element_type=F32)
                              for xh in x1_halves], axis=0) + br_ref[...]
    x1 = jnp.concatenate(x1_halves, axis=0)
    x1_ref[...] = x1
    _store_row_chunks(x1p_ref, (), _pack_halves(x1), tm)

    lane = lax.broadcasted_iota(I32, (tm, LANES), 1)
    g_lane = lane < N_GROUPS
    gl = jnp.where(g_lane, logits, NEG_INF)
    g_max = jnp.max(gl, axis=-1, keepdims=True)
    g_idx = jnp.min(jnp.where(gl == g_max, lane, LANES), axis=-1, keepdims=True)
    p_g = 1.0 / jnp.sum(jnp.where(g_lane, jnp.exp(gl - g_max), 0.0), axis=-1, keepdims=True)
    lo = N_GROUPS + g_idx * EXPERTS_PER_GROUP
    el = jnp.where(lane >= lo, jnp.where(lane < lo + EXPERTS_PER_GROUP, logits, NEG_INF), NEG_INF)
    v1 = jnp.max(el, axis=-1, keepdims=True)
    i1 = jnp.min(jnp.where(el == v1, lane, LANES), axis=-1, keepdims=True)
    el2 = jnp.where(lane == i1, NEG_INF, el)
    v2 = jnp.max(el2, axis=-1, keepdims=True)
    i2 = jnp.min(jnp.where(lane == i1, LANES, jnp.where(el2 == v2, lane, LANES)), axis=-1, keepdims=True)
    ex = jnp.exp(v2 - v1)
    w1 = 1.0 / (1.0 + ex)
    gate1 = p_g * w1
    gate2 = p_g * (ex * w1)
    e1 = i1 - N_GROUPS
    e2 = i2 - N_GROUPS

    carry = carry_scr[...]
    tri = tri_ref[...]
    ranks = []
    for e in (e1, e2):
        hit = lane == e
        oh = jnp.where(hit, 1.0, 0.0)
        before = jnp.dot(tri, oh.astype(BF16), preferred_element_type=F32) + carry
        ranks.append(jnp.sum(jnp.where(hit, before, 0.0), axis=-1, keepdims=True).astype(I32))
        carry = carry + jnp.sum(oh, axis=0, keepdims=True)
    carry_scr[...] = carry
    cnt_ref[...] = carry

    mi_ref[...] = jnp.where(lane == 0, e1, jnp.where(lane == 1, e2,
                            jnp.where(lane == 2, ranks[0], jnp.where(lane == 3, ranks[1], 0))))
    mf_ref[...] = jnp.where(lane == 0, gate1, jnp.where(lane == 1, gate2, 0.0))


def _mix(o_n, hc, ga, gb, x2, w_oa, w_co, b_co, w_out, ln_g, ln_b, w_r, b_r, *, tm):
    t, d = x2.shape
    assert d // 2 == ROW_CHUNKS * LANES
    row = lambda w: pl.BlockSpec((tm, w), lambda i: (i, 0))
    full = lambda a: pl.BlockSpec(a.shape, lambda i: (0,) * a.ndim, pipeline_mode=pl.Buffered(1))
    tri = (jnp.arange(tm)[:, None] > jnp.arange(tm)[None, :]).astype(BF16)
    b_co, ln_g, ln_b = b_co.reshape(1, d), ln_g.reshape(1, d), ln_b.reshape(1, d)
    return pl.pallas_call(
        functools.partial(_mix_kernel, tm=tm),
        grid=(t // tm,),
        in_specs=[row(d), row(d), row(d), row(d), row(d),
                  full(w_oa), full(w_co), full(b_co), full(w_out), full(ln_g), full(ln_b),
                  full(w_r), full(b_r), full(tri)],
        out_specs=[row(d), pl.BlockSpec((tm * ROW_CHUNKS, LANES), lambda i: (i, 0)), row(LANES), row(LANES),
                   pl.BlockSpec((1, LANES), lambda i: (0, 0))],
        out_shape=[jax.ShapeDtypeStruct((t, d), F32),
                   jax.ShapeDtypeStruct((t * ROW_CHUNKS, LANES), U32),
                   jax.ShapeDtypeStruct((t, LANES), I32),
                   jax.ShapeDtypeStruct((t, LANES), F32),
                   jax.ShapeDtypeStruct((1, LANES), F32)],
        scratch_shapes=[pltpu.VMEM((1, LANES), F32)],
        compiler_params=_cparams(1),
        name="mix",
    )(o_n, hc, ga, gb, x2, w_oa, w_co, b_co, w_out, ln_g, ln_b, w_r, b_r, tri)


def _row_copy(src, src_row, dst, dst_row, sem):
    def piece(row):
        return pl.ds(pl.multiple_of(row * ROW_CHUNKS, ROW_CHUNKS), ROW_CHUNKS)
    return pltpu.make_async_copy(src.at[piece(src_row), :], dst.at[piece(dst_row), :], sem)


def _dispatch_kernel(dest_ref, x_ref, xs_in_ref, xs_ref, sem, *, tm):
    del xs_in_ref
    base = pl.program_id(0) * tm

    def issue(g, carry):
        for u in range(ISSUE_UNROLL):
            r = g * ISSUE_UNROLL + u
            for c in range(TOP_K):
                _row_copy(x_ref, r, xs_ref, dest_ref[TOP_K * (base + r) + c], sem).start(priority=c)
        return carry

    lax.fori_loop(0, tm // ISSUE_UNROLL, issue, 0)
    for _ in range(TOP_K):
        pltpu.make_async_copy(x_ref, xs_ref.at[pl.ds(0, tm * ROW_CHUNKS), :], sem).wait()


def _dispatch(dest_flat, x1p, n_rows, *, tm):
    w = x1p.shape[1]
    t = x1p.shape[0] // ROW_CHUNKS
    assert tm % ISSUE_UNROLL == 0
    return pl.pallas_call(
        functools.partial(_dispatch_kernel, tm=tm),
        grid_spec=pltpu.PrefetchScalarGridSpec(
            num_scalar_prefetch=1,
            grid=(t // tm,),
            in_specs=[pl.BlockSpec((tm * ROW_CHUNKS, w), lambda i, dest: (i, 0)),
                      pl.BlockSpec(memory_space=pl.ANY)],
            out_specs=pl.BlockSpec(memory_space=pl.ANY),
            scratch_shapes=[pltpu.SemaphoreType.DMA(())]),
        out_shape=jax.ShapeDtypeStruct((n_rows * ROW_CHUNKS, w), U32),
        input_output_aliases={2: 0},
        compiler_params=_cparams(1),
        name="dispatch",
    )(dest_flat, x1p, jnp.zeros((n_rows * ROW_CHUNKS, w), U32))


def _expert_kernel(blk_e_ref, n_used_ref, xs_ref, wg_ref, wu_ref, wd_ref, y_ref, wg_b, wu_b, wd_b):
    i = pl.program_id(0)
    active = i < n_used_ref[0]
    tm = xs_ref.shape[0] // ROW_CHUNKS

    @pl.when(jnp.logical_not(active))
    def _unused_block():
        y_ref[...] = jnp.zeros(y_ref.shape, y_ref.dtype)

    @pl.when(active & ((i == 0) | (blk_e_ref[i] != blk_e_ref[jnp.maximum(i - 1, 0)])))
    def _new_expert():
        wg_b[...] = wg_ref[0].astype(BF16)
        wu_b[...] = wu_ref[0].astype(BF16)
        wd_b[...] = wd_ref[0].astype(BF16)

    @pl.when(active)
    def _():
        xb = _unpack_halves(_load_row_chunks(xs_ref, (), tm)).astype(BF16)
        g = jnp.dot(xb, wg_b[...], preferred_element_type=F32)
        u = jnp.dot(xb, wu_b[...], preferred_element_type=F32)
        hid = (g * _sigmoid(g) * u).astype(BF16)
        y = jnp.dot(hid, wd_b[...], preferred_element_type=F32)
        _store_row_chunks(y_ref, (), _pack_halves(y), tm)


def _experts(blk_e, n_used, xs, w_gate, w_up, w_down, *, tm):
    w = xs.shape[1]
    n_rows = xs.shape[0] // ROW_CHUNKS
    _, d, de = w_gate.shape
    blk = lambda i, blk_e, n_used: jnp.minimum(i, n_used[0] - 1)
    row_map = lambda i, blk_e, n_used: (blk(i, blk_e, n_used), 0)
    w_map = lambda i, blk_e, n_used: (blk_e[blk(i, blk_e, n_used)], 0, 0)
    return pl.pallas_call(
        _expert_kernel,
        grid_spec=pltpu.PrefetchScalarGridSpec(
            num_scalar_prefetch=2,
            grid=(n_rows // tm,),
            in_specs=[pl.BlockSpec((tm * ROW_CHUNKS, w), row_map),
                      pl.BlockSpec((1, d, de), w_map),
                      pl.BlockSpec((1, d, de), w_map),
                      pl.BlockSpec((1, de, d), w_map)],
            out_specs=pl.BlockSpec((tm * ROW_CHUNKS, w), lambda i, blk_e, n_used: (i, 0)),
            scratch_shapes=[pltpu.VMEM((d, de), BF16), pltpu.VMEM((d, de), BF16), pltpu.VMEM((de, d), BF16)]),
        out_shape=jax.ShapeDtypeStruct((n_rows * ROW_CHUNKS, w), U32),
        compiler_params=_cparams(1),
        name="experts",
    )(blk_e, n_used, xs, w_gate, w_up, w_down)


def _combine_kernel(dest_ref, x1_ref, mf_ref, g_ref, b_ref, yp_ref, o_ref, buf, sems, *, tm, n_tiles):
    i = pl.program_id(0)

    def issue(tile, slot):
        base = tile * tm

        def body(g, carry):
            for u in range(ISSUE_UNROLL):
                r = g * ISSUE_UNROLL + u
                for c in range(TOP_K):
                    _row_copy(yp_ref, dest_ref[TOP_K * (base + r) + c], buf.at[slot, c], r,
                              sems.at[slot]).start(priority=c)
            return carry

        lax.fori_loop(0, tm // ISSUE_UNROLL, body, 0)

    @pl.when(i == 0)
    def _first():
        issue(0, 0)

    @pl.when(i + 1 < n_tiles)
    def _next():
        issue(i + 1, (i + 1) % 2)

    slot = i % 2
    for c in range(TOP_K):
        pltpu.make_async_copy(yp_ref.at[pl.ds(0, tm * ROW_CHUNKS), :], buf.at[slot, c], sems.at[slot]).wait()

    gates = mf_ref[...]
    ffn = (gates[:, 0:1] * _unpack_halves(_load_row_chunks(buf, (slot, 0), tm))
           + gates[:, 1:2] * _unpack_halves(_load_row_chunks(buf, (slot, 1), tm)))
    o_ref[...] = _layer_norm(DN_ALPHA * x1_ref[...] + ffn, g_ref[...], b_ref[...])


def _combine(dest_flat, x1, mf, ln_g, ln_b, yp, *, tm):
    t, d = x1.shape
    n_tiles = t // tm
    assert tm % ISSUE_UNROLL == 0
    vec = pl.BlockSpec((1, d), lambda i, dest: (0, 0))
    return pl.pallas_call(
        functools.partial(_combine_kernel, tm=tm, n_tiles=n_tiles),
        grid_spec=pltpu.PrefetchScalarGridSpec(
            num_scalar_prefetch=1,
            grid=(n_tiles,),
            in_specs=[pl.BlockSpec((tm, d), lambda i, dest: (i, 0)),
                      pl.BlockSpec((tm, LANES), lambda i, dest: (i, 0)),
                      vec, vec,
                      pl.BlockSpec(memory_space=pl.ANY)],
            out_specs=pl.BlockSpec((tm, d), lambda i, dest: (i, 0)),
            scratch_shapes=[pltpu.VMEM((2, TOP_K, tm * ROW_CHUNKS, LANES), U32),
                            pltpu.SemaphoreType.DMA((2,))]),
        out_shape=jax.ShapeDtypeStruct((t, d), F32),
        compiler_params=_cparams(1),
        name="combine",
    )(dest_flat, x1, mf, ln_g.reshape(1, d), ln_b.reshape(1, d), yp)


def _tile(n, pref):
    return pref if n % pref == 0 else n


def kernel(x, w_in, b_in, diff_lambda, head_norm_g, w_o_attn, rel_bias, conv_w, conv_b, conv_ln_g,
           conv_ln_b, w_conv_out, b_conv_out, w_out, ln1_g, ln1_b, router_g_w, router_g_b,
           router_e_w, router_e_b, expert_w_gate, expert_w_up, expert_w_down, ln2_g, ln2_b):
    bsz, s, d = x.shape
    t = bsz * s
    width = N_HEADS * 2 * HEAD_DIM
    assert w_in.shape[0] == DEPTH and w_in.shape[2] == 7 * width and width == d
    assert conv_w.shape[2] == width and TOP_K == 2
    tm_moe = _tile(t, 512)

    for li in range(DEPTH):
        lam_init = 0.8 - 0.6 * math.exp(-0.3 * li)
        x2 = x.reshape(t, d)
        q, k, v, hglu, ga, gb = _proj(
            x2, w_in[li].astype(BF16), b_in[li].reshape(1, -1), width=width, tm=_tile(t, 512),
            chunk=_tile(width, 512), q_scale=HEAD_DIM ** -0.5 * LOG2E)
        o_n = _attention(q.reshape(bsz, s, width), k.reshape(bsz, s, width), v.reshape(bsz, s, width),
                         rel_bias, diff_lambda[li], head_norm_g[li], tk=256 if s % 512 == 0 else 128,
                         lam_init=lam_init)
        hc = _conv(hglu.reshape(bsz, s, width), conv_w[li], conv_b[li], conv_ln_g[li], conv_ln_b[li],
                   ts=_tile(s, 256), rc=16)

        n_r = N_GROUPS + N_EXPERTS
        w_r = jnp.pad(jnp.concatenate([router_g_w[li], router_e_w[li]], axis=1),
                      ((0, 0), (0, LANES - n_r))).astype(BF16)
        b_r = jnp.pad(jnp.concatenate([router_g_b[li], router_e_b[li]]), (0, LANES - n_r)).reshape(1, LANES)
        x1, x1p, mi, mf, cnt = _mix(
            o_n.reshape(t, width), hc.reshape(t, width), ga, gb, x2,
            w_o_attn[li].astype(BF16), w_conv_out[li].astype(BF16), b_conv_out[li],
            w_out[li].astype(BF16), ln1_g[li], ln1_b[li], w_r, b_r, tm=_tile(t, 512))

        counts = cnt[0, :N_EXPERTS].astype(I32)
        padded = (counts + tm_moe - 1) // tm_moe * tm_moe
        pad_ends = jnp.cumsum(padded)
        pad_starts = pad_ends - padded
        chosen = mi[:, 0:TOP_K, None] == jnp.arange(N_EXPERTS, dtype=I32)[None, None, :]
        dest_flat = (jnp.sum(jnp.where(chosen, pad_starts[None, None, :], 0), axis=-1)
                     + mi[:, TOP_K:2 * TOP_K]).reshape(t * TOP_K)
        n_rows = t * TOP_K + N_EXPERTS * tm_moe
        n_blocks = n_rows // tm_moe
        blk_start = jnp.arange(n_blocks, dtype=I32) * tm_moe
        blk_e = jnp.minimum(jnp.sum(blk_start[:, None] >= pad_ends[None, :], axis=1),
                            N_EXPERTS - 1).astype(I32)
        n_used = (pad_ends[-1:] // tm_moe).astype(I32)

        xs = _dispatch(dest_flat, x1p, n_rows, tm=_tile(t, 512))
        yp = _experts(blk_e, n_used, xs, expert_w_gate[li], expert_w_up[li], expert_w_down[li], tm=tm_moe)
        x = _combine(dest_flat, x1, mf, ln2_g[li], ln2_b[li], yp, tm=_tile(t, 256)).reshape(bsz, s, d)
    return x
```

```python
import functools
import math

import jax
import jax.numpy as jnp
from jax import lax
from jax.experimental import pallas as pl
from jax.experimental.pallas import tpu as pltpu

F32 = jnp.float32
BF16 = jnp.bfloat16
U32 = jnp.uint32
I32 = jnp.int32

N_HEADS = 8
HEAD_DIM = 64
CONV_TAPS = 31
REL_BUCKETS = 32
REL_MAX_DIST = 128
N_GROUPS = 4
EXPERTS_PER_GROUP = 8
N_EXPERTS = N_GROUPS * EXPERTS_PER_GROUP
TOP_K = 2
DEPTH = 1
DN_ALPHA = (2.0 * DEPTH) ** 0.25
LN_EPS = 1e-5
NEG_INF = -1e30
LOG2E = 1.4426950408889634

LANES = 128
SUBLANES = 8
VMEM_LIMIT = 56 * 1024 * 1024
CONV_HALO = 32
CONV_SHIFT_ROWS = 56
HEADS_PER_STEP = 4
ONES_ROWS = 16
ROW_CHUNKS = 4
ISSUE_UNROLL = 8


def _cparams(n_axes, flags=None):
    return pltpu.CompilerParams(dimension_semantics=("arbitrary",) * n_axes,
                                vmem_limit_bytes=VMEM_LIMIT, flags=flags)


def _sigmoid(x):
    return 1.0 / (1.0 + jnp.exp(-x))


def _layer_norm(z, g, b):
    mu = jnp.mean(z, axis=-1, keepdims=True)
    zc = z - mu
    var = jnp.mean(zc * zc, axis=-1, keepdims=True)
    return zc * lax.rsqrt(var + LN_EPS) * g + b


def _pack_halves(y):
    n = y.shape[1] // 2
    bits = pltpu.bitcast(y.astype(BF16).astype(F32), U32)
    return (bits[:, :n] >> 16) | (bits[:, n:] & jnp.uint32(0xFFFF0000))


def _unpack_halves(w):
    lo = pltpu.bitcast(w << 16, F32)
    hi = pltpu.bitcast(w & jnp.uint32(0xFFFF0000), F32)
    return jnp.concatenate([lo, hi], axis=1)


def _store_row_chunks(ref, prefix, packed, m):
    for q in range(ROW_CHUNKS):
        ref[prefix + (pl.ds(q, m, stride=ROW_CHUNKS), slice(None))] = packed[:, q * LANES:(q + 1) * LANES]


def _load_row_chunks(ref, prefix, m):
    return jnp.concatenate([ref[prefix + (pl.ds(q, m, stride=ROW_CHUNKS), slice(None))]
                            for q in range(ROW_CHUNKS)], axis=1)


def _zero_after(value):
    bits = pltpu.bitcast(value[0:SUBLANES, 0:LANES], U32)
    return pltpu.bitcast((bits >> 16) >> 16, F32)[0:1, :]


def _conv_rows(sh, w_ref, b_ref, g_ref, bb_ref, o_ref, out_row0, r0, rc, anchor):
    off = CONV_HALO - (CONV_TAPS - 1)
    acc = None
    for t in range(CONV_TAPS):
        k, base = (off + t) % SUBLANES, (off + t) // SUBLANES * SUBLANES
        slab = sh[k, r0 + base:r0 + base + rc, :].reshape(rc // SUBLANES, SUBLANES, -1)
        term = slab * w_ref[t][None]
        acc = term if acc is None else acc + term
    bias = b_ref[...]
    if anchor is not None:
        bias = bias + jnp.tile(anchor, (1, bias.shape[1] // LANES))
    y = _layer_norm(acc.reshape(rc, -1) + bias, g_ref[...], bb_ref[...])
    o_ref[out_row0 + r0:out_row0 + r0 + rc, :] = (y * _sigmoid(y)).astype(o_ref.dtype)


def _proj_conv_kernel(x_ref, w_ref, b_ref, cw_ref, cb_ref, cg_ref, cbb_ref,
                      q_ref, k_ref, v_ref, ga_ref, gb_ref, hc_ref, sh, halo_scr,
                      *, width, chunk, q_scale, tm, ts, rc, tiles_per_seq):
    i = pl.program_id(0)
    xb = x_ref[...].astype(BF16)

    def lin(col0):
        return (jnp.dot(xb, w_ref[:, col0:col0 + chunk], preferred_element_type=F32)
                + b_ref[:, col0:col0 + chunk])

    glu = jnp.concatenate([(lin(3 * width + c) * _sigmoid(lin(4 * width + c))).astype(BF16).astype(F32)
                           for c in range(0, width, chunk)], axis=1)

    n_ext = CONV_HALO + ts
    n_sh = n_ext - SUBLANES

    def load_half(half):
        if half == 0:
            prev = halo_scr[...]
            sh[0, 0:CONV_HALO, :] = jnp.where(i % tiles_per_seq != 0, prev, jnp.zeros_like(prev))
        else:
            sh[0, 0:CONV_HALO, :] = glu[half * ts - CONV_HALO:half * ts]
        sh[0, CONV_HALO:n_ext, :] = glu[half * ts:(half + 1) * ts]
        for k in range(1, SUBLANES):
            for r0 in range(0, n_sh, CONV_SHIFT_ROWS):
                n = min(CONV_SHIFT_ROWS, n_sh - r0)
                sh[k, r0:r0 + n, :] = sh[0, r0 + k:r0 + k + n, :]

    def epilogue(ref, group, c, fn):
        def run():
            z = lin(group * width + c)
            ref[:, c:c + chunk] = fn(z).astype(BF16)
            return _zero_after(z)
        return run

    others = []
    for c in range(0, width, chunk):
        others += [epilogue(q_ref, 0, c, lambda z: z * q_scale), epilogue(k_ref, 1, c, lambda z: z),
                   epilogue(v_ref, 2, c, lambda z: z), epilogue(ga_ref, 5, c, _sigmoid),
                   epilogue(gb_ref, 6, c, _sigmoid)]
    conv_chunks = [(half, r0) for half in range(tm // ts) for r0 in range(0, ts, rc)]
    every = max(1, len(conv_chunks) // len(others))
    anchor = None
    for idx, (half, r0) in enumerate(conv_chunks):
        if r0 == 0:
            load_half(half)
        _conv_rows(sh, cw_ref, cb_ref, cg_ref, cbb_ref, hc_ref, half * ts, r0, rc, anchor)
        if idx % every == every - 1 and others:
            anchor = others.pop(0)()
    for run in others:
        run()
    halo_scr[...] = glu[tm - CONV_HALO:tm]


def _proj_conv(x2, w_in, b_in, conv_w, conv_b, ln_g, ln_b, *, width, tm, chunk, q_scale, ts, rc, seq):
    t, d = x2.shape
    n_cols = w_in.shape[1]
    c = width
    assert seq % tm == 0 and tm % ts == 0 and ts % rc == 0 and CONV_HALO >= CONV_TAPS - 1 and ts >= CONV_HALO
    out = jax.ShapeDtypeStruct((t, width), BF16)
    row_spec = pl.BlockSpec((tm, width), lambda i: (i, 0))
    const = lambda shape: pl.BlockSpec(shape, lambda i: (0,) * len(shape), pipeline_mode=pl.Buffered(1))
    kernel = functools.partial(_proj_conv_kernel, width=width, chunk=chunk, q_scale=q_scale, tm=tm, ts=ts,
                               rc=rc, tiles_per_seq=seq // tm)
    return pl.pallas_call(
        kernel,
        grid=(t // tm,),
        in_specs=[pl.BlockSpec((tm, d), lambda i: (i, 0)),
                  const((d, n_cols)), const((1, n_cols)),
                  const((CONV_TAPS, SUBLANES, c)), const((1, c)), const((1, c)), const((1, c))],
        out_specs=[row_spec] * 6,
        out_shape=[out] * 6,
        scratch_shapes=[pltpu.VMEM((SUBLANES, CONV_HALO + ts, c), F32), pltpu.VMEM((CONV_HALO, c), F32)],
        compiler_params=_cparams(1),
        name="proj_conv",
    )(x2, w_in, b_in, jnp.broadcast_to(conv_w[:, None, :], (CONV_TAPS, SUBLANES, c)),
      conv_b.reshape(1, c), ln_g.reshape(1, c), ln_b.reshape(1, c))


def _sublane_all(x, op):
    for shift in (4, 2, 1):
        x = op(x, pltpu.roll(x, shift, 0))
    return x


def _attn_kernel(tab_ref, bucket_ref, lam_ref, q_ref, k_ref, v_ref, g_ref, o_ref,
                 bias_scr, qz_scr, vt_scr, s0_scr, s1_scr, p0_scr, p1_scr, a0_scr, a1_scr,
                 m_scr, acc_scr, *, tq, tk, dh, lam_init):
    hp = pl.program_id(0)
    b = pl.program_id(1)
    qi = pl.program_id(2)
    hw = 2 * dh
    acc_rows = hw + ONES_ROWS
    n_kv = vt_scr.shape[1]
    sw = tk
    n_sub = 2 * tq // sw
    heads = range(HEADS_PER_STEP)

    @pl.when((b == 0) & (qi == 0))
    def _build_bias():
        for hh in heads:
            h = hp * HEADS_PER_STEP + hh
            far = tab_ref[REL_BUCKETS - 1, h]
            for t in range(3):
                for c in range(tq // sw):
                    bk = bucket_ref[t, :, c * sw:(c + 1) * sw]
                    acc = jnp.full((tk, sw), NEG_INF, F32)
                    for r in range(REL_BUCKETS):
                        acc = jnp.where(bk == r, (tab_ref[r, h] - far) * LOG2E, acc)
                    bias_scr[hh, t, :, c * sw:(c + 1) * sw] = acc

    @pl.when(qi == 0)
    def _transpose_values():
        for hh in heads:
            for jj in range(n_kv):
                vt_scr[hh, jj, 0:hw, :] = (v_ref[0, jj * tk:(jj + 1) * tk, hh * hw:(hh + 1) * hw]
                                           .astype(F32).T.astype(BF16))
                vt_scr[hh, jj, hw:hw + ONES_ROWS, :] = jnp.ones((ONES_ROWS, tk), BF16)

    lane = lax.broadcasted_iota(I32, (tq, hw), 1)
    for hh in heads:
        q = q_ref[0, :, hh * hw:(hh + 1) * hw]
        zero = jnp.zeros_like(q)
        qz_scr[hh, 0:tq, :] = jnp.where(lane < dh, q, zero)
        qz_scr[hh, tq:2 * tq, :] = jnp.where(lane >= dh, q, zero)
    m_scr[...] = jnp.full(m_scr.shape, NEG_INF, F32)
    acc_scr[...] = jnp.zeros(acc_scr.shape, F32)

    s_bufs, p_bufs, a_bufs = (s0_scr, s1_scr), (p0_scr, p1_scr), (a0_scr, a1_scr)
    p1_scr[...] = jnp.zeros(p1_scr.shape, BF16)
    a1_scr[...] = jnp.ones(a1_scr.shape, F32)

    all_cols = (slice(0, 2 * tq),)
    late_cols = (slice(tk, tq), slice(tq + tk, 2 * tq))

    def issue_scores(j, slot, cols=all_cols):
        rows = (slice(j * tk, (j + 1) * tk) if isinstance(j, int)
                else pl.ds(pl.multiple_of(j * tk, tk), tk))
        for hh in heads:
            kj = k_ref[0, rows, hh * hw:(hh + 1) * hw]
            for cs in cols:
                s_bufs[slot][hh, :, cs] = lax.dot_general(kj, qz_scr[hh, cs, :], (((1,), (1,)), ((), ())),
                                                          preferred_element_type=F32)

    def accumulate(hh, j_prev, slot_prev, cs=slice(0, 2 * tq)):
        n = cs.stop - cs.start
        pv = jnp.dot(vt_scr[hh, j_prev], p_bufs[slot_prev][hh, :, cs], preferred_element_type=F32)
        acc3 = acc_scr[hh, :, cs].reshape(acc_rows // SUBLANES, SUBLANES, n)
        return (a_bufs[slot_prev][hh, :, cs][None] * acc3).reshape(acc_rows, n) + pv

    def step(j, bias_idx, slot, issue_next=True, next_is_last=False, last=False):
        if issue_next:
            issue_scores(j + 1, 1 - slot, late_cols if next_is_last else all_cols)
        acc_new = [accumulate(hh, jnp.maximum(j - 1, 0), 1 - slot) for hh in heads]
        for hh in heads:
            for c in range(n_sub):
                cs = slice(c * sw, (c + 1) * sw)
                qs = (c * sw) % tq
                if last and qs < tk:
                    continue
                s = s_bufs[slot][hh, :, cs]
                if bias_idx is not None:
                    s = s + bias_scr[hh, bias_idx, :, qs:qs + sw]
                s3 = s.reshape(tk // SUBLANES, SUBLANES, sw)
                m_prev = m_scr[hh, :, cs]
                m_new = jnp.maximum(m_prev, _sublane_all(jnp.max(s3, axis=0), jnp.maximum))
                p3 = jnp.exp2(s3 - m_new[None])
                p_bufs[slot][hh, :, cs] = p3.reshape(tk, sw).astype(BF16)
                a_bufs[slot][hh, :, cs] = jnp.exp2(m_prev - m_new)
                m_scr[hh, :, cs] = m_new
        for hh in heads:
            acc_scr[hh] = acc_new[hh]

    issue_scores(0, 0)

    @pl.when(qi == 0)
    def _first_query_tile():
        step(0, 1, 0, next_is_last=True)
        step(1, 2, 1, issue_next=False, last=True)

    @pl.when(qi >= 1)
    def _later_query_tiles():
        def far_pair(i, carry):
            step(2 * i, None, 0)
            step(2 * i + 1, None, 1)
            return carry

        lax.fori_loop(0, qi - 1, far_pair, 0)
        step(2 * qi - 2, None, 0)
        step(2 * qi - 1, 0, 1)
        step(2 * qi, 1, 0, next_is_last=True)
        step(2 * qi + 1, 2, 1, issue_next=False, last=True)

    drained = [[accumulate(hh, 2 * qi + 1, 1, cs) for cs in late_cols] for hh in heads]
    for hh in heads:
        for cs, val in zip(late_cols, drained[hh]):
            acc_scr[hh, :, cs] = val

    lv = lam_ref[...]
    lam = (jnp.exp(jnp.sum(lv[0:1] * lv[1:2], axis=-1, keepdims=True))
           - jnp.exp(jnp.sum(lv[2:3] * lv[3:4], axis=-1, keepdims=True)) + lam_init)
    gain = g_ref[...] * (1.0 - lam_init)
    for hh in heads:
        acc_all = acc_scr[hh].reshape(acc_rows // SUBLANES, SUBLANES, 2 * tq)
        inv_l = 1.0 / acc_all[hw // SUBLANES]
        acc3 = acc_all[0:hw // SUBLANES]
        o3 = acc3[:, :, 0:tq] * inv_l[None, :, 0:tq] - lam * (acc3[:, :, tq:2 * tq] * inv_l[None, :, tq:2 * tq])
        ms = _sublane_all(jnp.sum(o3 * o3, axis=0), jnp.add) * (1.0 / hw)
        y = (o3 * lax.rsqrt(ms + LN_EPS)[None]).reshape(hw, tq) * gain
        o_ref[0, :, hh * hw:(hh + 1) * hw] = y.T.astype(o_ref.dtype)


def _rel_bucket(dist):
    max_exact = REL_BUCKETS // 2
    d = jnp.maximum(dist, 1).astype(F32)
    large = max_exact + (jnp.log(d / max_exact) / math.log(REL_MAX_DIST / max_exact)
                         * (REL_BUCKETS - max_exact)).astype(I32)
    large = jnp.minimum(large, REL_BUCKETS - 1)
    return jnp.where(dist < max_exact, dist, large)


def _near_buckets(tq, tk):
    tiles = []
    for rel_tile in (-1, 0, 1):
        dist = (jnp.arange(tq, dtype=I32)[None, :]
                - (rel_tile * tk + jnp.arange(tk, dtype=I32))[:, None])
        tiles.append(jnp.where(dist >= 0, _rel_bucket(jnp.maximum(dist, 0)), -1))
    return jnp.stack(tiles).astype(I32)


def _attention(q, k, v, rel_bias, diff_lambda, head_norm_g, *, tk, lam_init):
    bsz, s, width = q.shape
    hw = width // N_HEADS
    tq = 2 * tk
    hps, pw = HEADS_PER_STEP, HEADS_PER_STEP * hw
    assert hw == 2 * HEAD_DIM and hw % LANES == 0 and s % tq == 0 and N_HEADS % hps == 0
    assert tk + 1 >= REL_MAX_DIST
    kernel = functools.partial(_attn_kernel, tq=tq, tk=tk, dh=HEAD_DIM, lam_init=lam_init)
    stat = pltpu.VMEM((hps, SUBLANES, 2 * tq), F32)
    return pl.pallas_call(
        kernel,
        grid=(N_HEADS // hps, bsz, s // tq),
        in_specs=[pl.BlockSpec(memory_space=pltpu.SMEM),
                  pl.BlockSpec((3, tk, tq), lambda h, b, i: (0, 0, 0)),
                  pl.BlockSpec((4, HEAD_DIM), lambda h, b, i: (0, 0)),
                  pl.BlockSpec((1, tq, pw), lambda h, b, i: (b, i, h)),
                  pl.BlockSpec((1, s, pw), lambda h, b, i: (b, 0, h)),
                  pl.BlockSpec((1, s, pw), lambda h, b, i: (b, 0, h)),
                  pl.BlockSpec((hw, 1), lambda h, b, i: (0, 0))],
        out_specs=pl.BlockSpec((1, tq, pw), lambda h, b, i: (b, i, h)),
        out_shape=jax.ShapeDtypeStruct((bsz, s, width), BF16),
        scratch_shapes=[pltpu.VMEM((hps, 3, tk, tq), F32),
                        pltpu.VMEM((hps, 2 * tq, hw), BF16),
                        pltpu.VMEM((hps, s // tk, hw + ONES_ROWS, tk), BF16),
                        pltpu.VMEM((hps, tk, 2 * tq), F32),
                        pltpu.VMEM((hps, tk, 2 * tq), F32),
                        pltpu.VMEM((hps, tk, 2 * tq), BF16),
                        pltpu.VMEM((hps, tk, 2 * tq), BF16),
                        stat, stat,
                        stat,
                        pltpu.VMEM((hps, hw + ONES_ROWS, 2 * tq), F32)],
        compiler_params=_cparams(3),
        name="attn",
    )(rel_bias, _near_buckets(tq, tk), diff_lambda, q, k, v, head_norm_g.reshape(hw, 1))


def _mix_kernel(o_ref, hc_ref, ga_ref, gb_ref, x_ref, woa_ref, wco_ref, bco_ref, wout_ref,
                g1_ref, b1_ref, wr_ref, br_ref, tri_ref,
                x1_ref, x1p_ref, mi_ref, mf_ref, cnt_ref, carry_scr, *, tm):
    i = pl.program_id(0)

    @pl.when(i == 0)
    def _init():
        carry_scr[...] = jnp.zeros(carry_scr.shape, F32)

    halves = [slice(0, tm // 2), slice(tm // 2, tm)]
    branch = [(jnp.dot(o_ref[h, :], woa_ref[...], preferred_element_type=F32),
               jnp.dot(hc_ref[h, :], wco_ref[...], preferred_element_type=F32) + bco_ref[...]) for h in halves]
    mixed = [jnp.dot((ga_ref[h, :].astype(F32) * y_a + gb_ref[h, :].astype(F32) * y_b).astype(BF16),
                     wout_ref[...], preferred_element_type=F32) for h, (y_a, y_b) in zip(halves, branch)]
    x1_halves = [_layer_norm(DN_ALPHA * x_ref[h, :] + mx, g1_ref[...], b1_ref[...]) for h, mx in zip(halves, mixed)]
    logits = jnp.concatenate([jnp.dot(xh.astype(BF16), wr_ref[...], preferred_element_type=F32)
                              for xh in x1_halves], axis=0) + br_ref[...]
    x1 = jnp.concatenate(x1_halves, axis=0)
    x1_ref[...] = x1
    _store_row_chunks(x1p_ref, (), _pack_halves(x1), tm)

    lane = lax.broadcasted_iota(I32, (tm, LANES), 1)
    g_lane = lane < N_GROUPS
    gl = jnp.where(g_lane, logits, NEG_INF)
    g_max = jnp.max(gl, axis=-1, keepdims=True)
    g_idx = jnp.min(jnp.where(gl == g_max, lane, LANES), axis=-1, keepdims=True)
    p_g = 1.0 / jnp.sum(jnp.where(g_lane, jnp.exp(gl - g_max), 0.0), axis=-1, keepdims=True)
    lo = N_GROUPS + g_idx * EXPERTS_PER_GROUP
    el = jnp.where(lane >= lo, jnp.where(lane < lo + EXPERTS_PER_GROUP, logits, NEG_INF), NEG_INF)
    v1 = jnp.max(el, axis=-1, keepdims=True)
    i1 = jnp.min(jnp.where(el == v1, lane, LANES), axis=-1, keepdims=True)
    el2 = jnp.where(lane == i1, NEG_INF, el)
    v2 = jnp.max(el2, axis=-1, keepdims=True)
    i2 = jnp.min(jnp.where(lane == i1, LANES, jnp.where(el2 == v2, lane, LANES)), axis=-1, keepdims=True)
    ex = jnp.exp(v2 - v1)
    w1 = 1.0 / (1.0 + ex)
    gate1 = p_g * w1
    gate2 = p_g * (ex * w1)
    e1 = i1 - N_GROUPS
    e2 = i2 - N_GROUPS

    carry = carry_scr[...]
    tri = tri_ref[...]
    ranks = []
    for e in (e1, e2):
        hit = lane == e
        oh = jnp.where(hit, 1.0, 0.0)
        before = jnp.dot(tri, oh.astype(BF16), preferred_element_type=F32) + carry
        ranks.append(jnp.sum(jnp.where(hit, before, 0.0), axis=-1, keepdims=True).astype(I32))
        carry = carry + jnp.sum(oh, axis=0, keepdims=True)
    carry_scr[...] = carry
    cnt_ref[...] = carry

    mi_ref[...] = jnp.where(lane == 0, e1, jnp.where(lane == 1, e2,
                            jnp.where(lane == 2, ranks[0], jnp.where(lane == 3, ranks[1], 0))))
    mf_ref[...] = jnp.where(lane == 0, gate1, jnp.where(lane == 1, gate2, 0.0))


def _mix(o_n, hc, ga, gb, x2, w_oa, w_co, b_co, w_out, ln_g, ln_b, w_r, b_r, *, tm):
    t, d = x2.shape
    assert d // 2 == ROW_CHUNKS * LANES
    row = lambda w: pl.BlockSpec((tm, w), lambda i: (i, 0))
    full = lambda a: pl.BlockSpec(a.shape, lambda i: (0,) * a.ndim, pipeline_mode=pl.Buffered(1))
    tri = (jnp.arange(tm)[:, None] > jnp.arange(tm)[None, :]).astype(BF16)
    b_co, ln_g, ln_b = b_co.reshape(1, d), ln_g.reshape(1, d), ln_b.reshape(1, d)
    return pl.pallas_call(
        functools.partial(_mix_kernel, tm=tm),
        grid=(t // tm,),
        in_specs=[row(d), row(d), row(d), row(d), row(d),
                  full(w_oa), full(w_co), full(b_co), full(w_out), full(ln_g), full(ln_b),
                  full(w_r), full(b_r), full(tri)],
        out_specs=[row(d), pl.BlockSpec((tm * ROW_CHUNKS, LANES), lambda i: (i, 0)), row(LANES), row(LANES),
                   pl.BlockSpec((1, LANES), lambda i: (0, 0))],
        out_shape=[jax.ShapeDtypeStruct((t, d), F32),
                   jax.ShapeDtypeStruct((t * ROW_CHUNKS, LANES), U32),
                   jax.ShapeDtypeStruct((t, LANES), I32),
                   jax.ShapeDtypeStruct((t, LANES), F32),
                   jax.ShapeDtypeStruct((1, LANES), F32)],
        scratch_shapes=[pltpu.VMEM((1, LANES), F32)],
        compiler_params=_cparams(1),
        name="mix",
    )(o_n, hc, ga, gb, x2, w_oa, w_co, b_co, w_out, ln_g, ln_b, w_r, b_r, tri)


def _row_copy(src, src_row, dst, dst_row, sem):
    def piece(row):
        return pl.ds(pl.multiple_of(row * ROW_CHUNKS, ROW_CHUNKS), ROW_CHUNKS)
    return pltpu.make_async_copy(src.at[piece(src_row), :], dst.at[piece(dst_row), :], sem)


def _dispatch_kernel(dest_ref, x_ref, xs_in_ref, xs_ref, sem, *, tm):
    del xs_in_ref
    base = pl.program_id(0) * tm

    def issue(g, carry):
        for u in range(ISSUE_UNROLL):
            r = g * ISSUE_UNROLL + u
            for c in range(TOP_K):
                _row_copy(x_ref, r, xs_ref, dest_ref[TOP_K * (base + r) + c], sem).start(priority=c)
        return carry

    lax.fori_loop(0, tm // ISSUE_UNROLL, issue, 0)
    for _ in range(TOP_K):
        pltpu.make_async_copy(x_ref, xs_ref.at[pl.ds(0, tm * ROW_CHUNKS), :], sem).wait()


def _dispatch(dest_flat, x1p, n_rows, *, tm):
    w = x1p.shape[1]
    t = x1p.shape[0] // ROW_CHUNKS
    assert tm % ISSUE_UNROLL == 0
    return pl.pallas_call(
        functools.partial(_dispatch_kernel, tm=tm),
        grid_spec=pltpu.PrefetchScalarGridSpec(
            num_scalar_prefetch=1,
            grid=(t // tm,),
            in_specs=[pl.BlockSpec((tm * ROW_CHUNKS, w), lambda i, dest: (i, 0)),
                      pl.BlockSpec(memory_space=pl.ANY)],
            out_specs=pl.BlockSpec(memory_space=pl.ANY),
            scratch_shapes=[pltpu.SemaphoreType.DMA(())]),
        out_shape=jax.ShapeDtypeStruct((n_rows * ROW_CHUNKS, w), U32),
        input_output_aliases={2: 0},
        compiler_params=_cparams(1),
        name="dispatch",
    )(dest_flat, x1p, jnp.zeros((n_rows * ROW_CHUNKS, w), U32))


def _expert_kernel(blk_e_ref, n_used_ref, xs_ref, wg_ref, wu_ref, wd_ref, y_ref, wg_b, wu_b, wd_b):
    i = pl.program_id(0)
    active = i < n_used_ref[0]
    tm = xs_ref.shape[0] // ROW_CHUNKS

    @pl.when(jnp.logical_not(active))
    def _unused_block():
        y_ref[...] = jnp.zeros(y_ref.shape, y_ref.dtype)

    @pl.when(active & ((i == 0) | (blk_e_ref[i] != blk_e_ref[jnp.maximum(i - 1, 0)])))
    def _new_expert():
        wg_b[...] = wg_ref[0].astype(BF16)
        wu_b[...] = wu_ref[0].astype(BF16)
        wd_b[...] = wd_ref[0].astype(BF16)

    @pl.when(active)
    def _():
        xb = _unpack_halves(_load_row_chunks(xs_ref, (), tm)).astype(BF16)
        g = jnp.dot(xb, wg_b[...], preferred_element_type=F32)
        u = jnp.dot(xb, wu_b[...], preferred_element_type=F32)
        hid = (g * _sigmoid(g) * u).astype(BF16)
        y = jnp.dot(hid, wd_b[...], preferred_element_type=F32)
        _store_row_chunks(y_ref, (), _pack_halves(y), tm)


def _experts(blk_e, n_used, xs, w_gate, w_up, w_down, *, tm):
    w = xs.shape[1]
    n_rows = xs.shape[0] // ROW_CHUNKS
    _, d, de = w_gate.shape
    blk = lambda i, blk_e, n_used: jnp.minimum(i, n_used[0] - 1)
    row_map = lambda i, blk_e, n_used: (blk(i, blk_e, n_used), 0)
    w_map = lambda i, blk_e, n_used: (blk_e[blk(i, blk_e, n_used)], 0, 0)
    return pl.pallas_call(
        _expert_kernel,
        grid_spec=pltpu.PrefetchScalarGridSpec(
            num_scalar_prefetch=2,
            grid=(n_rows // tm,),
            in_specs=[pl.BlockSpec((tm * ROW_CHUNKS, w), row_map),
                      pl.BlockSpec((1, d, de), w_map),
                      pl.BlockSpec((1, d, de), w_map),
                      pl.BlockSpec((1, de, d), w_map)],
            out_specs=pl.BlockSpec((tm * ROW_CHUNKS, w), lambda i, blk_e, n_used: (i, 0)),
            scratch_shapes=[pltpu.VMEM((d, de), BF16), pltpu.VMEM((d, de), BF16), pltpu.VMEM((de, d), BF16)]),
        out_shape=jax.ShapeDtypeStruct((n_rows * ROW_CHUNKS, w), U32),
        compiler_params=_cparams(1),
        name="experts",
    )(blk_e, n_used, xs, w_gate, w_up, w_down)


def _combine_kernel(dest_ref, x1_ref, mf_ref, g_ref, b_ref, yp_ref, o_ref, buf, sems, *, tm, n_tiles):
    i = pl.program_id(0)

    def issue(tile, slot):
        base = tile * tm

        def body(g, carry):
            for u in range(ISSUE_UNROLL):
                r = g * ISSUE_UNROLL + u
                for c in range(TOP_K):
                    _row_copy(yp_ref, dest_ref[TOP_K * (base + r) + c], buf.at[slot, c], r,
                              sems.at[slot]).start(priority=c)
            return carry

        lax.fori_loop(0, tm // ISSUE_UNROLL, body, 0)

    @pl.when(i == 0)
    def _first():
        issue(0, 0)

    @pl.when(i + 1 < n_tiles)
    def _next():
        issue(i + 1, (i + 1) % 2)

    slot = i % 2
    for c in range(TOP_K):
        pltpu.make_async_copy(yp_ref.at[pl.ds(0, tm * ROW_CHUNKS), :], buf.at[slot, c], sems.at[slot]).wait()

    gates = mf_ref[...]
    ffn = (gates[:, 0:1] * _unpack_halves(_load_row_chunks(buf, (slot, 0), tm))
           + gates[:, 1:2] * _unpack_halves(_load_row_chunks(buf, (slot, 1), tm)))
    o_ref[...] = _layer_norm(DN_ALPHA * x1_ref[...] + ffn, g_ref[...], b_ref[...])


def _combine(dest_flat, x1, mf, ln_g, ln_b, yp, *, tm):
    t, d = x1.shape
    n_tiles = t // tm
    assert tm % ISSUE_UNROLL == 0
    vec = pl.BlockSpec((1, d), lambda i, dest: (0, 0))
    return pl.pallas_call(
        functools.partial(_combine_kernel, tm=tm, n_tiles=n_tiles),
        grid_spec=pltpu.PrefetchScalarGridSpec(
            num_scalar_prefetch=1,
            grid=(n_tiles,),
            in_specs=[pl.BlockSpec((tm, d), lambda i, dest: (i, 0)),
                      pl.BlockSpec((tm, LANES), lambda i, dest: (i, 0)),
                      vec, vec,
                      pl.BlockSpec(memory_space=pl.ANY)],
            out_specs=pl.BlockSpec((tm, d), lambda i, dest: (i, 0)),
            scratch_shapes=[pltpu.VMEM((2, TOP_K, tm * ROW_CHUNKS, LANES), U32),
                            pltpu.SemaphoreType.DMA((2,))]),
        out_shape=jax.ShapeDtypeStruct((t, d), F32),
        compiler_params=_cparams(1),
        name="combine",
    )(dest_flat, x1, mf, ln_g.reshape(1, d), ln_b.reshape(1, d), yp)


def _tile(n, pref):
    return pref if n % pref == 0 else n


def kernel(x, w_in, b_in, diff_lambda, head_norm_g, w_o_attn, rel_bias, conv_w, conv_b, conv_ln_g,
           conv_ln_b, w_conv_out, b_conv_out, w_out, ln1_g, ln1_b, router_g_w, router_g_b,
           router_e_w, router_e_b, expert_w_gate, expert_w_up, expert_w_down, ln2_g, ln2_b):
    bsz, s, d = x.shape
    t = bsz * s
    width = N_HEADS * 2 * HEAD_DIM
    assert w_in.shape[0] == DEPTH and w_in.shape[2] == 7 * width and width == d
    assert conv_w.shape[2] == width and TOP_K == 2
    tm_moe = _tile(t, 512)

    for li in range(DEPTH):
        lam_init = 0.8 - 0.6 * math.exp(-0.3 * li)
        x2 = x.reshape(t, d)
        q, k, v, ga, gb, hc = _proj_conv(
            x2, w_in[li].astype(BF16), b_in[li].reshape(1, -1), conv_w[li], conv_b[li], conv_ln_g[li],
            conv_ln_b[li], width=width, tm=_tile(s, 512), chunk=_tile(width, 512),
            q_scale=HEAD_DIM ** -0.5 * LOG2E, ts=_tile(s, 256), rc=16, seq=s)
        o_n = _attention(q.reshape(bsz, s, width), k.reshape(bsz, s, width), v.reshape(bsz, s, width),
                         rel_bias, diff_lambda[li], head_norm_g[li], tk=256 if s % 512 == 0 else 128,
                         lam_init=lam_init)

        n_r = N_GROUPS + N_EXPERTS
        w_r = jnp.pad(jnp.concatenate([router_g_w[li], router_e_w[li]], axis=1),
                      ((0, 0), (0, LANES - n_r))).astype(BF16)
        b_r = jnp.pad(jnp.concatenate([router_g_b[li], router_e_b[li]]), (0, LANES - n_r)).reshape(1, LANES)
        x1, x1p, mi, mf, cnt = _mix(
            o_n.reshape(t, width), hc.reshape(t, width), ga, gb, x2,
            w_o_attn[li].astype(BF16), w_conv_out[li].astype(BF16), b_conv_out[li],
            w_out[li].astype(BF16), ln1_g[li], ln1_b[li], w_r, b_r, tm=_tile(t, 512))

        counts = cnt[0, :N_EXPERTS].astype(I32)
        padded = (counts + tm_moe - 1) // tm_moe * tm_moe
        pad_ends = jnp.cumsum(padded)
        pad_starts = pad_ends - padded
        chosen = mi[:, 0:TOP_K, None] == jnp.arange(N_EXPERTS, dtype=I32)[None, None, :]
        dest_flat = (jnp.sum(jnp.where(chosen, pad_starts[None, None, :], 0), axis=-1)
                     + mi[:, TOP_K:2 * TOP_K]).reshape(t * TOP_K)
        n_rows = t * TOP_K + N_EXPERTS * tm_moe
        n_blocks = n_rows // tm_moe
        blk_start = jnp.arange(n_blocks, dtype=I32) * tm_moe
        blk_e = jnp.minimum(jnp.sum(blk_start[:, None] >= pad_ends[None, :], axis=1),
                            N_EXPERTS - 1).astype(I32)
        n_used = (pad_ends[-1:] // tm_moe).astype(I32)

        xs = _dispatch(dest_flat, x1p, n_rows, tm=_tile(t, 512))
        yp = _experts(blk_e, n_used, xs, expert_w_gate[li], expert_w_up[li], expert_w_down[li], tm=tm_moe)
        x = _combine(dest_flat, x1, mf, ln2_g[li], ln2_b[li], yp, tm=_tile(t, 256)).reshape(bsz, s, d)
    return x
```

```python
import functools
import math

import jax
import jax.numpy as jnp
from jax import lax
from jax.experimental import pallas as pl
from jax.experimental.pallas import tpu as pltpu

F32 = jnp.float32
BF16 = jnp.bfloat16
U32 = jnp.uint32
I32 = jnp.int32

N_HEADS = 8
HEAD_DIM = 64
CONV_TAPS = 31
REL_BUCKETS = 32
REL_MAX_DIST = 128
N_GROUPS = 4
EXPERTS_PER_GROUP = 8
N_EXPERTS = N_GROUPS * EXPERTS_PER_GROUP
TOP_K = 2
DEPTH = 1
DN_ALPHA = (2.0 * DEPTH) ** 0.25
LN_EPS = 1e-5
NEG_INF = -1e30
LOG2E = 1.4426950408889634

LANES = 128
SUBLANES = 8
VMEM_LIMIT = 56 * 1024 * 1024
CONV_HALO = 32
CONV_SHIFT_ROWS = 56
HEADS_PER_STEP = 4
ONES_ROWS = 16
ROW_CHUNKS = 4
ISSUE_UNROLL = 8


def _cparams(n_axes, flags=None):
    return pltpu.CompilerParams(dimension_semantics=("arbitrary",) * n_axes,
                                vmem_limit_bytes=VMEM_LIMIT, flags=flags)


def _sigmoid(x):
    return 1.0 / (1.0 + jnp.exp(-x))


def _layer_norm(z, g, b):
    mu = jnp.mean(z, axis=-1, keepdims=True)
    zc = z - mu
    var = jnp.mean(zc * zc, axis=-1, keepdims=True)
    return zc * lax.rsqrt(var + LN_EPS) * g + b


def _pack_halves(y):
    n = y.shape[1] // 2
    bits = pltpu.bitcast(y.astype(BF16).astype(F32), U32)
    return (bits[:, :n] >> 16) | (bits[:, n:] & jnp.uint32(0xFFFF0000))


def _unpack_halves(w):
    lo = pltpu.bitcast(w << 16, F32)
    hi = pltpu.bitcast(w & jnp.uint32(0xFFFF0000), F32)
    return jnp.concatenate([lo, hi], axis=1)


def _store_row_chunks(ref, prefix, packed, m):
    for q in range(ROW_CHUNKS):
        ref[prefix + (pl.ds(q, m, stride=ROW_CHUNKS), slice(None))] = packed[:, q * LANES:(q + 1) * LANES]


def _load_row_chunks(ref, prefix, m):
    return jnp.concatenate([ref[prefix + (pl.ds(q, m, stride=ROW_CHUNKS), slice(None))]
                            for q in range(ROW_CHUNKS)], axis=1)


def _zero_after(value):
    bits = pltpu.bitcast(value[0:SUBLANES, 0:LANES], U32)
    return pltpu.bitcast((bits >> 16) >> 16, F32)[0:1, :]


def _conv_rows(sh, w_ref, b_ref, g_ref, bb_ref, o_ref, out_row0, r0, rc, anchor):
    off = CONV_HALO - (CONV_TAPS - 1)
    acc = None
    for t in range(CONV_TAPS):
        k, base = (off + t) % SUBLANES, (off + t) // SUBLANES * SUBLANES
        slab = sh[k, r0 + base:r0 + base + rc, :].reshape(rc // SUBLANES, SUBLANES, -1)
        term = slab * w_ref[t][None]
        acc = term if acc is None else acc + term
    bias = b_ref[...]
    if anchor is not None:
        bias = bias + jnp.tile(anchor, (1, bias.shape[1] // LANES))
    y = _layer_norm(acc.reshape(rc, -1) + bias, g_ref[...], bb_ref[...])
    o_ref[out_row0 + r0:out_row0 + r0 + rc, :] = (y * _sigmoid(y)).astype(o_ref.dtype)


def _proj_conv_kernel(x_ref, w_ref, b_ref, cw_ref, cb_ref, cg_ref, cbb_ref,
                      q_ref, k_ref, v_ref, ga_ref, gb_ref, hc_ref, sh, halo_scr,
                      *, width, chunk, q_scale, tm, ts, rc, tiles_per_seq):
    i = pl.program_id(0)
    xb = x_ref[...].astype(BF16)

    def lin(col0):
        return (jnp.dot(xb, w_ref[:, col0:col0 + chunk], preferred_element_type=F32)
                + b_ref[:, col0:col0 + chunk])

    glu = jnp.concatenate([(lin(3 * width + c) * _sigmoid(lin(4 * width + c))).astype(BF16).astype(F32)
                           for c in range(0, width, chunk)], axis=1)

    n_ext = CONV_HALO + ts
    n_sh = n_ext - SUBLANES

    def load_half(half):
        if half == 0:
            prev = halo_scr[...]
            sh[0, 0:CONV_HALO, :] = jnp.where(i % tiles_per_seq != 0, prev, jnp.zeros_like(prev))
        else:
            sh[0, 0:CONV_HALO, :] = glu[half * ts - CONV_HALO:half * ts]
        sh[0, CONV_HALO:n_ext, :] = glu[half * ts:(half + 1) * ts]
        for k in range(1, SUBLANES):
            for r0 in range(0, n_sh, CONV_SHIFT_ROWS):
                n = min(CONV_SHIFT_ROWS, n_sh - r0)
                sh[k, r0:r0 + n, :] = sh[0, r0 + k:r0 + k + n, :]

    def epilogue(ref, group, c, fn):
        def run():
            z = lin(group * width + c)
            ref[:, c:c + chunk] = fn(z).astype(BF16)
            return _zero_after(z)
        return run

    others = []
    for c in range(0, width, chunk):
        others += [epilogue(q_ref, 0, c, lambda z: z * q_scale), epilogue(k_ref, 1, c, lambda z: z),
                   epilogue(v_ref, 2, c, lambda z: z), epilogue(ga_ref, 5, c, _sigmoid),
                   epilogue(gb_ref, 6, c, _sigmoid)]
    conv_chunks = [(half, r0) for half in range(tm // ts) for r0 in range(0, ts, rc)]
    every = max(1, len(conv_chunks) // len(others))
    anchor = None
    for idx, (half, r0) in enumerate(conv_chunks):
        if r0 == 0:
            load_half(half)
        _conv_rows(sh, cw_ref, cb_ref, cg_ref, cbb_ref, hc_ref, half * ts, r0, rc, anchor)
        if idx % every == every - 1 and others:
            anchor = others.pop(0)()
    for run in others:
        run()
    halo_scr[...] = glu[tm - CONV_HALO:tm]


def _proj_conv(x2, w_in, b_in, conv_w, conv_b, ln_g, ln_b, *, width, tm, chunk, q_scale, ts, rc, seq):
    t, d = x2.shape
    n_cols = w_in.shape[1]
    c = width
    assert seq % tm == 0 and tm % ts == 0 and ts % rc == 0 and CONV_HALO >= CONV_TAPS - 1 and ts >= CONV_HALO
    out = jax.ShapeDtypeStruct((t, width), BF16)
    row_spec = pl.BlockSpec((tm, width), lambda i: (i, 0))
    const = lambda shape: pl.BlockSpec(shape, lambda i: (0,) * len(shape), pipeline_mode=pl.Buffered(1))
    kernel = functools.partial(_proj_conv_kernel, width=width, chunk=chunk, q_scale=q_scale, tm=tm, ts=ts,
                               rc=rc, tiles_per_seq=seq // tm)
    return pl.pallas_call(
        kernel,
        grid=(t // tm,),
        in_specs=[pl.BlockSpec((tm, d), lambda i: (i, 0)),
                  const((d, n_cols)), const((1, n_cols)),
                  const((CONV_TAPS, SUBLANES, c)), const((1, c)), const((1, c)), const((1, c))],
        out_specs=[row_spec] * 6,
        out_shape=[out] * 6,
        scratch_shapes=[pltpu.VMEM((SUBLANES, CONV_HALO + ts, c), F32), pltpu.VMEM((CONV_HALO, c), F32)],
        compiler_params=_cparams(1),
        name="proj_conv",
    )(x2, w_in, b_in, jnp.broadcast_to(conv_w[:, None, :], (CONV_TAPS, SUBLANES, c)),
      conv_b.reshape(1, c), ln_g.reshape(1, c), ln_b.reshape(1, c))


def _sublane_all(x, op):
    for shift in (4, 2, 1):
        x = op(x, pltpu.roll(x, shift, 0))
    return x


def _attn_kernel(tab_ref, bucket_ref, lam_ref, q_ref, k_ref, v_ref, g_ref, o_ref,
                 bias_scr, qz_scr, vt_scr, s0_scr, s1_scr, p0_scr, p1_scr, a0_scr, a1_scr,
                 m_scr, acc_scr, *, tq, tk, dh, lam_init):
    hp = pl.program_id(0)
    b = pl.program_id(1)
    qi = pl.program_id(2)
    hw = 2 * dh
    acc_rows = hw + ONES_ROWS
    n_kv = vt_scr.shape[1]
    sw = tk
    n_sub = 2 * tq // sw
    heads = range(HEADS_PER_STEP)

    @pl.when((b == 0) & (qi == 0))
    def _build_bias():
        for hh in heads:
            h = hp * HEADS_PER_STEP + hh
            far = tab_ref[REL_BUCKETS - 1, h]
            for t in range(3):
                for c in range(tq // sw):
                    bk = bucket_ref[t, :, c * sw:(c + 1) * sw]
                    acc = jnp.full((tk, sw), NEG_INF, F32)
                    for r in range(REL_BUCKETS):
                        acc = jnp.where(bk == r, (tab_ref[r, h] - far) * LOG2E, acc)
                    bias_scr[hh, t, :, c * sw:(c + 1) * sw] = acc

    @pl.when(qi == 0)
    def _transpose_values():
        for hh in heads:
            for jj in range(n_kv):
                vt_scr[hh, jj, 0:hw, :] = (v_ref[0, jj * tk:(jj + 1) * tk, hh * hw:(hh + 1) * hw]
                                           .astype(F32).T.astype(BF16))
                vt_scr[hh, jj, hw:hw + ONES_ROWS, :] = jnp.ones((ONES_ROWS, tk), BF16)

    lane = lax.broadcasted_iota(I32, (tq, hw), 1)
    for hh in heads:
        q = q_ref[0, :, hh * hw:(hh + 1) * hw]
        zero = jnp.zeros_like(q)
        qz_scr[hh, 0:tq, :] = jnp.where(lane < dh, q, zero)
        qz_scr[hh, tq:2 * tq, :] = jnp.where(lane >= dh, q, zero)
    m_scr[...] = jnp.full(m_scr.shape, NEG_INF, F32)
    acc_scr[...] = jnp.zeros(acc_scr.shape, F32)

    s_bufs, p_bufs, a_bufs = (s0_scr, s1_scr), (p0_scr, p1_scr), (a0_scr, a1_scr)
    p1_scr[...] = jnp.zeros(p1_scr.shape, BF16)
    a1_scr[...] = jnp.ones(a1_scr.shape, F32)

    all_cols = (slice(0, 2 * tq),)
    late_cols = (slice(tk, tq), slice(tq + tk, 2 * tq))

    def issue_scores(j, slot, cols=all_cols):
        rows = (slice(j * tk, (j + 1) * tk) if isinstance(j, int)
                else pl.ds(pl.multiple_of(j * tk, tk), tk))
        for hh in heads:
            kj = k_ref[0, rows, hh * hw:(hh + 1) * hw]
            for cs in cols:
                s_bufs[slot][hh, :, cs] = lax.dot_general(kj, qz_scr[hh, cs, :], (((1,), (1,)), ((), ())),
                                                          preferred_element_type=F32)

    def accumulate(hh, j_prev, slot_prev, cs=slice(0, 2 * tq)):
        n = cs.stop - cs.start
        pv = jnp.dot(vt_scr[hh, j_prev], p_bufs[slot_prev][hh, :, cs], preferred_element_type=F32)
        acc3 = acc_scr[hh, :, cs].reshape(acc_rows // SUBLANES, SUBLANES, n)
        return (a_bufs[slot_prev][hh, :, cs][None] * acc3).reshape(acc_rows, n) + pv

    def step(j, bias_idx, slot, issue_next=True, next_is_last=False, last=False):
        if issue_next:
            issue_scores(j + 1, 1 - slot, late_cols if next_is_last else all_cols)
        acc_new = [accumulate(hh, jnp.maximum(j - 1, 0), 1 - slot) for hh in heads]
        for hh in heads:
            for c in range(n_sub):
                cs = slice(c * sw, (c + 1) * sw)
                qs = (c * sw) % tq
                if last and qs < tk:
                    continue
                s = s_bufs[slot][hh, :, cs]
                if bias_idx is not None:
                    s = s + bias_scr[hh, bias_idx, :, qs:qs + sw]
                s3 = s.reshape(tk // SUBLANES, SUBLANES, sw)
                m_prev = m_scr[hh, :, cs]
                m_new = jnp.maximum(m_prev, _sublane_all(jnp.max(s3, axis=0), jnp.maximum))
                p3 = jnp.exp2(s3 - m_new[None])
                p_bufs[slot][hh, :, cs] = p3.reshape(tk, sw).astype(BF16)
                a_bufs[slot][hh, :, cs] = jnp.exp2(m_prev - m_new)
                m_scr[hh, :, cs] = m_new
        for hh in heads:
            acc_scr[hh] = acc_new[hh]

    issue_scores(0, 0)

    @pl.when(qi == 0)
    def _first_query_tile():
        step(0, 1, 0, next_is_last=True)
        step(1, 2, 1, issue_next=False, last=True)

    @pl.when(qi >= 1)
    def _later_query_tiles():
        def far_pair(i, carry):
            step(2 * i, None, 0)
            step(2 * i + 1, None, 1)
            return carry

        lax.fori_loop(0, qi - 1, far_pair, 0)
        step(2 * qi - 2, None, 0)
        step(2 * qi - 1, 0, 1)
        step(2 * qi, 1, 0, next_is_last=True)
        step(2 * qi + 1, 2, 1, issue_next=False, last=True)

    drained = [[accumulate(hh, 2 * qi + 1, 1, cs) for cs in late_cols] for hh in heads]
    for hh in heads:
        for cs, val in zip(late_cols, drained[hh]):
            acc_scr[hh, :, cs] = val

    lv = lam_ref[...]
    lam = (jnp.exp(jnp.sum(lv[0:1] * lv[1:2], axis=-1, keepdims=True))
           - jnp.exp(jnp.sum(lv[2:3] * lv[3:4], axis=-1, keepdims=True)) + lam_init)
    gain = g_ref[...] * (1.0 - lam_init)
    for hh in heads:
        acc_all = acc_scr[hh].reshape(acc_rows // SUBLANES, SUBLANES, 2 * tq)
        inv_l = 1.0 / acc_all[hw // SUBLANES]
        acc3 = acc_all[0:hw // SUBLANES]
        o3 = acc3[:, :, 0:tq] * inv_l[None, :, 0:tq] - lam * (acc3[:, :, tq:2 * tq] * inv_l[None, :, tq:2 * tq])
        ms = _sublane_all(jnp.sum(o3 * o3, axis=0), jnp.add) * (1.0 / hw)
        y = (o3 * lax.rsqrt(ms + LN_EPS)[None]).reshape(hw, tq) * gain
        o_ref[0, :, hh * hw:(hh + 1) * hw] = y.T.astype(o_ref.dtype)


def _rel_bucket(dist):
    max_exact = REL_BUCKETS // 2
    d = jnp.maximum(dist, 1).astype(F32)
    large = max_exact + (jnp.log(d / max_exact) / math.log(REL_MAX_DIST / max_exact)
                         * (REL_BUCKETS - max_exact)).astype(I32)
    large = jnp.minimum(large, REL_BUCKETS - 1)
    return jnp.where(dist < max_exact, dist, large)


def _near_buckets(tq, tk):
    tiles = []
    for rel_tile in (-1, 0, 1):
        dist = (jnp.arange(tq, dtype=I32)[None, :]
                - (rel_tile * tk + jnp.arange(tk, dtype=I32))[:, None])
        tiles.append(jnp.where(dist >= 0, _rel_bucket(jnp.maximum(dist, 0)), -1))
    return jnp.stack(tiles).astype(I32)


def _attention(q, k, v, rel_bias, diff_lambda, head_norm_g, *, tk, lam_init):
    bsz, s, width = q.shape
    hw = width // N_HEADS
    tq = 2 * tk
    hps, pw = HEADS_PER_STEP, HEADS_PER_STEP * hw
    assert hw == 2 * HEAD_DIM and hw % LANES == 0 and s % tq == 0 and N_HEADS % hps == 0
    assert tk + 1 >= REL_MAX_DIST
    kernel = functools.partial(_attn_kernel, tq=tq, tk=tk, dh=HEAD_DIM, lam_init=lam_init)
    stat = pltpu.VMEM((hps, SUBLANES, 2 * tq), F32)
    return pl.pallas_call(
        kernel,
        grid=(N_HEADS // hps, bsz, s // tq),
        in_specs=[pl.BlockSpec(memory_space=pltpu.SMEM),
                  pl.BlockSpec((3, tk, tq), lambda h, b, i: (0, 0, 0)),
                  pl.BlockSpec((4, HEAD_DIM), lambda h, b, i: (0, 0)),
                  pl.BlockSpec((1, tq, pw), lambda h, b, i: (b, i, h)),
                  pl.BlockSpec((1, s, pw), lambda h, b, i: (b, 0, h)),
                  pl.BlockSpec((1, s, pw), lambda h, b, i: (b, 0, h)),
                  pl.BlockSpec((hw, 1), lambda h, b, i: (0, 0))],
        out_specs=pl.BlockSpec((1, tq, pw), lambda h, b, i: (b, i, h)),
        out_shape=jax.ShapeDtypeStruct((bsz, s, width), BF16),
        scratch_shapes=[pltpu.VMEM((hps, 3, tk, tq), F32),
                        pltpu.VMEM((hps, 2 * tq, hw), BF16),
                        pltpu.VMEM((hps, s // tk, hw + ONES_ROWS, tk), BF16),
                        pltpu.VMEM((hps, tk, 2 * tq), F32),
                        pltpu.VMEM((hps, tk, 2 * tq), F32),
                        pltpu.VMEM((hps, tk, 2 * tq), BF16),
                        pltpu.VMEM((hps, tk, 2 * tq), BF16),
                        stat, stat,
                        stat,
                        pltpu.VMEM((hps, hw + ONES_ROWS, 2 * tq), F32)],
        compiler_params=_cparams(3),
        name="attn",
    )(rel_bias, _near_buckets(tq, tk), diff_lambda, q, k, v, head_norm_g.reshape(hw, 1))


def _mix_kernel(o_ref, hc_ref, ga_ref, gb_ref, x_ref, woa_ref, wco_ref, bco_ref, wout_ref,
                g1_ref, b1_ref, wr_ref, br_ref, tri_ref,
                x1_ref, x1p_ref, mi_ref, mf_ref, cnt_ref, carry_scr, *, tm):
    i = pl.program_id(0)

    @pl.when(i == 0)
    def _init():
        carry_scr[...] = jnp.zeros(carry_scr.shape, F32)

    halves = [slice(0, tm // 2), slice(tm // 2, tm)]
    branch = [(jnp.dot(o_ref[h, :], woa_ref[...], preferred_element_type=F32),
               jnp.dot(hc_ref[h, :], wco_ref[...], preferred_element_type=F32) + bco_ref[...]) for h in halves]
    mixed = [jnp.dot((ga_ref[h, :].astype(F32) * y_a + gb_ref[h, :].astype(F32) * y_b).astype(BF16),
                     wout_ref[...], preferred_element_type=F32) for h, (y_a, y_b) in zip(halves, branch)]
    x1_halves = [_layer_norm(DN_ALPHA * x_ref[h, :] + mx, g1_ref[...], b1_ref[...]) for h, mx in zip(halves, mixed)]
    logits = jnp.concatenate([jnp.dot(xh.astype(BF16), wr_ref[...], preferred_element_type=F32)
                              for xh in x1_halves], axis=0) + br_ref[...]
    x1 = jnp.concatenate(x1_halves, axis=0)
    x1_ref[...] = x1
    _store_row_chunks(x1p_ref, (), _pack_halves(x1), tm)

    lane = lax.broadcasted_iota(I32, (tm, LANES), 1)
    g_lane = lane < N_GROUPS
    gl = jnp.where(g_lane, logits, NEG_INF)
    g_max = jnp.max(gl, axis=-1, keepdims=True)
    g_idx = jnp.min(jnp.where(gl == g_max, lane, LANES), axis=-1, keepdims=True)
    p_g = 1.0 / jnp.sum(jnp.where(g_lane, jnp.exp(gl - g_max), 0.0), axis=-1, keepdims=True)
    lo = N_GROUPS + g_idx * EXPERTS_PER_GROUP
    el = jnp.where(lane >= lo, jnp.where(lane < lo + EXPERTS_PER_GROUP, logits, NEG_INF), NEG_INF)
    v1 = jnp.max(el, axis=-1, keepdims=True)
    i1 = jnp.min(jnp.where(el == v1, lane, LANES), axis=-1, keepdims=True)
    el2 = jnp.where(lane == i1, NEG_INF, el)
    v2 = jnp.max(el2, axis=-1, keepdims=True)
    i2 = jnp.min(jnp.where(lane == i1, LANES, jnp.where(el2 == v2, lane, LANES)), axis=-1, keepdims=True)
    ex = jnp.exp(v2 - v1)
    w1 = 1.0 / (1.0 + ex)
    gate1 = p_g * w1
    gate2 = p_g * (ex * w1)
    e1 = i1 - N_GROUPS
    e2 = i2 - N_GROUPS

    carry = carry_scr[...]
    tri = tri_ref[...]
    ranks = []
    for e in (e1, e2):
        hit = lane == e
        oh = jnp.where(hit, 1.0, 0.0)
        before = jnp.dot(tri, oh.astype(BF16), preferred_element_type=F32) + carry
        ranks.append(jnp.sum(jnp.where(hit, before, 0.0), axis=-1, keepdims=True).astype(I32))
        carry = carry + jnp.sum(oh, axis=0, keepdims=True)
    carry_scr[...] = carry
    cnt_ref[...] = carry

    mi_ref[...] = jnp.where(lane == 0, e1, jnp.where(lane == 1, e2,
                            jnp.where(lane == 2, ranks[0], jnp.where(lane == 3, ranks[1], 0))))
    mf_ref[...] = jnp.where(lane == 0, gate1, jnp.where(lane == 1, gate2, 0.0))


def _mix(o_n, hc, ga, gb, x2, w_oa, w_co, b_co, w_out, ln_g, ln_b, w_r, b_r, *, tm):
    t, d = x2.shape
    assert d // 2 == ROW_CHUNKS * LANES
    row = lambda w: pl.BlockSpec((tm, w), lambda i: (i, 0))
    full = lambda a: pl.BlockSpec(a.shape, lambda i: (0,) * a.ndim, pipeline_mode=pl.Buffered(1))
    tri = (jnp.arange(tm)[:, None] > jnp.arange(tm)[None, :]).astype(BF16)
    b_co, ln_g, ln_b = b_co.reshape(1, d), ln_g.reshape(1, d), ln_b.reshape(1, d)
    return pl.pallas_call(
        functools.partial(_mix_kernel, tm=tm),
        grid=(t // tm,),
        in_specs=[row(d), row(d), row(d), row(d), row(d),
                  full(w_oa), full(w_co), full(b_co), full(w_out), full(ln_g), full(ln_b),
                  full(w_r), full(b_r), full(tri)],
        out_specs=[row(d), pl.BlockSpec((tm * ROW_CHUNKS, LANES), lambda i: (i, 0)), row(LANES), row(LANES),
                   pl.BlockSpec((1, LANES), lambda i: (0, 0))],
        out_shape=[jax.ShapeDtypeStruct((t, d), F32),
                   jax.ShapeDtypeStruct((t * ROW_CHUNKS, LANES), U32),
                   jax.ShapeDtypeStruct((t, LANES), I32),
                   jax.ShapeDtypeStruct((t, LANES), F32),
                   jax.ShapeDtypeStruct((1, LANES), F32)],
        scratch_shapes=[pltpu.VMEM((1, LANES), F32)],
        compiler_params=_cparams(1),
        name="mix",
    )(o_n, hc, ga, gb, x2, w_oa, w_co, b_co, w_out, ln_g, ln_b, w_r, b_r, tri)


def _row_copy(src, src_row, dst, dst_row, sem):
    def piece(row):
        return pl.ds(pl.multiple_of(row * ROW_CHUNKS, ROW_CHUNKS), ROW_CHUNKS)
    return pltpu.make_async_copy(src.at[piece(src_row), :], dst.at[piece(dst_row), :], sem)


def _dispatch_kernel(dest_ref, x_ref, xs_in_ref, xs_ref, sem, *, tm):
    del xs_in_ref
    base = pl.program_id(0) * tm

    def issue(g, carry):
        for u in range(ISSUE_UNROLL):
            r = g * ISSUE_UNROLL + u
            for c in range(TOP_K):
                _row_copy(x_ref, r, xs_ref, dest_ref[TOP_K * (base + r) + c], sem).start(priority=c)
        return carry

    lax.fori_loop(0, tm // ISSUE_UNROLL, issue, 0)
    for _ in range(TOP_K):
        pltpu.make_async_copy(x_ref, xs_ref.at[pl.ds(0, tm * ROW_CHUNKS), :], sem).wait()


def _dispatch(dest_flat, x1p, n_rows, *, tm):
    w = x1p.shape[1]
    t = x1p.shape[0] // ROW_CHUNKS
    assert tm % ISSUE_UNROLL == 0
    return pl.pallas_call(
        functools.partial(_dispatch_kernel, tm=tm),
        grid_spec=pltpu.PrefetchScalarGridSpec(
            num_scalar_prefetch=1,
            grid=(t // tm,),
            in_specs=[pl.BlockSpec((tm * ROW_CHUNKS, w), lambda i, dest: (i, 0)),
                      pl.BlockSpec(memory_space=pl.ANY)],
            out_specs=pl.BlockSpec(memory_space=pl.ANY),
            scratch_shapes=[pltpu.SemaphoreType.DMA(())]),
        out_shape=jax.ShapeDtypeStruct((n_rows * ROW_CHUNKS, w), U32),
        input_output_aliases={2: 0},
        compiler_params=_cparams(1),
        name="dispatch",
    )(dest_flat, x1p, jnp.zeros((n_rows * ROW_CHUNKS, w), U32))


def _expert_kernel(blk_e_ref, n_used_ref, next_e_ref, run_ref, xs_ref, wg_hbm, wu_hbm, wd_hbm, y_ref,
                   wg_f, wu_f, wd_f, wg_b, wu_b, wd_b, sems):
    i = pl.program_id(0)
    active = i < n_used_ref[0]
    tm = xs_ref.shape[0] // ROW_CHUNKS
    e = blk_e_ref[i]

    def weight_copies(expert, slot):
        return [pltpu.make_async_copy(hbm.at[expert], buf.at[slot], sems.at[slot, n])
                for n, (hbm, buf) in enumerate(((wg_hbm, wg_f), (wu_hbm, wu_f), (wd_hbm, wd_f)))]

    @pl.when(jnp.logical_not(active))
    def _unused_block():
        y_ref[...] = jnp.zeros(y_ref.shape, y_ref.dtype)

    @pl.when(i == 0)
    def _first_fetch():
        for cp in weight_copies(e, 0):
            cp.start()

    @pl.when(active & ((i == 0) | (e != blk_e_ref[jnp.maximum(i - 1, 0)])))
    def _new_expert():
        slot = run_ref[e] % 2
        for cp in weight_copies(e, slot):
            cp.wait()
        wg_b[...] = wg_f[slot].astype(BF16)
        wu_b[...] = wu_f[slot].astype(BF16)
        wd_b[...] = wd_f[slot].astype(BF16)

        @pl.when(next_e_ref[e] >= 0)
        def _prefetch_next():
            for cp in weight_copies(next_e_ref[e], 1 - slot):
                cp.start()

    @pl.when(active)
    def _():
        xb = _unpack_halves(_load_row_chunks(xs_ref, (), tm)).astype(BF16)
        g = jnp.dot(xb, wg_b[...], preferred_element_type=F32)
        u = jnp.dot(xb, wu_b[...], preferred_element_type=F32)
        hid = (g * _sigmoid(g) * u).astype(BF16)
        y = jnp.dot(hid, wd_b[...], preferred_element_type=F32)
        _store_row_chunks(y_ref, (), _pack_halves(y), tm)


def _experts(blk_e, n_used, next_e, run_idx, xs, w_gate, w_up, w_down, *, tm):
    w = xs.shape[1]
    n_rows = xs.shape[0] // ROW_CHUNKS
    _, d, de = w_gate.shape
    row_map = lambda i, blk_e, n_used, next_e, run_idx: (jnp.minimum(i, n_used[0] - 1), 0)
    hbm = pl.BlockSpec(memory_space=pl.ANY)
    return pl.pallas_call(
        _expert_kernel,
        grid_spec=pltpu.PrefetchScalarGridSpec(
            num_scalar_prefetch=4,
            grid=(n_rows // tm,),
            in_specs=[pl.BlockSpec((tm * ROW_CHUNKS, w), row_map), hbm, hbm, hbm],
            out_specs=pl.BlockSpec((tm * ROW_CHUNKS, w), lambda i, blk_e, n_used, next_e, run_idx: (i, 0)),
            scratch_shapes=[pltpu.VMEM((2, d, de), F32), pltpu.VMEM((2, d, de), F32), pltpu.VMEM((2, de, d), F32),
                            pltpu.VMEM((d, de), BF16), pltpu.VMEM((d, de), BF16), pltpu.VMEM((de, d), BF16),
                            pltpu.SemaphoreType.DMA((2, 3))]),
        out_shape=jax.ShapeDtypeStruct((n_rows * ROW_CHUNKS, w), U32),
        compiler_params=_cparams(1),
        name="experts",
    )(blk_e, n_used, next_e, run_idx, xs, w_gate, w_up, w_down)


def _combine_kernel(dest_ref, x1_ref, mf_ref, g_ref, b_ref, yp_ref, o_ref, buf, sems, *, tm, n_tiles):
    i = pl.program_id(0)

    def issue(tile, slot):
        base = tile * tm

        def body(g, carry):
            for u in range(ISSUE_UNROLL):
                r = g * ISSUE_UNROLL + u
                for c in range(TOP_K):
                    _row_copy(yp_ref, dest_ref[TOP_K * (base + r) + c], buf.at[slot, c], r,
                              sems.at[slot]).start(priority=c)
            return carry

        lax.fori_loop(0, tm // ISSUE_UNROLL, body, 0)

    @pl.when(i == 0)
    def _first():
        issue(0, 0)

    @pl.when(i + 1 < n_tiles)
    def _next():
        issue(i + 1, (i + 1) % 2)

    slot = i % 2
    for c in range(TOP_K):
        pltpu.make_async_copy(yp_ref.at[pl.ds(0, tm * ROW_CHUNKS), :], buf.at[slot, c], sems.at[slot]).wait()

    gates = mf_ref[...]
    ffn = (gates[:, 0:1] * _unpack_halves(_load_row_chunks(buf, (slot, 0), tm))
           + gates[:, 1:2] * _unpack_halves(_load_row_chunks(buf, (slot, 1), tm)))
    o_ref[...] = _layer_norm(DN_ALPHA * x1_ref[...] + ffn, g_ref[...], b_ref[...])


def _combine(dest_flat, x1, mf, ln_g, ln_b, yp, *, tm):
    t, d = x1.shape
    n_tiles = t // tm
    assert tm % ISSUE_UNROLL == 0
    vec = pl.BlockSpec((1, d), lambda i, dest: (0, 0))
    return pl.pallas_call(
        functools.partial(_combine_kernel, tm=tm, n_tiles=n_tiles),
        grid_spec=pltpu.PrefetchScalarGridSpec(
            num_scalar_prefetch=1,
            grid=(n_tiles,),
            in_specs=[pl.BlockSpec((tm, d), lambda i, dest: (i, 0)),
                      pl.BlockSpec((tm, LANES), lambda i, dest: (i, 0)),
                      vec, vec,
                      pl.BlockSpec(memory_space=pl.ANY)],
            out_specs=pl.BlockSpec((tm, d), lambda i, dest: (i, 0)),
            scratch_shapes=[pltpu.VMEM((2, TOP_K, tm * ROW_CHUNKS, LANES), U32),
                            pltpu.SemaphoreType.DMA((2,))]),
        out_shape=jax.ShapeDtypeStruct((t, d), F32),
        compiler_params=_cparams(1),
        name="combine",
    )(dest_flat, x1, mf, ln_g.reshape(1, d), ln_b.reshape(1, d), yp)


def _tile(n, pref):
    return pref if n % pref == 0 else n


def kernel(x, w_in, b_in, diff_lambda, head_norm_g, w_o_attn, rel_bias, conv_w, conv_b, conv_ln_g,
           conv_ln_b, w_conv_out, b_conv_out, w_out, ln1_g, ln1_b, router_g_w, router_g_b,
           router_e_w, router_e_b, expert_w_gate, expert_w_up, expert_w_down, ln2_g, ln2_b):
    bsz, s, d = x.shape
    t = bsz * s
    width = N_HEADS * 2 * HEAD_DIM
    assert w_in.shape[0] == DEPTH and w_in.shape[2] == 7 * width and width == d
    assert conv_w.shape[2] == width and TOP_K == 2
    tm_moe = _tile(t, 512)

    for li in range(DEPTH):
        lam_init = 0.8 - 0.6 * math.exp(-0.3 * li)
        x2 = x.reshape(t, d)
        q, k, v, ga, gb, hc = _proj_conv(
            x2, w_in[li].astype(BF16), b_in[li].reshape(1, -1), conv_w[li], conv_b[li], conv_ln_g[li],
            conv_ln_b[li], width=width, tm=_tile(s, 512), chunk=_tile(width, 512),
            q_scale=HEAD_DIM ** -0.5 * LOG2E, ts=_tile(s, 256), rc=16, seq=s)
        o_n = _attention(q.reshape(bsz, s, width), k.reshape(bsz, s, width), v.reshape(bsz, s, width),
                         rel_bias, diff_lambda[li], head_norm_g[li], tk=256 if s % 512 == 0 else 128,
                         lam_init=lam_init)

        n_r = N_GROUPS + N_EXPERTS
        w_r = jnp.pad(jnp.concatenate([router_g_w[li], router_e_w[li]], axis=1),
                      ((0, 0), (0, LANES - n_r))).astype(BF16)
        b_r = jnp.pad(jnp.concatenate([router_g_b[li], router_e_b[li]]), (0, LANES - n_r)).reshape(1, LANES)
        x1, x1p, mi, mf, cnt = _mix(
            o_n.reshape(t, width), hc.reshape(t, width), ga, gb, x2,
            w_o_attn[li].astype(BF16), w_conv_out[li].astype(BF16), b_conv_out[li],
            w_out[li].astype(BF16), ln1_g[li], ln1_b[li], w_r, b_r, tm=_tile(t, 512))

        counts = cnt[0, :N_EXPERTS].astype(I32)
        padded = (counts + tm_moe - 1) // tm_moe * tm_moe
        pad_ends = jnp.cumsum(padded)
        pad_starts = pad_ends - padded
        chosen = mi[:, 0:TOP_K, None] == jnp.arange(N_EXPERTS, dtype=I32)[None, None, :]
        dest_flat = (jnp.sum(jnp.where(chosen, pad_starts[None, None, :], 0), axis=-1)
                     + mi[:, TOP_K:2 * TOP_K]).reshape(t * TOP_K)
        n_rows = t * TOP_K + N_EXPERTS * tm_moe
        n_blocks = n_rows // tm_moe
        blk_start = jnp.arange(n_blocks, dtype=I32) * tm_moe
        blk_e = jnp.minimum(jnp.sum(blk_start[:, None] >= pad_ends[None, :], axis=1),
                            N_EXPERTS - 1).astype(I32)
        n_used = (pad_ends[-1:] // tm_moe).astype(I32)
        has_rows = padded > 0
        later = lax.cummin(jnp.where(has_rows, jnp.arange(N_EXPERTS, dtype=I32), N_EXPERTS), reverse=True)
        next_e = jnp.concatenate([later[1:], jnp.full((1,), N_EXPERTS, I32)])
        next_e = jnp.where(next_e < N_EXPERTS, next_e, -1).astype(I32)
        run_idx = (jnp.cumsum(has_rows.astype(I32)) - 1).astype(I32)

        xs = _dispatch(dest_flat, x1p, n_rows, tm=_tile(t, 512))
        yp = _experts(blk_e, n_used, next_e, run_idx, xs, expert_w_gate[li], expert_w_up[li], expert_w_down[li],
                      tm=tm_moe)
        x = _combine(dest_flat, x1, mf, ln2_g[li], ln2_b[li], yp, tm=_tile(t, 256)).reshape(bsz, s, d)
    return x
```

```python
import functools
import math

import jax
import jax.numpy as jnp
from jax import lax
from jax.experimental import pallas as pl
from jax.experimental.pallas import tpu as pltpu

F32 = jnp.float32
BF16 = jnp.bfloat16
U32 = jnp.uint32
I32 = jnp.int32

N_HEADS = 8
HEAD_DIM = 64
CONV_TAPS = 31
REL_BUCKETS = 32
REL_MAX_DIST = 128
N_GROUPS = 4
EXPERTS_PER_GROUP = 8
N_EXPERTS = N_GROUPS * EXPERTS_PER_GROUP
TOP_K = 2
DEPTH = 1
DN_ALPHA = (2.0 * DEPTH) ** 0.25
LN_EPS = 1e-5
NEG_INF = -1e30
LOG2E = 1.4426950408889634

LANES = 128
SUBLANES = 8
VMEM_LIMIT = 56 * 1024 * 1024
CONV_HALO = 32
CONV_SHIFT_ROWS = 56
HEADS_PER_STEP = 4
ONES_ROWS = 16
ROW_CHUNKS = 4
ISSUE_UNROLL = 8


def _cparams(n_axes):
    return pltpu.CompilerParams(dimension_semantics=("arbitrary",) * n_axes, vmem_limit_bytes=VMEM_LIMIT)


def _sigmoid(x):
    return 1.0 / (1.0 + jnp.exp(-x))


def _layer_norm(z, g, b):
    mu = jnp.mean(z, axis=-1, keepdims=True)
    zc = z - mu
    var = jnp.mean(zc * zc, axis=-1, keepdims=True)
    return zc * lax.rsqrt(var + LN_EPS) * g + b


def _pack_halves(y):
    n = y.shape[1] // 2
    bits = pltpu.bitcast(y.astype(BF16).astype(F32), U32)
    return (bits[:, :n] >> 16) | (bits[:, n:] & jnp.uint32(0xFFFF0000))


def _unpack_halves(w):
    lo = pltpu.bitcast(w << 16, F32)
    hi = pltpu.bitcast(w & jnp.uint32(0xFFFF0000), F32)
    return jnp.concatenate([lo, hi], axis=1)


def _store_row_chunks(ref, prefix, packed, m):
    for q in range(ROW_CHUNKS):
        ref[prefix + (pl.ds(q, m, stride=ROW_CHUNKS), slice(None))] = packed[:, q * LANES:(q + 1) * LANES]


def _load_row_chunks(ref, prefix, m):
    return jnp.concatenate([ref[prefix + (pl.ds(q, m, stride=ROW_CHUNKS), slice(None))]
                            for q in range(ROW_CHUNKS)], axis=1)


def _zero_after(value):
    bits = pltpu.bitcast(value[0:SUBLANES, 0:LANES], U32)
    return pltpu.bitcast((bits >> 16) >> 16, F32)[0:1, :]


def _conv_rows(sh, w_ref, b_ref, g_ref, bb_ref, o_ref, out_row0, r0, rc, anchor):
    off = CONV_HALO - (CONV_TAPS - 1)
    acc = None
    for t in range(CONV_TAPS):
        k, base = (off + t) % SUBLANES, (off + t) // SUBLANES * SUBLANES
        slab = sh[k, r0 + base:r0 + base + rc, :].reshape(rc // SUBLANES, SUBLANES, -1)
        term = slab * w_ref[t][None]
        acc = term if acc is None else acc + term
    bias = b_ref[...]
    if anchor is not None:
        bias = bias + jnp.tile(anchor, (1, bias.shape[1] // LANES))
    y = _layer_norm(acc.reshape(rc, -1) + bias, g_ref[...], bb_ref[...])
    o_ref[out_row0 + r0:out_row0 + r0 + rc, :] = (y * _sigmoid(y)).astype(o_ref.dtype)


def _proj_conv_kernel(x_ref, w_ref, b_ref, cw_ref, cb_ref, cg_ref, cbb_ref,
                      q_ref, k_ref, v_ref, ga_ref, gb_ref, hc_ref, sh, halo_scr,
                      *, width, chunk, q_scale, tm, ts, rc, tiles_per_seq):
    i = pl.program_id(0)
    xb = x_ref[...].astype(BF16)

    def lin(col0):
        return (jnp.dot(xb, w_ref[:, col0:col0 + chunk], preferred_element_type=F32)
                + b_ref[:, col0:col0 + chunk])

    glu = jnp.concatenate([(lin(3 * width + c) * _sigmoid(lin(4 * width + c))).astype(BF16).astype(F32)
                           for c in range(0, width, chunk)], axis=1)

    n_ext = CONV_HALO + ts
    n_sh = n_ext - SUBLANES

    def load_half(half):
        if half == 0:
            prev = halo_scr[...]
            sh[0, 0:CONV_HALO, :] = jnp.where(i % tiles_per_seq != 0, prev, jnp.zeros_like(prev))
        else:
            sh[0, 0:CONV_HALO, :] = glu[half * ts - CONV_HALO:half * ts]
        sh[0, CONV_HALO:n_ext, :] = glu[half * ts:(half + 1) * ts]
        for k in range(1, SUBLANES):
            for r0 in range(0, n_sh, CONV_SHIFT_ROWS):
                n = min(CONV_SHIFT_ROWS, n_sh - r0)
                sh[k, r0:r0 + n, :] = sh[0, r0 + k:r0 + k + n, :]

    def epilogue(ref, group, c, fn):
        def run():
            z = lin(group * width + c)
            ref[:, c:c + chunk] = fn(z).astype(BF16)
            return _zero_after(z)
        return run

    others = []
    for c in range(0, width, chunk):
        others += [epilogue(q_ref, 0, c, lambda z: z * q_scale), epilogue(k_ref, 1, c, lambda z: z),
                   epilogue(v_ref, 2, c, lambda z: z), epilogue(ga_ref, 5, c, _sigmoid),
                   epilogue(gb_ref, 6, c, _sigmoid)]
    conv_chunks = [(half, r0) for half in range(tm // ts) for r0 in range(0, ts, rc)]
    every = max(1, len(conv_chunks) // len(others))
    anchor = None
    for idx, (half, r0) in enumerate(conv_chunks):
        if r0 == 0:
            load_half(half)
        _conv_rows(sh, cw_ref, cb_ref, cg_ref, cbb_ref, hc_ref, half * ts, r0, rc, anchor)
        if idx % every == every - 1 and others:
            anchor = others.pop(0)()
    for run in others:
        run()
    halo_scr[...] = glu[tm - CONV_HALO:tm]


def _proj_conv(x2, w_in, b_in, conv_w, conv_b, ln_g, ln_b, *, width, tm, chunk, q_scale, ts, rc, seq):
    t, d = x2.shape
    n_cols = w_in.shape[1]
    c = width
    assert seq % tm == 0 and tm % ts == 0 and ts % rc == 0 and CONV_HALO >= CONV_TAPS - 1 and ts >= CONV_HALO
    out = jax.ShapeDtypeStruct((t, width), BF16)
    row_spec = pl.BlockSpec((tm, width), lambda i: (i, 0))
    const = lambda shape: pl.BlockSpec(shape, lambda i: (0,) * len(shape), pipeline_mode=pl.Buffered(1))
    kernel = functools.partial(_proj_conv_kernel, width=width, chunk=chunk, q_scale=q_scale, tm=tm, ts=ts,
                               rc=rc, tiles_per_seq=seq // tm)
    return pl.pallas_call(
        kernel,
        grid=(t // tm,),
        in_specs=[pl.BlockSpec((tm, d), lambda i: (i, 0)),
                  const((d, n_cols)), const((1, n_cols)),
                  const((CONV_TAPS, SUBLANES, c)), const((1, c)), const((1, c)), const((1, c))],
        out_specs=[row_spec] * 6,
        out_shape=[out] * 6,
        scratch_shapes=[pltpu.VMEM((SUBLANES, CONV_HALO + ts, c), F32), pltpu.VMEM((CONV_HALO, c), F32)],
        compiler_params=_cparams(1),
        name="proj_conv",
    )(x2, w_in, b_in, jnp.broadcast_to(conv_w[:, None, :], (CONV_TAPS, SUBLANES, c)),
      conv_b.reshape(1, c), ln_g.reshape(1, c), ln_b.reshape(1, c))


def _sublane_all(x, op):
    for shift in (4, 2, 1):
        x = op(x, pltpu.roll(x, shift, 0))
    return x


def _attn_kernel(tab_ref, bucket_ref, lam_ref, q_ref, k_ref, v_ref, g_ref, o_ref,
                 bias_scr, qz_scr, vt_scr, s0_scr, s1_scr, p0_scr, p1_scr, a0_scr, a1_scr,
                 m_scr, acc_scr, *, tq, tk, dh, lam_init):
    hp = pl.program_id(0)
    b = pl.program_id(1)
    qi = pl.program_id(2)
    hw = 2 * dh
    acc_rows = hw + ONES_ROWS
    n_kv = vt_scr.shape[1]
    sw = tk
    n_sub = 2 * tq // sw
    heads = range(HEADS_PER_STEP)

    @pl.when((b == 0) & (qi == 0))
    def _build_bias():
        for hh in heads:
            h = hp * HEADS_PER_STEP + hh
            far = tab_ref[REL_BUCKETS - 1, h]
            for t in range(3):
                for c in range(tq // sw):
                    bk = bucket_ref[t, :, c * sw:(c + 1) * sw]
                    acc = jnp.full((tk, sw), NEG_INF, F32)
                    for r in range(REL_BUCKETS):
                        acc = jnp.where(bk == r, (tab_ref[r, h] - far) * LOG2E, acc)
                    bias_scr[hh, t, :, c * sw:(c + 1) * sw] = acc

    @pl.when(qi == 0)
    def _transpose_values():
        for hh in heads:
            for jj in range(n_kv):
                vt_scr[hh, jj, 0:hw, :] = (v_ref[0, jj * tk:(jj + 1) * tk, hh * hw:(hh + 1) * hw]
                                           .astype(F32).T.astype(BF16))
                vt_scr[hh, jj, hw:hw + ONES_ROWS, :] = jnp.ones((ONES_ROWS, tk), BF16)

    lane = lax.broadcasted_iota(I32, (tq, hw), 1)
    for hh in heads:
        q = q_ref[0, :, hh * hw:(hh + 1) * hw]
        zero = jnp.zeros_like(q)
        qz_scr[hh, 0:tq, :] = jnp.where(lane < dh, q, zero)
        qz_scr[hh, tq:2 * tq, :] = jnp.where(lane >= dh, q, zero)
    m_scr[...] = jnp.full(m_scr.shape, NEG_INF, F32)
    acc_scr[...] = jnp.zeros(acc_scr.shape, F32)

    s_bufs, p_bufs, a_bufs = (s0_scr, s1_scr), (p0_scr, p1_scr), (a0_scr, a1_scr)
    p1_scr[...] = jnp.zeros(p1_scr.shape, BF16)
    a1_scr[...] = jnp.ones(a1_scr.shape, F32)

    all_cols = (slice(0, 2 * tq),)
    late_cols = (slice(tk, tq), slice(tq + tk, 2 * tq))

    def issue_scores(j, slot, cols=all_cols):
        rows = (slice(j * tk, (j + 1) * tk) if isinstance(j, int)
                else pl.ds(pl.multiple_of(j * tk, tk), tk))
        for hh in heads:
            kj = k_ref[0, rows, hh * hw:(hh + 1) * hw]
            for cs in cols:
                s_bufs[slot][hh, :, cs] = lax.dot_general(kj, qz_scr[hh, cs, :], (((1,), (1,)), ((), ())),
                                                          preferred_element_type=F32)

    def accumulate(hh, j_prev, slot_prev, cs=slice(0, 2 * tq)):
        n = cs.stop - cs.start
        pv = jnp.dot(vt_scr[hh, j_prev], p_bufs[slot_prev][hh, :, cs], preferred_element_type=F32)
        acc3 = acc_scr[hh, :, cs].reshape(acc_rows // SUBLANES, SUBLANES, n)
        return (a_bufs[slot_prev][hh, :, cs][None] * acc3).reshape(acc_rows, n) + pv

    def step(j, bias_idx, slot, issue_next=True, next_is_last=False, last=False):
        if issue_next:
            issue_scores(j + 1, 1 - slot, late_cols if next_is_last else all_cols)
        acc_new = [accumulate(hh, jnp.maximum(j - 1, 0), 1 - slot) for hh in heads]
        for hh in heads:
            for c in range(n_sub):
                cs = slice(c * sw, (c + 1) * sw)
                qs = (c * sw) % tq
                if last and qs < tk:
                    continue
                s = s_bufs[slot][hh, :, cs]
                if bias_idx is not None:
                    s = s + bias_scr[hh, bias_idx, :, qs:qs + sw]
                s3 = s.reshape(tk // SUBLANES, SUBLANES, sw)
                m_prev = m_scr[hh, :, cs]
                m_new = jnp.maximum(m_prev, _sublane_all(jnp.max(s3, axis=0), jnp.maximum))
                p3 = jnp.exp2(s3 - m_new[None])
                p_bufs[slot][hh, :, cs] = p3.reshape(tk, sw).astype(BF16)
                a_bufs[slot][hh, :, cs] = jnp.exp2(m_prev - m_new)
                m_scr[hh, :, cs] = m_new
        for hh in heads:
            acc_scr[hh] = acc_new[hh]

    issue_scores(0, 0)

    @pl.when(qi == 0)
    def _first_query_tile():
        step(0, 1, 0, next_is_last=True)
        step(1, 2, 1, issue_next=False, last=True)

    @pl.when(qi >= 1)
    def _later_query_tiles():
        def far_pair(i, carry):
            step(2 * i, None, 0)
            step(2 * i + 1, None, 1)
            return carry

        lax.fori_loop(0, qi - 1, far_pair, 0)
        step(2 * qi - 2, None, 0)
        step(2 * qi - 1, 0, 1)
        step(2 * qi, 1, 0, next_is_last=True)
        step(2 * qi + 1, 2, 1, issue_next=False, last=True)

    drained = [[accumulate(hh, 2 * qi + 1, 1, cs) for cs in late_cols] for hh in heads]
    for hh in heads:
        for cs, val in zip(late_cols, drained[hh]):
            acc_scr[hh, :, cs] = val

    lv = lam_ref[...]
    lam = (jnp.exp(jnp.sum(lv[0:1] * lv[1:2], axis=-1, keepdims=True))
           - jnp.exp(jnp.sum(lv[2:3] * lv[3:4], axis=-1, keepdims=True)) + lam_init)
    gain = g_ref[...] * (1.0 - lam_init)
    for hh in heads:
        acc_all = acc_scr[hh].reshape(acc_rows // SUBLANES, SUBLANES, 2 * tq)
        inv_l = 1.0 / acc_all[hw // SUBLANES]
        acc3 = acc_all[0:hw // SUBLANES]
        o3 = acc3[:, :, 0:tq] * inv_l[None, :, 0:tq] - lam * (acc3[:, :, tq:2 * tq] * inv_l[None, :, tq:2 * tq])
        ms = _sublane_all(jnp.sum(o3 * o3, axis=0), jnp.add) * (1.0 / hw)
        y = (o3 * lax.rsqrt(ms + LN_EPS)[None]).reshape(hw, tq) * gain
        o_ref[0, :, hh * hw:(hh + 1) * hw] = y.T.astype(o_ref.dtype)


def _rel_bucket(dist):
    max_exact = REL_BUCKETS // 2
    d = jnp.maximum(dist, 1).astype(F32)
    large = max_exact + (jnp.log(d / max_exact) / math.log(REL_MAX_DIST / max_exact)
                         * (REL_BUCKETS - max_exact)).astype(I32)
    large = jnp.minimum(large, REL_BUCKETS - 1)
    return jnp.where(dist < max_exact, dist, large)


def _near_buckets(tq, tk):
    tiles = []
    for rel_tile in (-1, 0, 1):
        dist = (jnp.arange(tq, dtype=I32)[None, :]
                - (rel_tile * tk + jnp.arange(tk, dtype=I32))[:, None])
        tiles.append(jnp.where(dist >= 0, _rel_bucket(jnp.maximum(dist, 0)), -1))
    return jnp.stack(tiles).astype(I32)


def _attention(q, k, v, rel_bias, diff_lambda, head_norm_g, *, tk, lam_init):
    bsz, s, width = q.shape
    hw = width // N_HEADS
    tq = 2 * tk
    hps, pw = HEADS_PER_STEP, HEADS_PER_STEP * hw
    assert hw == 2 * HEAD_DIM and hw % LANES == 0 and s % tq == 0 and N_HEADS % hps == 0
    assert tk + 1 >= REL_MAX_DIST
    kernel = functools.partial(_attn_kernel, tq=tq, tk=tk, dh=HEAD_DIM, lam_init=lam_init)
    stat = pltpu.VMEM((hps, SUBLANES, 2 * tq), F32)
    return pl.pallas_call(
        kernel,
        grid=(N_HEADS // hps, bsz, s // tq),
        in_specs=[pl.BlockSpec(memory_space=pltpu.SMEM),
                  pl.BlockSpec((3, tk, tq), lambda h, b, i: (0, 0, 0)),
                  pl.BlockSpec((4, HEAD_DIM), lambda h, b, i: (0, 0)),
                  pl.BlockSpec((1, tq, pw), lambda h, b, i: (b, i, h)),
                  pl.BlockSpec((1, s, pw), lambda h, b, i: (b, 0, h)),
                  pl.BlockSpec((1, s, pw), lambda h, b, i: (b, 0, h)),
                  pl.BlockSpec((hw, 1), lambda h, b, i: (0, 0))],
        out_specs=pl.BlockSpec((1, tq, pw), lambda h, b, i: (b, i, h)),
        out_shape=jax.ShapeDtypeStruct((bsz, s, width), BF16),
        scratch_shapes=[pltpu.VMEM((hps, 3, tk, tq), F32),
                        pltpu.VMEM((hps, 2 * tq, hw), BF16),
                        pltpu.VMEM((hps, s // tk, hw + ONES_ROWS, tk), BF16),
                        pltpu.VMEM((hps, tk, 2 * tq), F32),
                        pltpu.VMEM((hps, tk, 2 * tq), F32),
                        pltpu.VMEM((hps, tk, 2 * tq), BF16),
                        pltpu.VMEM((hps, tk, 2 * tq), BF16),
                        stat, stat,
                        stat,
                        pltpu.VMEM((hps, hw + ONES_ROWS, 2 * tq), F32)],
        compiler_params=_cparams(3),
        name="attn",
    )(rel_bias, _near_buckets(tq, tk), diff_lambda, q, k, v, head_norm_g.reshape(hw, 1))


def _mix_kernel(o_ref, hc_ref, ga_ref, gb_ref, x_ref, woa_ref, wco_ref, bco_ref, wout_ref,
                g1_ref, b1_ref, wr_ref, br_ref, tri_ref,
                x1_ref, x1p_ref, mi_ref, mf_ref, cnt_ref, carry_scr, *, tm):
    i = pl.program_id(0)

    @pl.when(i == 0)
    def _init():
        carry_scr[...] = jnp.zeros(carry_scr.shape, F32)

    halves = [slice(0, tm // 2), slice(tm // 2, tm)]
    branch = [(jnp.dot(o_ref[h, :], woa_ref[...], preferred_element_type=F32),
               jnp.dot(hc_ref[h, :], wco_ref[...], preferred_element_type=F32) + bco_ref[...]) for h in halves]
    mixed = [jnp.dot((ga_ref[h, :].astype(F32) * y_a + gb_ref[h, :].astype(F32) * y_b).astype(BF16),
                     wout_ref[...], preferred_element_type=F32) for h, (y_a, y_b) in zip(halves, branch)]
    x1_halves = [_layer_norm(DN_ALPHA * x_ref[h, :] + mx, g1_ref[...], b1_ref[...]) for h, mx in zip(halves, mixed)]
    logits = jnp.concatenate([jnp.dot(xh.astype(BF16), wr_ref[...], preferred_element_type=F32)
                              for xh in x1_halves], axis=0) + br_ref[...]
    x1 = jnp.concatenate(x1_halves, axis=0)
    x1_ref[...] = x1
    _store_row_chunks(x1p_ref, (), _pack_halves(x1), tm)

    lane = lax.broadcasted_iota(I32, (tm, LANES), 1)
    g_lane = lane < N_GROUPS
    gl = jnp.where(g_lane, logits, NEG_INF)
    g_max = jnp.max(gl, axis=-1, keepdims=True)
    g_idx = jnp.min(jnp.where(gl == g_max, lane, LANES), axis=-1, keepdims=True)
    p_g = 1.0 / jnp.sum(jnp.where(g_lane, jnp.exp(gl - g_max), 0.0), axis=-1, keepdims=True)
    lo = N_GROUPS + g_idx * EXPERTS_PER_GROUP
    el = jnp.where(lane >= lo, jnp.where(lane < lo + EXPERTS_PER_GROUP, logits, NEG_INF), NEG_INF)
    v1 = jnp.max(el, axis=-1, keepdims=True)
    i1 = jnp.min(jnp.where(el == v1, lane, LANES), axis=-1, keepdims=True)
    el2 = jnp.where(lane == i1, NEG_INF, el)
    v2 = jnp.max(el2, axis=-1, keepdims=True)
    i2 = jnp.min(jnp.where(lane == i1, LANES, jnp.where(el2 == v2, lane, LANES)), axis=-1, keepdims=True)
    ex = jnp.exp(v2 - v1)
    w1 = 1.0 / (1.0 + ex)
    gate1 = p_g * w1
    gate2 = p_g * (ex * w1)
    e1 = i1 - N_GROUPS
    e2 = i2 - N_GROUPS

    carry = carry_scr[...]
    tri = tri_ref[...]
    ranks = []
    for e in (e1, e2):
        hit = lane == e
        oh = jnp.where(hit, 1.0, 0.0)
        before = jnp.dot(tri, oh.astype(BF16), preferred_element_type=F32) + carry
        ranks.append(jnp.sum(jnp.where(hit, before, 0.0), axis=-1, keepdims=True).astype(I32))
        carry = carry + jnp.sum(oh, axis=0, keepdims=True)
    carry_scr[...] = carry
    cnt_ref[...] = carry

    mi_ref[...] = jnp.where(lane == 0, e1, jnp.where(lane == 1, e2,
                            jnp.where(lane == 2, ranks[0], jnp.where(lane == 3, ranks[1], 0))))
    mf_ref[...] = jnp.where(lane == 0, gate1, jnp.where(lane == 1, gate2, 0.0))


def _mix(o_n, hc, ga, gb, x2, w_oa, w_co, b_co, w_out, ln_g, ln_b, w_r, b_r, *, tm):
    t, d = x2.shape
    assert d // 2 == ROW_CHUNKS * LANES
    row = lambda w: pl.BlockSpec((tm, w), lambda i: (i, 0))
    full = lambda a: pl.BlockSpec(a.shape, lambda i: (0,) * a.ndim, pipeline_mode=pl.Buffered(1))
    tri = (jnp.arange(tm)[:, None] > jnp.arange(tm)[None, :]).astype(BF16)
    b_co, ln_g, ln_b = b_co.reshape(1, d), ln_g.reshape(1, d), ln_b.reshape(1, d)
    return pl.pallas_call(
        functools.partial(_mix_kernel, tm=tm),
        grid=(t // tm,),
        in_specs=[row(d), row(d), row(d), row(d), row(d),
                  full(w_oa), full(w_co), full(b_co), full(w_out), full(ln_g), full(ln_b),
                  full(w_r), full(b_r), full(tri)],
        out_specs=[row(d), pl.BlockSpec((tm * ROW_CHUNKS, LANES), lambda i: (i, 0)), row(LANES), row(LANES),
                   pl.BlockSpec((1, LANES), lambda i: (0, 0))],
        out_shape=[jax.ShapeDtypeStruct((t, d), F32),
                   jax.ShapeDtypeStruct((t * ROW_CHUNKS, LANES), U32),
                   jax.ShapeDtypeStruct((t, LANES), I32),
                   jax.ShapeDtypeStruct((t, LANES), F32),
                   jax.ShapeDtypeStruct((1, LANES), F32)],
        scratch_shapes=[pltpu.VMEM((1, LANES), F32)],
        compiler_params=_cparams(1),
        name="mix",
    )(o_n, hc, ga, gb, x2, w_oa, w_co, b_co, w_out, ln_g, ln_b, w_r, b_r, tri)


def _row_copy(src, src_row, dst, dst_row, sem):
    def piece(row):
        return pl.ds(pl.multiple_of(row * ROW_CHUNKS, ROW_CHUNKS), ROW_CHUNKS)
    return pltpu.make_async_copy(src.at[piece(src_row), :], dst.at[piece(dst_row), :], sem)


def _dispatch_kernel(dest_ref, x_ref, xs_in_ref, xs_ref, sem, *, tm):
    del xs_in_ref
    base = pl.program_id(0) * tm

    def issue(g, carry):
        for u in range(ISSUE_UNROLL):
            r = g * ISSUE_UNROLL + u
            for c in range(TOP_K):
                _row_copy(x_ref, r, xs_ref, dest_ref[TOP_K * (base + r) + c], sem).start(priority=c)
        return carry

    lax.fori_loop(0, tm // ISSUE_UNROLL, issue, 0)
    for _ in range(TOP_K):
        pltpu.make_async_copy(x_ref, xs_ref.at[pl.ds(0, tm * ROW_CHUNKS), :], sem).wait()


def _dispatch(dest_flat, x1p, n_rows, *, tm):
    w = x1p.shape[1]
    t = x1p.shape[0] // ROW_CHUNKS
    assert tm % ISSUE_UNROLL == 0
    return pl.pallas_call(
        functools.partial(_dispatch_kernel, tm=tm),
        grid_spec=pltpu.PrefetchScalarGridSpec(
            num_scalar_prefetch=1,
            grid=(t // tm,),
            in_specs=[pl.BlockSpec((tm * ROW_CHUNKS, w), lambda i, dest: (i, 0)),
                      pl.BlockSpec(memory_space=pl.ANY)],
            out_specs=pl.BlockSpec(memory_space=pl.ANY),
            scratch_shapes=[pltpu.SemaphoreType.DMA(())]),
        out_shape=jax.ShapeDtypeStruct((n_rows * ROW_CHUNKS, w), U32),
        input_output_aliases={2: 0},
        compiler_params=_cparams(1),
        name="dispatch",
    )(dest_flat, x1p, jnp.zeros((n_rows * ROW_CHUNKS, w), U32))


def _expert_kernel(blk_e_ref, n_used_ref, next_e_ref, run_ref, xs_ref, wg_hbm, wu_hbm, wd_hbm, y_ref,
                   wg_f, wu_f, wd_f, wg_b, wu_b, wd_b, sems):
    i = pl.program_id(0)
    active = i < n_used_ref[0]
    tm = xs_ref.shape[0] // ROW_CHUNKS
    e = blk_e_ref[i]

    def weight_copies(expert, slot):
        return [pltpu.make_async_copy(hbm.at[expert], buf.at[slot], sems.at[slot, n])
                for n, (hbm, buf) in enumerate(((wg_hbm, wg_f), (wu_hbm, wu_f), (wd_hbm, wd_f)))]

    @pl.when(jnp.logical_not(active))
    def _unused_block():
        y_ref[...] = jnp.zeros(y_ref.shape, y_ref.dtype)

    @pl.when(i == 0)
    def _first_fetch():
        for cp in weight_copies(e, 0):
            cp.start()

    @pl.when(active & ((i == 0) | (e != blk_e_ref[jnp.maximum(i - 1, 0)])))
    def _new_expert():
        slot = run_ref[e] % 2
        for cp in weight_copies(e, slot):
            cp.wait()
        wg_b[...] = wg_f[slot].astype(BF16)
        wu_b[...] = wu_f[slot].astype(BF16)
        wd_b[...] = wd_f[slot].astype(BF16)

        @pl.when(next_e_ref[e] >= 0)
        def _prefetch_next():
            for cp in weight_copies(next_e_ref[e], 1 - slot):
                cp.start()

    @pl.when(active)
    def _():
        xb = _unpack_halves(_load_row_chunks(xs_ref, (), tm)).astype(BF16)
        g = jnp.dot(xb, wg_b[...], preferred_element_type=F32)
        u = jnp.dot(xb, wu_b[...], preferred_element_type=F32)
        hid = (g * _sigmoid(g) * u).astype(BF16)
        y = jnp.dot(hid, wd_b[...], preferred_element_type=F32)
        _store_row_chunks(y_ref, (), _pack_halves(y), tm)


def _experts(blk_e, n_used, next_e, run_idx, xs, w_gate, w_up, w_down, *, tm):
    w = xs.shape[1]
    n_rows = xs.shape[0] // ROW_CHUNKS
    _, d, de = w_gate.shape
    row_map = lambda i, blk_e, n_used, next_e, run_idx: (jnp.minimum(i, n_used[0] - 1), 0)
    hbm = pl.BlockSpec(memory_space=pl.ANY)
    return pl.pallas_call(
        _expert_kernel,
        grid_spec=pltpu.PrefetchScalarGridSpec(
            num_scalar_prefetch=4,
            grid=(n_rows // tm,),
            in_specs=[pl.BlockSpec((tm * ROW_CHUNKS, w), row_map), hbm, hbm, hbm],
            out_specs=pl.BlockSpec((tm * ROW_CHUNKS, w), lambda i, blk_e, n_used, next_e, run_idx: (i, 0)),
            scratch_shapes=[pltpu.VMEM((2, d, de), F32), pltpu.VMEM((2, d, de), F32), pltpu.VMEM((2, de, d), F32),
                            pltpu.VMEM((d, de), BF16), pltpu.VMEM((d, de), BF16), pltpu.VMEM((de, d), BF16),
                            pltpu.SemaphoreType.DMA((2, 3))]),
        out_shape=jax.ShapeDtypeStruct((n_rows * ROW_CHUNKS, w), U32),
        compiler_params=_cparams(1),
        name="experts",
    )(blk_e, n_used, next_e, run_idx, xs, w_gate, w_up, w_down)


def _combine_kernel(dest_ref, x1_ref, mf_ref, g_ref, b_ref, yp_ref, o_ref, buf, sems, *, tm, n_tiles):
    i = pl.program_id(0)

    def issue(tile, slot):
        base = tile * tm

        def body(g, carry):
            for u in range(ISSUE_UNROLL):
                r = g * ISSUE_UNROLL + u
                for c in range(TOP_K):
                    _row_copy(yp_ref, dest_ref[TOP_K * (base + r) + c], buf.at[slot, c], r,
                              sems.at[slot]).start(priority=c)
            return carry

        lax.fori_loop(0, tm // ISSUE_UNROLL, body, 0)

    @pl.when(i == 0)
    def _first():
        issue(0, 0)

    @pl.when(i + 1 < n_tiles)
    def _next():
        issue(i + 1, (i + 1) % 2)

    slot = i % 2
    for c in range(TOP_K):
        pltpu.make_async_copy(yp_ref.at[pl.ds(0, tm * ROW_CHUNKS), :], buf.at[slot, c], sems.at[slot]).wait()

    gates = mf_ref[...]
    ffn = (gates[:, 0:1] * _unpack_halves(_load_row_chunks(buf, (slot, 0), tm))
           + gates[:, 1:2] * _unpack_halves(_load_row_chunks(buf, (slot, 1), tm)))
    o_ref[...] = _layer_norm(DN_ALPHA * x1_ref[...] + ffn, g_ref[...], b_ref[...])


def _combine(dest_flat, x1, mf, ln_g, ln_b, yp, *, tm):
    t, d = x1.shape
    n_tiles = t // tm
    assert tm % ISSUE_UNROLL == 0
    vec = pl.BlockSpec((1, d), lambda i, dest: (0, 0))
    return pl.pallas_call(
        functools.partial(_combine_kernel, tm=tm, n_tiles=n_tiles),
        grid_spec=pltpu.PrefetchScalarGridSpec(
            num_scalar_prefetch=1,
            grid=(n_tiles,),
            in_specs=[pl.BlockSpec((tm, d), lambda i, dest: (i, 0)),
                      pl.BlockSpec((tm, LANES), lambda i, dest: (i, 0)),
                      vec, vec,
                      pl.BlockSpec(memory_space=pl.ANY)],
            out_specs=pl.BlockSpec((tm, d), lambda i, dest: (i, 0)),
            scratch_shapes=[pltpu.VMEM((2, TOP_K, tm * ROW_CHUNKS, LANES), U32),
                            pltpu.SemaphoreType.DMA((2,))]),
        out_shape=jax.ShapeDtypeStruct((t, d), F32),
        compiler_params=_cparams(1),
        name="combine",
    )(dest_flat, x1, mf, ln_g.reshape(1, d), ln_b.reshape(1, d), yp)


def _tile(n, pref):
    return pref if n % pref == 0 else n


def kernel(x, w_in, b_in, diff_lambda, head_norm_g, w_o_attn, rel_bias, conv_w, conv_b, conv_ln_g,
           conv_ln_b, w_conv_out, b_conv_out, w_out, ln1_g, ln1_b, router_g_w, router_g_b,
           router_e_w, router_e_b, expert_w_gate, expert_w_up, expert_w_down, ln2_g, ln2_b):
    bsz, s, d = x.shape
    t = bsz * s
    width = N_HEADS * 2 * HEAD_DIM
    assert w_in.shape[0] == DEPTH and w_in.shape[2] == 7 * width and width == d
    assert conv_w.shape[2] == width and TOP_K == 2
    tm_moe = _tile(t, 512)

    for li in range(DEPTH):
        lam_init = 0.8 - 0.6 * math.exp(-0.3 * li)
        x2 = x.reshape(t, d)
        q, k, v, ga, gb, hc = _proj_conv(
            x2, w_in[li].astype(BF16), b_in[li].reshape(1, -1), conv_w[li], conv_b[li], conv_ln_g[li],
            conv_ln_b[li], width=width, tm=_tile(s, 512), chunk=_tile(width, 512),
            q_scale=HEAD_DIM ** -0.5 * LOG2E, ts=_tile(s, 256), rc=16, seq=s)
        o_n = _attention(q.reshape(bsz, s, width), k.reshape(bsz, s, width), v.reshape(bsz, s, width),
                         rel_bias, diff_lambda[li], head_norm_g[li], tk=256 if s % 512 == 0 else 128,
                         lam_init=lam_init)

        n_r = N_GROUPS + N_EXPERTS
        w_r = jnp.pad(jnp.concatenate([router_g_w[li], router_e_w[li]], axis=1),
                      ((0, 0), (0, LANES - n_r))).astype(BF16)
        b_r = jnp.pad(jnp.concatenate([router_g_b[li], router_e_b[li]]), (0, LANES - n_r)).reshape(1, LANES)
        x1, x1p, mi, mf, cnt = _mix(
            o_n.reshape(t, width), hc.reshape(t, width), ga, gb, x2,
            w_o_attn[li].astype(BF16), w_conv_out[li].astype(BF16), b_conv_out[li],
            w_out[li].astype(BF16), ln1_g[li], ln1_b[li], w_r, b_r, tm=_tile(t, 512))

        counts = cnt[0, :N_EXPERTS].astype(I32)
        padded = (counts + tm_moe - 1) // tm_moe * tm_moe
        pad_ends = jnp.cumsum(padded)
        pad_starts = pad_ends - padded
        chosen = mi[:, 0:TOP_K, None] == jnp.arange(N_EXPERTS, dtype=I32)[None, None, :]
        dest_flat = (jnp.sum(jnp.where(chosen, pad_starts[None, None, :], 0), axis=-1)
                     + mi[:, TOP_K:2 * TOP_K]).reshape(t * TOP_K)
        n_rows = t * TOP_K + N_EXPERTS * tm_moe
        n_blocks = n_rows // tm_moe
        blk_start = jnp.arange(n_blocks, dtype=I32) * tm_moe
        blk_e = jnp.minimum(jnp.sum(blk_start[:, None] >= pad_ends[None, :], axis=1),
                            N_EXPERTS - 1).astype(I32)
        n_used = (pad_ends[-1:] // tm_moe).astype(I32)
        has_rows = padded > 0
        later = lax.cummin(jnp.where(has_rows, jnp.arange(N_EXPERTS, dtype=I32), N_EXPERTS), reverse=True)
        next_e = jnp.concatenate([later[1:], jnp.full((1,), N_EXPERTS, I32)])
        next_e = jnp.where(next_e < N_EXPERTS, next_e, -1).astype(I32)
        run_idx = (jnp.cumsum(has_rows.astype(I32)) - 1).astype(I32)

        xs = _dispatch(dest_flat, x1p, n_rows, tm=_tile(t, 512))
        yp = _experts(blk_e, n_used, next_e, run_idx, xs, expert_w_gate[li], expert_w_up[li], expert_w_down[li],
                      tm=tm_moe)
        x = _combine(dest_flat, x1, mf, ln2_g[li], ln2_b[li], yp, tm=_tile(t, 256)).reshape(bsz, s, d)
    return x
```

```python
import functools
import math

import jax
import jax.numpy as jnp
from jax import lax
from jax.experimental import pallas as pl
from jax.experimental.pallas import tpu as pltpu

F32 = jnp.float32
BF16 = jnp.bfloat16
U32 = jnp.uint32
I32 = jnp.int32

N_HEADS = 8
HEAD_DIM = 64
CONV_TAPS = 31
REL_BUCKETS = 32
REL_MAX_DIST = 128
N_GROUPS = 4
EXPERTS_PER_GROUP = 8
N_EXPERTS = N_GROUPS * EXPERTS_PER_GROUP
TOP_K = 2
DEPTH = 1
DN_ALPHA = (2.0 * DEPTH) ** 0.25
LN_EPS = 1e-5
NEG_INF = -1e30
LOG2E = 1.4426950408889634

LANES = 128
SUBLANES = 8
VMEM_LIMIT = 56 * 1024 * 1024
CONV_HALO = 32
CONV_SHIFT_ROWS = 56
PROJ_DOTS_PER_CONV_CHUNK = 2
HEADS_PER_STEP = 4
ONES_ROWS = 16
ROW_CHUNKS = 4
ISSUE_UNROLL = 8


def _cparams(n_axes):
    return pltpu.CompilerParams(dimension_semantics=("arbitrary",) * n_axes, vmem_limit_bytes=VMEM_LIMIT)


def _sigmoid(x):
    return 1.0 / (1.0 + jnp.exp(-x))


def _layer_norm(z, g, b):
    mu = jnp.mean(z, axis=-1, keepdims=True)
    zc = z - mu
    var = jnp.mean(zc * zc, axis=-1, keepdims=True)
    return zc * lax.rsqrt(var + LN_EPS) * g + b


def _pack_halves(y):
    n = y.shape[1] // 2
    bits = pltpu.bitcast(y.astype(BF16).astype(F32), U32)
    return (bits[:, :n] >> 16) | (bits[:, n:] & jnp.uint32(0xFFFF0000))


def _unpack_halves(w):
    lo = pltpu.bitcast(w << 16, F32)
    hi = pltpu.bitcast(w & jnp.uint32(0xFFFF0000), F32)
    return jnp.concatenate([lo, hi], axis=1)


def _store_row_chunks(ref, prefix, packed, m):
    for q in range(ROW_CHUNKS):
        ref[prefix + (pl.ds(q, m, stride=ROW_CHUNKS), slice(None))] = packed[:, q * LANES:(q + 1) * LANES]


def _load_row_chunks(ref, prefix, m):
    return jnp.concatenate([ref[prefix + (pl.ds(q, m, stride=ROW_CHUNKS), slice(None))]
                            for q in range(ROW_CHUNKS)], axis=1)


def _zero_after(value):
    bits = pltpu.bitcast(value[0:SUBLANES, 0:LANES], U32)
    return pltpu.bitcast((bits >> 16) >> 16, F32)[0:1, :]


def _conv_rows(sh, w_ref, b_ref, g_ref, bb_ref, o_ref, out_row0, r0, rc, anchor):
    off = CONV_HALO - (CONV_TAPS - 1)
    acc = None
    for t in range(CONV_TAPS):
        k, base = (off + t) % SUBLANES, (off + t) // SUBLANES * SUBLANES
        slab = sh[k, r0 + base:r0 + base + rc, :].reshape(rc // SUBLANES, SUBLANES, -1)
        term = slab * w_ref[t][None]
        acc = term if acc is None else acc + term
    bias = b_ref[...]
    if anchor is not None:
        bias = bias + jnp.tile(anchor, (1, bias.shape[1] // LANES))
    y = _layer_norm(acc.reshape(rc, -1) + bias, g_ref[...], bb_ref[...])
    o_ref[out_row0 + r0:out_row0 + r0 + rc, :] = (y * _sigmoid(y)).astype(o_ref.dtype)


def _proj_conv_kernel(x_ref, w_ref, b_ref, cw_ref, cb_ref, cg_ref, cbb_ref,
                      q_ref, k_ref, v_ref, ga_ref, gb_ref, hc_ref, sh, halo_scr,
                      *, width, chunk, q_scale, tm, ts, rc, tiles_per_seq):
    i = pl.program_id(0)
    xb = x_ref[...].astype(BF16)

    def lin(col0):
        return (jnp.dot(xb, w_ref[:, col0:col0 + chunk], preferred_element_type=F32)
                + b_ref[:, col0:col0 + chunk])

    glu = jnp.concatenate([(lin(3 * width + c) * _sigmoid(lin(4 * width + c))).astype(BF16).astype(F32)
                           for c in range(0, width, chunk)], axis=1)

    n_ext = CONV_HALO + ts
    n_sh = n_ext - SUBLANES

    def load_half(half):
        if half == 0:
            prev = halo_scr[...]
            sh[0, 0:CONV_HALO, :] = jnp.where(i % tiles_per_seq != 0, prev, jnp.zeros_like(prev))
        else:
            sh[0, 0:CONV_HALO, :] = glu[half * ts - CONV_HALO:half * ts]
        sh[0, CONV_HALO:n_ext, :] = glu[half * ts:(half + 1) * ts]
        for k in range(1, SUBLANES):
            for r0 in range(0, n_sh, CONV_SHIFT_ROWS):
                n = min(CONV_SHIFT_ROWS, n_sh - r0)
                sh[k, r0:r0 + n, :] = sh[0, r0 + k:r0 + k + n, :]

    def epilogue(ref, group, c, fn):
        def run():
            z = lin(group * width + c)
            ref[:, c:c + chunk] = fn(z).astype(BF16)
            return _zero_after(z)
        return run

    others = []
    for c in range(0, width, chunk):
        others += [epilogue(q_ref, 0, c, lambda z: z * q_scale), epilogue(k_ref, 1, c, lambda z: z),
                   epilogue(v_ref, 2, c, lambda z: z), epilogue(ga_ref, 5, c, _sigmoid),
                   epilogue(gb_ref, 6, c, _sigmoid)]
    conv_chunks = [(half, r0) for half in range(tm // ts) for r0 in range(0, ts, rc)]
    anchor = None
    for half, r0 in conv_chunks:
        if r0 == 0:
            load_half(half)
        _conv_rows(sh, cw_ref, cb_ref, cg_ref, cbb_ref, hc_ref, half * ts, r0, rc, anchor)
        for _ in range(PROJ_DOTS_PER_CONV_CHUNK):
            if others:
                anchor = others.pop(0)()
    for run in others:
        run()
    halo_scr[...] = glu[tm - CONV_HALO:tm]


def _proj_conv(x2, w_in, b_in, conv_w, conv_b, ln_g, ln_b, *, width, tm, chunk, q_scale, ts, rc, seq):
    t, d = x2.shape
    n_cols = w_in.shape[1]
    c = width
    assert seq % tm == 0 and tm % ts == 0 and ts % rc == 0 and CONV_HALO >= CONV_TAPS - 1 and ts >= CONV_HALO
    out = jax.ShapeDtypeStruct((t, width), BF16)
    row_spec = pl.BlockSpec((tm, width), lambda i: (i, 0))
    const = lambda shape: pl.BlockSpec(shape, lambda i: (0,) * len(shape), pipeline_mode=pl.Buffered(1))
    kernel = functools.partial(_proj_conv_kernel, width=width, chunk=chunk, q_scale=q_scale, tm=tm, ts=ts,
                               rc=rc, tiles_per_seq=seq // tm)
    return pl.pallas_call(
        kernel,
        grid=(t // tm,),
        in_specs=[pl.BlockSpec((tm, d), lambda i: (i, 0)),
                  const((d, n_cols)), const((1, n_cols)),
                  const((CONV_TAPS, SUBLANES, c)), const((1, c)), const((1, c)), const((1, c))],
        out_specs=[row_spec] * 6,
        out_shape=[out] * 6,
        scratch_shapes=[pltpu.VMEM((SUBLANES, CONV_HALO + ts, c), F32), pltpu.VMEM((CONV_HALO, c), F32)],
        compiler_params=_cparams(1),
        name="proj_conv",
    )(x2, w_in, b_in, jnp.broadcast_to(conv_w[:, None, :], (CONV_TAPS, SUBLANES, c)),
      conv_b.reshape(1, c), ln_g.reshape(1, c), ln_b.reshape(1, c))


def _sublane_all(x, op):
    for shift in (4, 2, 1):
        x = op(x, pltpu.roll(x, shift, 0))
    return x


def _attn_kernel(tab_ref, bucket_ref, lam_ref, q_ref, k_ref, v_ref, g_ref, o_ref,
                 bias_scr, qz_scr, vt_scr, s0_scr, s1_scr, p0_scr, p1_scr, a0_scr, a1_scr,
                 m_scr, acc_scr, *, tq, tk, dh, lam_init):
    hp = pl.program_id(0)
    b = pl.program_id(1)
    qi = pl.program_id(2)
    hw = 2 * dh
    acc_rows = hw + ONES_ROWS
    n_kv = vt_scr.shape[1]
    sw = tk
    n_sub = 2 * tq // sw
    heads = range(HEADS_PER_STEP)

    @pl.when((b == 0) & (qi == 0))
    def _build_bias():
        for hh in heads:
            h = hp * HEADS_PER_STEP + hh
            far = tab_ref[REL_BUCKETS - 1, h]
            for t in range(3):
                for c in range(tq // sw):
                    bk = bucket_ref[t, :, c * sw:(c + 1) * sw]
                    acc = jnp.full((tk, sw), NEG_INF, F32)
                    for r in range(REL_BUCKETS):
                        acc = jnp.where(bk == r, (tab_ref[r, h] - far) * LOG2E, acc)
                    bias_scr[hh, t, :, c * sw:(c + 1) * sw] = acc

    @pl.when(qi == 0)
    def _transpose_values():
        for hh in heads:
            for jj in range(n_kv):
                vt_scr[hh, jj, 0:hw, :] = (v_ref[0, jj * tk:(jj + 1) * tk, hh * hw:(hh + 1) * hw]
                                           .astype(F32).T.astype(BF16))
                vt_scr[hh, jj, hw:hw + ONES_ROWS, :] = jnp.ones((ONES_ROWS, tk), BF16)

    lane = lax.broadcasted_iota(I32, (tq, hw), 1)
    for hh in heads:
        q = q_ref[0, :, hh * hw:(hh + 1) * hw]
        zero = jnp.zeros_like(q)
        qz_scr[hh, 0:tq, :] = jnp.where(lane < dh, q, zero)
        qz_scr[hh, tq:2 * tq, :] = jnp.where(lane >= dh, q, zero)
    m_scr[...] = jnp.full(m_scr.shape, NEG_INF, F32)
    acc_scr[...] = jnp.zeros(acc_scr.shape, F32)

    s_bufs, p_bufs, a_bufs = (s0_scr, s1_scr), (p0_scr, p1_scr), (a0_scr, a1_scr)
    p1_scr[...] = jnp.zeros(p1_scr.shape, BF16)
    a1_scr[...] = jnp.ones(a1_scr.shape, F32)

    all_cols = (slice(0, 2 * tq),)
    late_cols = (slice(tk, tq), slice(tq + tk, 2 * tq))

    def issue_scores(j, slot, cols=all_cols):
        rows = (slice(j * tk, (j + 1) * tk) if isinstance(j, int)
                else pl.ds(pl.multiple_of(j * tk, tk), tk))
        for hh in heads:
            kj = k_ref[0, rows, hh * hw:(hh + 1) * hw]
            for cs in cols:
                s_bufs[slot][hh, :, cs] = lax.dot_general(kj, qz_scr[hh, cs, :], (((1,), (1,)), ((), ())),
                                                          preferred_element_type=F32)

    def accumulate(hh, j_prev, slot_prev, cs=slice(0, 2 * tq)):
        n = cs.stop - cs.start
        pv = jnp.dot(vt_scr[hh, j_prev], p_bufs[slot_prev][hh, :, cs], preferred_element_type=F32)
        acc3 = acc_scr[hh, :, cs].reshape(acc_rows // SUBLANES, SUBLANES, n)
        return (a_bufs[slot_prev][hh, :, cs][None] * acc3).reshape(acc_rows, n) + pv

    def step(j, bias_idx, slot, issue_next=True, next_is_last=False, last=False):
        if issue_next:
            issue_scores(j + 1, 1 - slot, late_cols if next_is_last else all_cols)
        acc_new = [accumulate(hh, jnp.maximum(j - 1, 0), 1 - slot) for hh in heads]
        for hh in heads:
            for c in range(n_sub):
                cs = slice(c * sw, (c + 1) * sw)
                qs = (c * sw) % tq
                if last and qs < tk:
                    continue
                s = s_bufs[slot][hh, :, cs]
                if bias_idx is not None:
                    s = s + bias_scr[hh, bias_idx, :, qs:qs + sw]
                s3 = s.reshape(tk // SUBLANES, SUBLANES, sw)
                m_prev = m_scr[hh, :, cs]
                m_new = jnp.maximum(m_prev, _sublane_all(jnp.max(s3, axis=0), jnp.maximum))
                p3 = jnp.exp2(s3 - m_new[None])
                p_bufs[slot][hh, :, cs] = p3.reshape(tk, sw).astype(BF16)
                a_bufs[slot][hh, :, cs] = jnp.exp2(m_prev - m_new)
                m_scr[hh, :, cs] = m_new
        for hh in heads:
            acc_scr[hh] = acc_new[hh]

    issue_scores(0, 0)

    @pl.when(qi == 0)
    def _first_query_tile():
        step(0, 1, 0, next_is_last=True)
        step(1, 2, 1, issue_next=False, last=True)

    @pl.when(qi >= 1)
    def _later_query_tiles():
        def far_pair(i, carry):
            step(2 * i, None, 0)
            step(2 * i + 1, None, 1)
            return carry

        lax.fori_loop(0, qi - 1, far_pair, 0)
        step(2 * qi - 2, None, 0)
        step(2 * qi - 1, 0, 1)
        step(2 * qi, 1, 0, next_is_last=True)
        step(2 * qi + 1, 2, 1, issue_next=False, last=True)

    drained = [[accumulate(hh, 2 * qi + 1, 1, cs) for cs in late_cols] for hh in heads]
    for hh in heads:
        for cs, val in zip(late_cols, drained[hh]):
            acc_scr[hh, :, cs] = val

    lv = lam_ref[...]
    lam = (jnp.exp(jnp.sum(lv[0:1] * lv[1:2], axis=-1, keepdims=True))
           - jnp.exp(jnp.sum(lv[2:3] * lv[3:4], axis=-1, keepdims=True)) + lam_init)
    gain = g_ref[...] * (1.0 - lam_init)
    for hh in heads:
        acc_all = acc_scr[hh].reshape(acc_rows // SUBLANES, SUBLANES, 2 * tq)
        inv_l = 1.0 / acc_all[hw // SUBLANES]
        acc3 = acc_all[0:hw // SUBLANES]
        o3 = acc3[:, :, 0:tq] * inv_l[None, :, 0:tq] - lam * (acc3[:, :, tq:2 * tq] * inv_l[None, :, tq:2 * tq])
        ms = _sublane_all(jnp.sum(o3 * o3, axis=0), jnp.add) * (1.0 / hw)
        y = (o3 * lax.rsqrt(ms + LN_EPS)[None]).reshape(hw, tq) * gain
        o_ref[0, :, hh * hw:(hh + 1) * hw] = y.T.astype(o_ref.dtype)


def _rel_bucket(dist):
    max_exact = REL_BUCKETS // 2
    d = jnp.maximum(dist, 1).astype(F32)
    large = max_exact + (jnp.log(d / max_exact) / math.log(REL_MAX_DIST / max_exact)
                         * (REL_BUCKETS - max_exact)).astype(I32)
    large = jnp.minimum(large, REL_BUCKETS - 1)
    return jnp.where(dist < max_exact, dist, large)


def _near_buckets(tq, tk):
    tiles = []
    for rel_tile in (-1, 0, 1):
        dist = (jnp.arange(tq, dtype=I32)[None, :]
                - (rel_tile * tk + jnp.arange(tk, dtype=I32))[:, None])
        tiles.append(jnp.where(dist >= 0, _rel_bucket(jnp.maximum(dist, 0)), -1))
    return jnp.stack(tiles).astype(I32)


def _attention(q, k, v, rel_bias, diff_lambda, head_norm_g, *, tk, lam_init):
    bsz, s, width = q.shape
    hw = width // N_HEADS
    tq = 2 * tk
    hps, pw = HEADS_PER_STEP, HEADS_PER_STEP * hw
    assert hw == 2 * HEAD_DIM and hw % LANES == 0 and s % tq == 0 and N_HEADS % hps == 0
    assert tk + 1 >= REL_MAX_DIST
    kernel = functools.partial(_attn_kernel, tq=tq, tk=tk, dh=HEAD_DIM, lam_init=lam_init)
    stat = pltpu.VMEM((hps, SUBLANES, 2 * tq), F32)
    return pl.pallas_call(
        kernel,
        grid=(N_HEADS // hps, bsz, s // tq),
        in_specs=[pl.BlockSpec(memory_space=pltpu.SMEM),
                  pl.BlockSpec((3, tk, tq), lambda h, b, i: (0, 0, 0)),
                  pl.BlockSpec((4, HEAD_DIM), lambda h, b, i: (0, 0)),
                  pl.BlockSpec((1, tq, pw), lambda h, b, i: (b, i, h)),
                  pl.BlockSpec((1, s, pw), lambda h, b, i: (b, 0, h)),
                  pl.BlockSpec((1, s, pw), lambda h, b, i: (b, 0, h)),
                  pl.BlockSpec((hw, 1), lambda h, b, i: (0, 0))],
        out_specs=pl.BlockSpec((1, tq, pw), lambda h, b, i: (b, i, h)),
        out_shape=jax.ShapeDtypeStruct((bsz, s, width), BF16),
        scratch_shapes=[pltpu.VMEM((hps, 3, tk, tq), F32),
                        pltpu.VMEM((hps, 2 * tq, hw), BF16),
                        pltpu.VMEM((hps, s // tk, hw + ONES_ROWS, tk), BF16),
                        pltpu.VMEM((hps, tk, 2 * tq), F32),
                        pltpu.VMEM((hps, tk, 2 * tq), F32),
                        pltpu.VMEM((hps, tk, 2 * tq), BF16),
                        pltpu.VMEM((hps, tk, 2 * tq), BF16),
                        stat, stat,
                        stat,
                        pltpu.VMEM((hps, hw + ONES_ROWS, 2 * tq), F32)],
        compiler_params=_cparams(3),
        name="attn",
    )(rel_bias, _near_buckets(tq, tk), diff_lambda, q, k, v, head_norm_g.reshape(hw, 1))


def _mix_kernel(o_ref, hc_ref, ga_ref, gb_ref, x_ref, woa_ref, wco_ref, bco_ref, wout_ref,
                g1_ref, b1_ref, wr_ref, br_ref, tri_ref,
                x1_ref, x1p_ref, mi_ref, mf_ref, cnt_ref, carry_scr, *, tm):
    i = pl.program_id(0)

    @pl.when(i == 0)
    def _init():
        carry_scr[...] = jnp.zeros(carry_scr.shape, F32)

    halves = [slice(0, tm // 2), slice(tm // 2, tm)]
    branch = [(jnp.dot(o_ref[h, :], woa_ref[...], preferred_element_type=F32),
               jnp.dot(hc_ref[h, :], wco_ref[...], preferred_element_type=F32) + bco_ref[...]) for h in halves]
    mixed = [jnp.dot((ga_ref[h, :].astype(F32) * y_a + gb_ref[h, :].astype(F32) * y_b).astype(BF16),
                     wout_ref[...], preferred_element_type=F32) for h, (y_a, y_b) in zip(halves, branch)]
    x1_halves = [_layer_norm(DN_ALPHA * x_ref[h, :] + mx, g1_ref[...], b1_ref[...]) for h, mx in zip(halves, mixed)]
    logits = jnp.concatenate([jnp.dot(xh.astype(BF16), wr_ref[...], preferred_element_type=F32)
                              for xh in x1_halves], axis=0) + br_ref[...]
    x1 = jnp.concatenate(x1_halves, axis=0)
    x1_ref[...] = x1
    _store_row_chunks(x1p_ref, (), _pack_halves(x1), tm)

    lane = lax.broadcasted_iota(I32, (tm, LANES), 1)
    g_lane = lane < N_GROUPS
    gl = jnp.where(g_lane, logits, NEG_INF)
    g_max = jnp.max(gl, axis=-1, keepdims=True)
    g_idx = jnp.min(jnp.where(gl == g_max, lane, LANES), axis=-1, keepdims=True)
    p_g = 1.0 / jnp.sum(jnp.where(g_lane, jnp.exp(gl - g_max), 0.0), axis=-1, keepdims=True)
    lo = N_GROUPS + g_idx * EXPERTS_PER_GROUP
    el = jnp.where(lane >= lo, jnp.where(lane < lo + EXPERTS_PER_GROUP, logits, NEG_INF), NEG_INF)
    v1 = jnp.max(el, axis=-1, keepdims=True)
    i1 = jnp.min(jnp.where(el == v1, lane, LANES), axis=-1, keepdims=True)
    el2 = jnp.where(lane == i1, NEG_INF, el)
    v2 = jnp.max(el2, axis=-1, keepdims=True)
    i2 = jnp.min(jnp.where(lane == i1, LANES, jnp.where(el2 == v2, lane, LANES)), axis=-1, keepdims=True)
    ex = jnp.exp(v2 - v1)
    w1 = 1.0 / (1.0 + ex)
    gate1 = p_g * w1
    gate2 = p_g * (ex * w1)
    e1 = i1 - N_GROUPS
    e2 = i2 - N_GROUPS

    carry = carry_scr[...]
    tri = tri_ref[...]
    ranks = []
    for e in (e1, e2):
        hit = lane == e
        oh = jnp.where(hit, 1.0, 0.0)
        before = jnp.dot(tri, oh.astype(BF16), preferred_element_type=F32) + carry
        ranks.append(jnp.sum(jnp.where(hit, before, 0.0), axis=-1, keepdims=True).astype(I32))
        carry = carry + jnp.sum(oh, axis=0, keepdims=True)
    carry_scr[...] = carry
    cnt_ref[...] = carry

    mi_ref[...] = jnp.where(lane == 0, e1, jnp.where(lane == 1, e2,
                            jnp.where(lane == 2, ranks[0], jnp.where(lane == 3, ranks[1], 0))))
    mf_ref[...] = jnp.where(lane == 0, gate1, jnp.where(lane == 1, gate2, 0.0))


def _mix(o_n, hc, ga, gb, x2, w_oa, w_co, b_co, w_out, ln_g, ln_b, w_r, b_r, *, tm):
    t, d = x2.shape
    assert d // 2 == ROW_CHUNKS * LANES
    row = lambda w: pl.BlockSpec((tm, w), lambda i: (i, 0))
    full = lambda a: pl.BlockSpec(a.shape, lambda i: (0,) * a.ndim, pipeline_mode=pl.Buffered(1))
    tri = (jnp.arange(tm)[:, None] > jnp.arange(tm)[None, :]).astype(BF16)
    b_co, ln_g, ln_b = b_co.reshape(1, d), ln_g.reshape(1, d), ln_b.reshape(1, d)
    return pl.pallas_call(
        functools.partial(_mix_kernel, tm=tm),
        grid=(t // tm,),
        in_specs=[row(d), row(d), row(d), row(d), row(d),
                  full(w_oa), full(w_co), full(b_co), full(w_out), full(ln_g), full(ln_b),
                  full(w_r), full(b_r), full(tri)],
        out_specs=[row(d), pl.BlockSpec((tm * ROW_CHUNKS, LANES), lambda i: (i, 0)), row(LANES), row(LANES),
                   pl.BlockSpec((1, LANES), lambda i: (0, 0))],
        out_shape=[jax.ShapeDtypeStruct((t, d), F32),
                   jax.ShapeDtypeStruct((t * ROW_CHUNKS, LANES), U32),
                   jax.ShapeDtypeStruct((t, LANES), I32),
                   jax.ShapeDtypeStruct((t, LANES), F32),
                   jax.ShapeDtypeStruct((1, LANES), F32)],
        scratch_shapes=[pltpu.VMEM((1, LANES), F32)],
        compiler_params=_cparams(1),
        name="mix",
    )(o_n, hc, ga, gb, x2, w_oa, w_co, b_co, w_out, ln_g, ln_b, w_r, b_r, tri)


def _row_copy(src, src_row, dst, dst_row, sem):
    def piece(row):
        return pl.ds(pl.multiple_of(row * ROW_CHUNKS, ROW_CHUNKS), ROW_CHUNKS)
    return pltpu.make_async_copy(src.at[piece(src_row), :], dst.at[piece(dst_row), :], sem)


def _dispatch_kernel(dest_ref, x_ref, xs_in_ref, xs_ref, sem, *, tm):
    del xs_in_ref
    base = pl.program_id(0) * tm

    def issue(g, carry):
        for u in range(ISSUE_UNROLL):
            r = g * ISSUE_UNROLL + u
            for c in range(TOP_K):
                _row_copy(x_ref, r, xs_ref, dest_ref[TOP_K * (base + r) + c], sem).start(priority=c)
        return carry

    lax.fori_loop(0, tm // ISSUE_UNROLL, issue, 0)
    for _ in range(TOP_K):
        pltpu.make_async_copy(x_ref, xs_ref.at[pl.ds(0, tm * ROW_CHUNKS), :], sem).wait()


def _dispatch(dest_flat, x1p, n_rows, *, tm):
    w = x1p.shape[1]
    t = x1p.shape[0] // ROW_CHUNKS
    assert tm % ISSUE_UNROLL == 0
    return pl.pallas_call(
        functools.partial(_dispatch_kernel, tm=tm),
        grid_spec=pltpu.PrefetchScalarGridSpec(
            num_scalar_prefetch=1,
            grid=(t // tm,),
            in_specs=[pl.BlockSpec((tm * ROW_CHUNKS, w), lambda i, dest: (i, 0)),
                      pl.BlockSpec(memory_space=pl.ANY)],
            out_specs=pl.BlockSpec(memory_space=pl.ANY),
            scratch_shapes=[pltpu.SemaphoreType.DMA(())]),
        out_shape=jax.ShapeDtypeStruct((n_rows * ROW_CHUNKS, w), U32),
        input_output_aliases={2: 0},
        compiler_params=_cparams(1),
        name="dispatch",
    )(dest_flat, x1p, jnp.zeros((n_rows * ROW_CHUNKS, w), U32))


def _expert_kernel(blk_e_ref, n_used_ref, next_e_ref, run_ref, xs_ref, wg_hbm, wu_hbm, wd_hbm, y_ref,
                   wg_f, wu_f, wd_f, wg_b, wu_b, wd_b, sems):
    i = pl.program_id(0)
    active = i < n_used_ref[0]
    tm = xs_ref.shape[0] // ROW_CHUNKS
    e = blk_e_ref[i]

    def weight_copies(expert, slot):
        return [pltpu.make_async_copy(hbm.at[expert], buf.at[slot], sems.at[slot, n])
                for n, (hbm, buf) in enumerate(((wg_hbm, wg_f), (wu_hbm, wu_f), (wd_hbm, wd_f)))]

    @pl.when(jnp.logical_not(active))
    def _unused_block():
        y_ref[...] = jnp.zeros(y_ref.shape, y_ref.dtype)

    @pl.when(i == 0)
    def _first_fetch():
        for cp in weight_copies(e, 0):
            cp.start()

    @pl.when(active & ((i == 0) | (e != blk_e_ref[jnp.maximum(i - 1, 0)])))
    def _new_expert():
        slot = run_ref[e] % 2
        for cp in weight_copies(e, slot):
            cp.wait()
        wg_b[...] = wg_f[slot].astype(BF16)
        wu_b[...] = wu_f[slot].astype(BF16)
        wd_b[...] = wd_f[slot].astype(BF16)

        @pl.when(next_e_ref[e] >= 0)
        def _prefetch_next():
            for cp in weight_copies(next_e_ref[e], 1 - slot):
                cp.start()

    @pl.when(active)
    def _():
        xb = _unpack_halves(_load_row_chunks(xs_ref, (), tm)).astype(BF16)
        g = jnp.dot(xb, wg_b[...], preferred_element_type=F32)
        u = jnp.dot(xb, wu_b[...], preferred_element_type=F32)
        hid = (g * _sigmoid(g) * u).astype(BF16)
        y = jnp.dot(hid, wd_b[...], preferred_element_type=F32)
        _store_row_chunks(y_ref, (), _pack_halves(y), tm)


def _experts(blk_e, n_used, next_e, run_idx, xs, w_gate, w_up, w_down, *, tm):
    w = xs.shape[1]
    n_rows = xs.shape[0] // ROW_CHUNKS
    _, d, de = w_gate.shape
    row_map = lambda i, blk_e, n_used, next_e, run_idx: (jnp.minimum(i, n_used[0] - 1), 0)
    hbm = pl.BlockSpec(memory_space=pl.ANY)
    return pl.pallas_call(
        _expert_kernel,
        grid_spec=pltpu.PrefetchScalarGridSpec(
            num_scalar_prefetch=4,
            grid=(n_rows // tm,),
            in_specs=[pl.BlockSpec((tm * ROW_CHUNKS, w), row_map), hbm, hbm, hbm],
            out_specs=pl.BlockSpec((tm * ROW_CHUNKS, w), lambda i, blk_e, n_used, next_e, run_idx: (i, 0)),
            scratch_shapes=[pltpu.VMEM((2, d, de), F32), pltpu.VMEM((2, d, de), F32), pltpu.VMEM((2, de, d), F32),
                            pltpu.VMEM((d, de), BF16), pltpu.VMEM((d, de), BF16), pltpu.VMEM((de, d), BF16),
                            pltpu.SemaphoreType.DMA((2, 3))]),
        out_shape=jax.ShapeDtypeStruct((n_rows * ROW_CHUNKS, w), U32),
        compiler_params=_cparams(1),
        name="experts",
    )(blk_e, n_used, next_e, run_idx, xs, w_gate, w_up, w_down)


def _combine_kernel(dest_ref, x1_ref, mf_ref, g_ref, b_ref, yp_ref, o_ref, buf, sems, *, tm, n_tiles):
    i = pl.program_id(0)

    def issue(tile, slot):
        base = tile * tm

        def body(g, carry):
            for u in range(ISSUE_UNROLL):
                r = g * ISSUE_UNROLL + u
                for c in range(TOP_K):
                    _row_copy(yp_ref, dest_ref[TOP_K * (base + r) + c], buf.at[slot, c], r,
                              sems.at[slot]).start(priority=c)
            return carry

        lax.fori_loop(0, tm // ISSUE_UNROLL, body, 0)

    @pl.when(i == 0)
    def _first():
        issue(0, 0)

    @pl.when(i + 1 < n_tiles)
    def _next():
        issue(i + 1, (i + 1) % 2)

    slot = i % 2
    for c in range(TOP_K):
        pltpu.make_async_copy(yp_ref.at[pl.ds(0, tm * ROW_CHUNKS), :], buf.at[slot, c], sems.at[slot]).wait()

    gates = mf_ref[...]
    ffn = (gates[:, 0:1] * _unpack_halves(_load_row_chunks(buf, (slot, 0), tm))
           + gates[:, 1:2] * _unpack_halves(_load_row_chunks(buf, (slot, 1), tm)))
    o_ref[...] = _layer_norm(DN_ALPHA * x1_ref[...] + ffn, g_ref[...], b_ref[...])


def _combine(dest_flat, x1, mf, ln_g, ln_b, yp, *, tm):
    t, d = x1.shape
    n_tiles = t // tm
    assert tm % ISSUE_UNROLL == 0
    vec = pl.BlockSpec((1, d), lambda i, dest: (0, 0))
    return pl.pallas_call(
        functools.partial(_combine_kernel, tm=tm, n_tiles=n_tiles),
        grid_spec=pltpu.PrefetchScalarGridSpec(
            num_scalar_prefetch=1,
            grid=(n_tiles,),
            in_specs=[pl.BlockSpec((tm, d), lambda i, dest: (i, 0)),
                      pl.BlockSpec((tm, LANES), lambda i, dest: (i, 0)),
                      vec, vec,
                      pl.BlockSpec(memory_space=pl.ANY)],
            out_specs=pl.BlockSpec((tm, d), lambda i, dest: (i, 0)),
            scratch_shapes=[pltpu.VMEM((2, TOP_K, tm * ROW_CHUNKS, LANES), U32),
                            pltpu.SemaphoreType.DMA((2,))]),
        out_shape=jax.ShapeDtypeStruct((t, d), F32),
        compiler_params=_cparams(1),
        name="combine",
    )(dest_flat, x1, mf, ln_g.reshape(1, d), ln_b.reshape(1, d), yp)


def _tile(n, pref):
    return pref if n % pref == 0 else n


def kernel(x, w_in, b_in, diff_lambda, head_norm_g, w_o_attn, rel_bias, conv_w, conv_b, conv_ln_g,
           conv_ln_b, w_conv_out, b_conv_out, w_out, ln1_g, ln1_b, router_g_w, router_g_b,
           router_e_w, router_e_b, expert_w_gate, expert_w_up, expert_w_down, ln2_g, ln2_b):
    bsz, s, d = x.shape
    t = bsz * s
    width = N_HEADS * 2 * HEAD_DIM
    assert w_in.shape[0] == DEPTH and w_in.shape[2] == 7 * width and width == d
    assert conv_w.shape[2] == width and TOP_K == 2
    tm_moe = _tile(t, 512)

    for li in range(DEPTH):
        lam_init = 0.8 - 0.6 * math.exp(-0.3 * li)
        x2 = x.reshape(t, d)
        q, k, v, ga, gb, hc = _proj_conv(
            x2, w_in[li].astype(BF16), b_in[li].reshape(1, -1), conv_w[li], conv_b[li], conv_ln_g[li],
            conv_ln_b[li], width=width, tm=_tile(s, 512), chunk=_tile(width, 512),
            q_scale=HEAD_DIM ** -0.5 * LOG2E, ts=_tile(s, 256), rc=16, seq=s)
        o_n = _attention(q.reshape(bsz, s, width), k.reshape(bsz, s, width), v.reshape(bsz, s, width),
                         rel_bias, diff_lambda[li], head_norm_g[li], tk=256 if s % 512 == 0 else 128,
                         lam_init=lam_init)

        n_r = N_GROUPS + N_EXPERTS
        w_r = jnp.pad(jnp.concatenate([router_g_w[li], router_e_w[li]], axis=1),
                      ((0, 0), (0, LANES - n_r))).astype(BF16)
        b_r = jnp.pad(jnp.concatenate([router_g_b[li], router_e_b[li]]), (0, LANES - n_r)).reshape(1, LANES)
        x1, x1p, mi, mf, cnt = _mix(
            o_n.reshape(t, width), hc.reshape(t, width), ga, gb, x2,
            w_o_attn[li].astype(BF16), w_conv_out[li].astype(BF16), b_conv_out[li],
            w_out[li].astype(BF16), ln1_g[li], ln1_b[li], w_r, b_r, tm=_tile(t, 512))

        counts = cnt[0, :N_EXPERTS].astype(I32)
        padded = (counts + tm_moe - 1) // tm_moe * tm_moe
        pad_ends = jnp.cumsum(padded)
        pad_starts = pad_ends - padded
        chosen = mi[:, 0:TOP_K, None] == jnp.arange(N_EXPERTS, dtype=I32)[None, None, :]
        dest_flat = (jnp.sum(jnp.where(chosen, pad_starts[None, None, :], 0), axis=-1)
                     + mi[:, TOP_K:2 * TOP_K]).reshape(t * TOP_K)
        n_rows = t * TOP_K + N_EXPERTS * tm_moe
        n_blocks = n_rows // tm_moe
        blk_start = jnp.arange(n_blocks, dtype=I32) * tm_moe
        blk_e = jnp.minimum(jnp.sum(blk_start[:, None] >= pad_ends[None, :], axis=1),
                            N_EXPERTS - 1).astype(I32)
        n_used = (pad_ends[-1:] // tm_moe).astype(I32)
        has_rows = padded > 0
        later = lax.cummin(jnp.where(has_rows, jnp.arange(N_EXPERTS, dtype=I32), N_EXPERTS), reverse=True)
        next_e = jnp.concatenate([later[1:], jnp.full((1,), N_EXPERTS, I32)])
        next_e = jnp.where(next_e < N_EXPERTS, next_e, -1).astype(I32)
        run_idx = (jnp.cumsum(has_rows.astype(I32)) - 1).astype(I32)

        xs = _dispatch(dest_flat, x1p, n_rows, tm=_tile(t, 512))
        yp = _experts(blk_e, n_used, next_e, run_idx, xs, expert_w_gate[li], expert_w_up[li], expert_w_down[li],
                      tm=tm_moe)
        x = _combine(dest_flat, x1, mf, ln2_g[li], ln2_b[li], yp, tm=_tile(t, 256)).reshape(bsz, s, d)
    return x
```

```python
import functools
import math

import jax
import jax.numpy as jnp
from jax import lax
from jax.experimental import pallas as pl
from jax.experimental.pallas import tpu as pltpu

F32 = jnp.float32
BF16 = jnp.bfloat16
U32 = jnp.uint32
I32 = jnp.int32

N_HEADS = 8
HEAD_DIM = 64
CONV_TAPS = 31
REL_BUCKETS = 32
REL_MAX_DIST = 128
N_GROUPS = 4
EXPERTS_PER_GROUP = 8
N_EXPERTS = N_GROUPS * EXPERTS_PER_GROUP
TOP_K = 2
DEPTH = 1
DN_ALPHA = (2.0 * DEPTH) ** 0.25
LN_EPS = 1e-5
NEG_INF = -1e30
LOG2E = 1.4426950408889634

LANES = 128
SUBLANES = 8
VMEM_LIMIT = 56 * 1024 * 1024
CONV_HALO = 32
CONV_SHIFT_ROWS = 56
PROJ_DOTS_PER_CONV_CHUNK = 2
HEADS_PER_STEP = 4
ONES_ROWS = 16
ROW_CHUNKS = 4
ISSUE_UNROLL = 8


def _cparams(n_axes):
    return pltpu.CompilerParams(dimension_semantics=("arbitrary",) * n_axes, vmem_limit_bytes=VMEM_LIMIT)


def _sigmoid(x):
    return 1.0 / (1.0 + jnp.exp(-x))


def _layer_norm(z, g, b):
    mu = jnp.mean(z, axis=-1, keepdims=True)
    zc = z - mu
    var = jnp.mean(zc * zc, axis=-1, keepdims=True)
    return zc * lax.rsqrt(var + LN_EPS) * g + b


def _pack_halves(y):
    n = y.shape[1] // 2
    bits = pltpu.bitcast(y.astype(BF16).astype(F32), U32)
    return (bits[:, :n] >> 16) | (bits[:, n:] & jnp.uint32(0xFFFF0000))


def _unpack_halves(w):
    lo = pltpu.bitcast(w << 16, F32)
    hi = pltpu.bitcast(w & jnp.uint32(0xFFFF0000), F32)
    return jnp.concatenate([lo, hi], axis=1)


def _store_row_chunks(ref, prefix, packed, m):
    for q in range(ROW_CHUNKS):
        ref[prefix + (pl.ds(q, m, stride=ROW_CHUNKS), slice(None))] = packed[:, q * LANES:(q + 1) * LANES]


def _load_row_chunks(ref, prefix, m):
    return jnp.concatenate([ref[prefix + (pl.ds(q, m, stride=ROW_CHUNKS), slice(None))]
                            for q in range(ROW_CHUNKS)], axis=1)


def _zero_after(value):
    bits = pltpu.bitcast(value[0:SUBLANES, 0:LANES], U32)
    return pltpu.bitcast((bits >> 16) >> 16, F32)[0:1, :]


def _conv_rows(sh, w_ref, b_ref, g_ref, bb_ref, o_ref, out_row0, r0, rc, anchor):
    off = CONV_HALO - (CONV_TAPS - 1)
    acc = None
    for t in range(CONV_TAPS):
        k, base = (off + t) % SUBLANES, (off + t) // SUBLANES * SUBLANES
        slab = sh[k, r0 + base:r0 + base + rc, :].reshape(rc // SUBLANES, SUBLANES, -1)
        term = slab * w_ref[t][None]
        acc = term if acc is None else acc + term
    bias = b_ref[...]
    if anchor is not None:
        bias = bias + jnp.tile(anchor, (1, bias.shape[1] // LANES))
    y = _layer_norm(acc.reshape(rc, -1) + bias, g_ref[...], bb_ref[...])
    o_ref[out_row0 + r0:out_row0 + r0 + rc, :] = (y * _sigmoid(y)).astype(o_ref.dtype)


def _proj_conv_kernel(x_ref, w_ref, b_ref, cw_ref, cb_ref, cg_ref, cbb_ref,
                      q_ref, k_ref, v_ref, ga_ref, gb_ref, hc_ref, sh, halo_scr,
                      *, width, chunk, q_scale, tm, ts, rc, tiles_per_seq):
    i = pl.program_id(0)
    xb = x_ref[...].astype(BF16)

    def lin(col0):
        return (jnp.dot(xb, w_ref[:, col0:col0 + chunk], preferred_element_type=F32)
                + b_ref[:, col0:col0 + chunk])

    glu = jnp.concatenate([(lin(3 * width + c) * _sigmoid(lin(4 * width + c))).astype(BF16).astype(F32)
                           for c in range(0, width, chunk)], axis=1)

    n_ext = CONV_HALO + ts
    n_sh = n_ext - SUBLANES

    def load_half(half):
        if half == 0:
            prev = halo_scr[...]
            sh[0, 0:CONV_HALO, :] = jnp.where(i % tiles_per_seq != 0, prev, jnp.zeros_like(prev))
        else:
            sh[0, 0:CONV_HALO, :] = glu[half * ts - CONV_HALO:half * ts]
        sh[0, CONV_HALO:n_ext, :] = glu[half * ts:(half + 1) * ts]
        for k in range(1, SUBLANES):
            for r0 in range(0, n_sh, CONV_SHIFT_ROWS):
                n = min(CONV_SHIFT_ROWS, n_sh - r0)
                sh[k, r0:r0 + n, :] = sh[0, r0 + k:r0 + k + n, :]

    def epilogue(ref, group, c, fn):
        def run():
            z = lin(group * width + c)
            ref[:, c:c + chunk] = fn(z).astype(BF16)
            return _zero_after(z)
        return run

    others = []
    for c in range(0, width, chunk):
        others += [epilogue(q_ref, 0, c, lambda z: z * q_scale), epilogue(k_ref, 1, c, lambda z: z),
                   epilogue(v_ref, 2, c, lambda z: z), epilogue(ga_ref, 5, c, _sigmoid),
                   epilogue(gb_ref, 6, c, _sigmoid)]
    conv_chunks = [(half, r0) for half in range(tm // ts) for r0 in range(0, ts, rc)]
    anchor = None
    for half, r0 in conv_chunks:
        if r0 == 0:
            load_half(half)
        _conv_rows(sh, cw_ref, cb_ref, cg_ref, cbb_ref, hc_ref, half * ts, r0, rc, anchor)
        for _ in range(PROJ_DOTS_PER_CONV_CHUNK):
            if others:
                anchor = others.pop(0)()
    for run in others:
        run()
    halo_scr[...] = glu[tm - CONV_HALO:tm]


def _proj_conv(x2, w_in, b_in, conv_w, conv_b, ln_g, ln_b, *, width, tm, chunk, q_scale, ts, rc, seq):
    t, d = x2.shape
    n_cols = w_in.shape[1]
    c = width
    assert seq % tm == 0 and tm % ts == 0 and ts % rc == 0 and CONV_HALO >= CONV_TAPS - 1 and ts >= CONV_HALO
    out = jax.ShapeDtypeStruct((t, width), BF16)
    row_spec = pl.BlockSpec((tm, width), lambda i: (i, 0))
    const = lambda shape: pl.BlockSpec(shape, lambda i: (0,) * len(shape), pipeline_mode=pl.Buffered(1))
    kernel = functools.partial(_proj_conv_kernel, width=width, chunk=chunk, q_scale=q_scale, tm=tm, ts=ts,
                               rc=rc, tiles_per_seq=seq // tm)
    return pl.pallas_call(
        kernel,
        grid=(t // tm,),
        in_specs=[pl.BlockSpec((tm, d), lambda i: (i, 0)),
                  const((d, n_cols)), const((1, n_cols)),
                  const((CONV_TAPS, SUBLANES, c)), const((1, c)), const((1, c)), const((1, c))],
        out_specs=[row_spec] * 6,
        out_shape=[out] * 6,
        scratch_shapes=[pltpu.VMEM((SUBLANES, CONV_HALO + ts, c), F32), pltpu.VMEM((CONV_HALO, c), F32)],
        compiler_params=_cparams(1),
        name="proj_conv",
    )(x2, w_in, b_in, jnp.broadcast_to(conv_w[:, None, :], (CONV_TAPS, SUBLANES, c)),
      conv_b.reshape(1, c), ln_g.reshape(1, c), ln_b.reshape(1, c))


def _sublane_all(x, op):
    for shift in (4, 2, 1):
        x = op(x, pltpu.roll(x, shift, 0))
    return x


def _attn_kernel(tab_ref, bucket_ref, lam_ref, q_ref, k_ref, v_ref, g_ref, o_ref,
                 bias_scr, qz_scr, vt_scr, s0_scr, s1_scr, p0_scr, p1_scr, a0_scr, a1_scr,
                 m_scr, acc_scr, *, tq, tk, dh, lam_init):
    hp = pl.program_id(0)
    b = pl.program_id(1)
    qi = pl.program_id(2)
    hw = 2 * dh
    acc_rows = hw + ONES_ROWS
    n_kv = vt_scr.shape[1]
    sw = tk
    n_sub = 2 * tq // sw
    heads = range(HEADS_PER_STEP)

    @pl.when((b == 0) & (qi == 0))
    def _build_bias():
        for hh in heads:
            h = hp * HEADS_PER_STEP + hh
            far = tab_ref[REL_BUCKETS - 1, h]
            for t in range(3):
                for c in range(tq // sw):
                    bk = bucket_ref[t, :, c * sw:(c + 1) * sw]
                    acc = jnp.full((tk, sw), NEG_INF, F32)
                    for r in range(REL_BUCKETS):
                        acc = jnp.where(bk == r, (tab_ref[r, h] - far) * LOG2E, acc)
                    bias_scr[hh, t, :, c * sw:(c + 1) * sw] = acc

    @pl.when(qi == 0)
    def _transpose_values():
        for hh in heads:
            for jj in range(n_kv):
                vt_scr[hh, jj, 0:hw, :] = (v_ref[0, jj * tk:(jj + 1) * tk, hh * hw:(hh + 1) * hw]
                                           .astype(F32).T.astype(BF16))
                vt_scr[hh, jj, hw:hw + ONES_ROWS, :] = jnp.ones((ONES_ROWS, tk), BF16)

    lane = lax.broadcasted_iota(I32, (tq, hw), 1)
    for hh in heads:
        q = q_ref[0, :, hh * hw:(hh + 1) * hw]
        zero = jnp.zeros_like(q)
        qz_scr[hh, 0:tq, :] = jnp.where(lane < dh, q, zero)
        qz_scr[hh, tq:2 * tq, :] = jnp.where(lane >= dh, q, zero)
    m_scr[...] = jnp.full(m_scr.shape, NEG_INF, F32)
    acc_scr[...] = jnp.zeros(acc_scr.shape, F32)

    s_bufs, p_bufs, a_bufs = (s0_scr, s1_scr), (p0_scr, p1_scr), (a0_scr, a1_scr)
    p1_scr[...] = jnp.zeros(p1_scr.shape, BF16)
    a1_scr[...] = jnp.ones(a1_scr.shape, F32)

    all_cols = (slice(0, 2 * tq),)
    late_cols = (slice(tk, tq), slice(tq + tk, 2 * tq))

    def issue_scores(j, slot, cols=all_cols):
        rows = (slice(j * tk, (j + 1) * tk) if isinstance(j, int)
                else pl.ds(pl.multiple_of(j * tk, tk), tk))
        for hh in heads:
            kj = k_ref[0, rows, hh * hw:(hh + 1) * hw]
            for cs in cols:
                s_bufs[slot][hh, :, cs] = lax.dot_general(kj, qz_scr[hh, cs, :], (((1,), (1,)), ((), ())),
                                                          preferred_element_type=F32)

    def accumulate(hh, j_prev, slot_prev, cs=slice(0, 2 * tq)):
        n = cs.stop - cs.start
        pv = jnp.dot(vt_scr[hh, j_prev], p_bufs[slot_prev][hh, :, cs], preferred_element_type=F32)
        acc3 = acc_scr[hh, :, cs].reshape(acc_rows // SUBLANES, SUBLANES, n)
        return (a_bufs[slot_prev][hh, :, cs][None] * acc3).reshape(acc_rows, n) + pv

    def step(j, bias_idx, slot, issue_next=True, next_is_last=False, last=False):
        if issue_next:
            issue_scores(j + 1, 1 - slot, late_cols if next_is_last else all_cols)
        acc_new = [accumulate(hh, jnp.maximum(j - 1, 0), 1 - slot) for hh in heads]
        for hh in heads:
            for c in range(n_sub):
                cs = slice(c * sw, (c + 1) * sw)
                qs = (c * sw) % tq
                if last and qs < tk:
                    continue
                s = s_bufs[slot][hh, :, cs]
                if bias_idx is not None:
                    s = s + bias_scr[hh, bias_idx, :, qs:qs + sw]
                s3 = s.reshape(tk // SUBLANES, SUBLANES, sw)
                m_prev = m_scr[hh, :, cs]
                m_new = jnp.maximum(m_prev, _sublane_all(jnp.max(s3, axis=0), jnp.maximum))
                p3 = jnp.exp2(s3 - m_new[None])
                p_bufs[slot][hh, :, cs] = p3.reshape(tk, sw).astype(BF16)
                a_bufs[slot][hh, :, cs] = jnp.exp2(m_prev - m_new)
                m_scr[hh, :, cs] = m_new
        for hh in heads:
            acc_scr[hh] = acc_new[hh]

    issue_scores(0, 0)

    @pl.when(qi == 0)
    def _first_query_tile():
        step(0, 1, 0, next_is_last=True)
        step(1, 2, 1, issue_next=False, last=True)

    @pl.when(qi >= 1)
    def _later_query_tiles():
        def far_pair(i, carry):
            step(2 * i, None, 0)
            step(2 * i + 1, None, 1)
            return carry

        lax.fori_loop(0, qi - 1, far_pair, 0)
        step(2 * qi - 2, None, 0)
        step(2 * qi - 1, 0, 1)
        step(2 * qi, 1, 0, next_is_last=True)
        step(2 * qi + 1, 2, 1, issue_next=False, last=True)

    drained = [[accumulate(hh, 2 * qi + 1, 1, cs) for cs in late_cols] for hh in heads]
    for hh in heads:
        for cs, val in zip(late_cols, drained[hh]):
            acc_scr[hh, :, cs] = val

    lv = lam_ref[...]
    lam = (jnp.exp(jnp.sum(lv[0:1] * lv[1:2], axis=-1, keepdims=True))
           - jnp.exp(jnp.sum(lv[2:3] * lv[3:4], axis=-1, keepdims=True)) + lam_init)
    gain = g_ref[...] * (1.0 - lam_init)
    for hh in heads:
        acc_all = acc_scr[hh].reshape(acc_rows // SUBLANES, SUBLANES, 2 * tq)
        inv_l = 1.0 / acc_all[hw // SUBLANES]
        acc3 = acc_all[0:hw // SUBLANES]
        o3 = acc3[:, :, 0:tq] * inv_l[None, :, 0:tq] - lam * (acc3[:, :, tq:2 * tq] * inv_l[None, :, tq:2 * tq])
        ms = _sublane_all(jnp.sum(o3 * o3, axis=0), jnp.add) * (1.0 / hw)
        y = (o3 * lax.rsqrt(ms + LN_EPS)[None]).reshape(hw, tq) * gain
        o_ref[0, :, hh * hw:(hh + 1) * hw] = y.T.astype(o_ref.dtype)


def _rel_bucket(dist):
    max_exact = REL_BUCKETS // 2
    d = jnp.maximum(dist, 1).astype(F32)
    large = max_exact + (jnp.log(d / max_exact) / math.log(REL_MAX_DIST / max_exact)
                         * (REL_BUCKETS - max_exact)).astype(I32)
    large = jnp.minimum(large, REL_BUCKETS - 1)
    return jnp.where(dist < max_exact, dist, large)


def _near_buckets(tq, tk):
    tiles = []
    for rel_tile in (-1, 0, 1):
        dist = (jnp.arange(tq, dtype=I32)[None, :]
                - (rel_tile * tk + jnp.arange(tk, dtype=I32))[:, None])
        tiles.append(jnp.where(dist >= 0, _rel_bucket(jnp.maximum(dist, 0)), -1))
    return jnp.stack(tiles).astype(I32)


def _attention(q, k, v, rel_bias, diff_lambda, head_norm_g, *, tk, lam_init):
    bsz, s, width = q.shape
    hw = width // N_HEADS
    tq = 2 * tk
    hps, pw = HEADS_PER_STEP, HEADS_PER_STEP * hw
    assert hw == 2 * HEAD_DIM and hw % LANES == 0 and s % tq == 0 and N_HEADS % hps == 0
    assert tk + 1 >= REL_MAX_DIST
    kernel = functools.partial(_attn_kernel, tq=tq, tk=tk, dh=HEAD_DIM, lam_init=lam_init)
    stat = pltpu.VMEM((hps, SUBLANES, 2 * tq), F32)
    return pl.pallas_call(
        kernel,
        grid=(N_HEADS // hps, bsz, s // tq),
        in_specs=[pl.BlockSpec(memory_space=pltpu.SMEM),
                  pl.BlockSpec((3, tk, tq), lambda h, b, i: (0, 0, 0)),
                  pl.BlockSpec((4, HEAD_DIM), lambda h, b, i: (0, 0)),
                  pl.BlockSpec((1, tq, pw), lambda h, b, i: (b, i, h)),
                  pl.BlockSpec((1, s, pw), lambda h, b, i: (b, 0, h)),
                  pl.BlockSpec((1, s, pw), lambda h, b, i: (b, 0, h)),
                  pl.BlockSpec((hw, 1), lambda h, b, i: (0, 0))],
        out_specs=pl.BlockSpec((1, tq, pw), lambda h, b, i: (b, i, h)),
        out_shape=jax.ShapeDtypeStruct((bsz, s, width), BF16),
        scratch_shapes=[pltpu.VMEM((hps, 3, tk, tq), F32),
                        pltpu.VMEM((hps, 2 * tq, hw), BF16),
                        pltpu.VMEM((hps, s // tk, hw + ONES_ROWS, tk), BF16),
                        pltpu.VMEM((hps, tk, 2 * tq), F32),
                        pltpu.VMEM((hps, tk, 2 * tq), F32),
                        pltpu.VMEM((hps, tk, 2 * tq), BF16),
                        pltpu.VMEM((hps, tk, 2 * tq), BF16),
                        stat, stat,
                        stat,
                        pltpu.VMEM((hps, hw + ONES_ROWS, 2 * tq), F32)],
        compiler_params=_cparams(3),
        name="attn",
    )(rel_bias, _near_buckets(tq, tk), diff_lambda, q, k, v, head_norm_g.reshape(hw, 1))


def _mix_kernel(o_ref, hc_ref, ga_ref, gb_ref, x_ref, woa_ref, wco_ref, bco_ref, wout_ref,
                g1_ref, b1_ref, wr_ref, br_ref, tri_ref,
                x1_ref, x1p_ref, mi_ref, mf_ref, cnt_ref, carry_scr, *, tm):
    i = pl.program_id(0)

    @pl.when(i == 0)
    def _init():
        carry_scr[...] = jnp.zeros(carry_scr.shape, F32)

    halves = [slice(0, tm // 2), slice(tm // 2, tm)]
    branch = [(jnp.dot(o_ref[h, :], woa_ref[...], preferred_element_type=F32),
               jnp.dot(hc_ref[h, :], wco_ref[...], preferred_element_type=F32) + bco_ref[...]) for h in halves]
    mixed = [jnp.dot((ga_ref[h, :].astype(F32) * y_a + gb_ref[h, :].astype(F32) * y_b).astype(BF16),
                     wout_ref[...], preferred_element_type=F32) for h, (y_a, y_b) in zip(halves, branch)]
    x1_halves = [_layer_norm(DN_ALPHA * x_ref[h, :] + mx, g1_ref[...], b1_ref[...]) for h, mx in zip(halves, mixed)]
    logits = jnp.concatenate([jnp.dot(xh.astype(BF16), wr_ref[...], preferred_element_type=F32)
                              for xh in x1_halves], axis=0) + br_ref[...]
    x1 = jnp.concatenate(x1_halves, axis=0)
    x1_ref[...] = x1
    _store_row_chunks(x1p_ref, (), _pack_halves(x1), tm)

    lane = lax.broadcasted_iota(I32, (tm, LANES), 1)
    g_lane = lane < N_GROUPS
    gl = jnp.where(g_lane, logits, NEG_INF)
    g_max = jnp.max(gl, axis=-1, keepdims=True)
    g_idx = jnp.min(jnp.where(gl == g_max, lane, LANES), axis=-1, keepdims=True)
    p_g = 1.0 / jnp.sum(jnp.where(g_lane, jnp.exp(gl - g_max), 0.0), axis=-1, keepdims=True)
    lo = N_GROUPS + g_idx * EXPERTS_PER_GROUP
    el = jnp.where(lane >= lo, jnp.where(lane < lo + EXPERTS_PER_GROUP, logits, NEG_INF), NEG_INF)
    v1 = jnp.max(el, axis=-1, keepdims=True)
    i1 = jnp.min(jnp.where(el == v1, lane, LANES), axis=-1, keepdims=True)
    el2 = jnp.where(lane == i1, NEG_INF, el)
    v2 = jnp.max(el2, axis=-1, keepdims=True)
    i2 = jnp.min(jnp.where(lane == i1, LANES, jnp.where(el2 == v2, lane, LANES)), axis=-1, keepdims=True)
    ex = jnp.exp(v2 - v1)
    w1 = 1.0 / (1.0 + ex)
    gate1 = p_g * w1
    gate2 = p_g * (ex * w1)
    e1 = i1 - N_GROUPS
    e2 = i2 - N_GROUPS

    carry = carry_scr[...]
    tri = tri_ref[...]
    ranks = []
    for e in (e1, e2):
        hit = lane == e
        oh = jnp.where(hit, 1.0, 0.0)
        before = jnp.dot(tri, oh.astype(BF16), preferred_element_type=F32) + carry
        ranks.append(jnp.sum(jnp.where(hit, before, 0.0), axis=-1, keepdims=True).astype(I32))
        carry = carry + jnp.sum(oh, axis=0, keepdims=True)
    carry_scr[...] = carry
    cnt_ref[...] = carry

    mi_ref[...] = jnp.where(lane == 0, e1, jnp.where(lane == 1, e2,
                            jnp.where(lane == 2, ranks[0], jnp.where(lane == 3, ranks[1], 0))))
    mf_ref[...] = jnp.where(lane == 0, gate1, jnp.where(lane == 1, gate2, 0.0))


def _mix(o_n, hc, ga, gb, x2, w_oa, w_co, b_co, w_out, ln_g, ln_b, w_r, b_r, *, tm):
    t, d = x2.shape
    assert d // 2 == ROW_CHUNKS * LANES
    row = lambda w: pl.BlockSpec((tm, w), lambda i: (i, 0))
    full = lambda a: pl.BlockSpec(a.shape, lambda i: (0,) * a.ndim, pipeline_mode=pl.Buffered(1))
    tri = (jnp.arange(tm)[:, None] > jnp.arange(tm)[None, :]).astype(BF16)
    b_co, ln_g, ln_b = b_co.reshape(1, d), ln_g.reshape(1, d), ln_b.reshape(1, d)
    return pl.pallas_call(
        functools.partial(_mix_kernel, tm=tm),
        grid=(t // tm,),
        in_specs=[row(d), row(d), row(d), row(d), row(d),
                  full(w_oa), full(w_co), full(b_co), full(w_out), full(ln_g), full(ln_b),
                  full(w_r), full(b_r), full(tri)],
        out_specs=[row(d), pl.BlockSpec((tm * ROW_CHUNKS, LANES), lambda i: (i, 0)), row(LANES), row(LANES),
                   pl.BlockSpec((1, LANES), lambda i: (0, 0))],
        out_shape=[jax.ShapeDtypeStruct((t, d), F32),
                   jax.ShapeDtypeStruct((t * ROW_CHUNKS, LANES), U32),
                   jax.ShapeDtypeStruct((t, LANES), I32),
                   jax.ShapeDtypeStruct((t, LANES), F32),
                   jax.ShapeDtypeStruct((1, LANES), F32)],
        scratch_shapes=[pltpu.VMEM((1, LANES), F32)],
        compiler_params=_cparams(1),
        name="mix",
    )(o_n, hc, ga, gb, x2, w_oa, w_co, b_co, w_out, ln_g, ln_b, w_r, b_r, tri)


def _row_copy(src, src_row, dst, dst_row, sem):
    def piece(row):
        return pl.ds(pl.multiple_of(row * ROW_CHUNKS, ROW_CHUNKS), ROW_CHUNKS)
    return pltpu.make_async_copy(src.at[piece(src_row), :], dst.at[piece(dst_row), :], sem)


def _dispatch_kernel(dest_ref, pad_ends_ref, padded_ref, n_used_ref, x_ref, xs_ref, zero_scr, sem, zsem,
                     *, tm, blk_rows, n_blocks):
    base = pl.program_id(0) * tm

    @pl.when(pl.program_id(0) == 0)
    def _zero_unowned_rows():
        zero_scr[...] = jnp.zeros(zero_scr.shape, zero_scr.dtype)
        piece = blk_rows * ROW_CHUNKS

        def fills():
            for e in range(N_EXPERTS):
                start = pl.multiple_of((pad_ends_ref[e] - blk_rows) * ROW_CHUNKS, SUBLANES)
                yield padded_ref[e] > 0, pltpu.make_async_copy(zero_scr, xs_ref.at[pl.ds(start, piece), :], zsem)
            for blk in range(n_blocks):
                yield blk >= n_used_ref[0], pltpu.make_async_copy(
                    zero_scr, xs_ref.at[pl.ds(blk * piece, piece), :], zsem)

        for cond, cp in fills():
            pl.when(cond)(cp.start)
        for cond, cp in fills():
            pl.when(cond)(cp.wait)

    def issue(g, carry):
        for u in range(ISSUE_UNROLL):
            r = g * ISSUE_UNROLL + u
            for c in range(TOP_K):
                _row_copy(x_ref, r, xs_ref, dest_ref[TOP_K * (base + r) + c], sem).start(priority=c)
        return carry

    lax.fori_loop(0, tm // ISSUE_UNROLL, issue, 0)
    for _ in range(TOP_K):
        pltpu.make_async_copy(x_ref, xs_ref.at[pl.ds(0, tm * ROW_CHUNKS), :], sem).wait()


def _dispatch(dest_flat, pad_ends, padded, n_used, x1p, n_rows, *, tm, blk_rows):
    w = x1p.shape[1]
    t = x1p.shape[0] // ROW_CHUNKS
    assert tm % ISSUE_UNROLL == 0 and n_rows % blk_rows == 0
    kernel = functools.partial(_dispatch_kernel, tm=tm, blk_rows=blk_rows, n_blocks=n_rows // blk_rows)
    return pl.pallas_call(
        kernel,
        grid_spec=pltpu.PrefetchScalarGridSpec(
            num_scalar_prefetch=4,
            grid=(t // tm,),
            in_specs=[pl.BlockSpec((tm * ROW_CHUNKS, w), lambda i, *_: (i, 0))],
            out_specs=pl.BlockSpec(memory_space=pl.ANY),
            scratch_shapes=[pltpu.VMEM((blk_rows * ROW_CHUNKS, w), U32),
                            pltpu.SemaphoreType.DMA(()), pltpu.SemaphoreType.DMA(())]),
        out_shape=jax.ShapeDtypeStruct((n_rows * ROW_CHUNKS, w), U32),
        compiler_params=_cparams(1),
        name="dispatch",
    )(dest_flat, pad_ends.astype(I32), padded.astype(I32), n_used, x1p)


def _expert_kernel(blk_e_ref, n_used_ref, next_e_ref, run_ref, xs_ref, wg_hbm, wu_hbm, wd_hbm, y_ref,
                   wg_f, wu_f, wd_f, wg_b, wu_b, wd_b, sems):
    i = pl.program_id(0)
    active = i < n_used_ref[0]
    tm = xs_ref.shape[0] // ROW_CHUNKS
    e = blk_e_ref[i]

    def weight_copies(expert, slot):
        return [pltpu.make_async_copy(hbm.at[expert], buf.at[slot], sems.at[slot, n])
                for n, (hbm, buf) in enumerate(((wg_hbm, wg_f), (wu_hbm, wu_f), (wd_hbm, wd_f)))]

    @pl.when(jnp.logical_not(active))
    def _unused_block():
        y_ref[...] = jnp.zeros(y_ref.shape, y_ref.dtype)

    @pl.when(i == 0)
    def _first_fetch():
        for cp in weight_copies(e, 0):
            cp.start()

    @pl.when(active & ((i == 0) | (e != blk_e_ref[jnp.maximum(i - 1, 0)])))
    def _new_expert():
        slot = run_ref[e] % 2
        for cp in weight_copies(e, slot):
            cp.wait()
        wg_b[...] = wg_f[slot].astype(BF16)
        wu_b[...] = wu_f[slot].astype(BF16)
        wd_b[...] = wd_f[slot].astype(BF16)

        @pl.when(next_e_ref[e] >= 0)
        def _prefetch_next():
            for cp in weight_copies(next_e_ref[e], 1 - slot):
                cp.start()

    @pl.when(active)
    def _():
        xb = _unpack_halves(_load_row_chunks(xs_ref, (), tm)).astype(BF16)
        g = jnp.dot(xb, wg_b[...], preferred_element_type=F32)
        u = jnp.dot(xb, wu_b[...], preferred_element_type=F32)
        hid = (g * _sigmoid(g) * u).astype(BF16)
        y = jnp.dot(hid, wd_b[...], preferred_element_type=F32)
        _store_row_chunks(y_ref, (), _pack_halves(y), tm)


def _experts(blk_e, n_used, next_e, run_idx, xs, w_gate, w_up, w_down, *, tm):
    w = xs.shape[1]
    n_rows = xs.shape[0] // ROW_CHUNKS
    _, d, de = w_gate.shape
    row_map = lambda i, blk_e, n_used, next_e, run_idx: (jnp.minimum(i, n_used[0] - 1), 0)
    hbm = pl.BlockSpec(memory_space=pl.ANY)
    return pl.pallas_call(
        _expert_kernel,
        grid_spec=pltpu.PrefetchScalarGridSpec(
            num_scalar_prefetch=4,
            grid=(n_rows // tm,),
            in_specs=[pl.BlockSpec((tm * ROW_CHUNKS, w), row_map), hbm, hbm, hbm],
            out_specs=pl.BlockSpec((tm * ROW_CHUNKS, w), lambda i, blk_e, n_used, next_e, run_idx: (i, 0)),
            scratch_shapes=[pltpu.VMEM((2, d, de), F32), pltpu.VMEM((2, d, de), F32), pltpu.VMEM((2, de, d), F32),
                            pltpu.VMEM((d, de), BF16), pltpu.VMEM((d, de), BF16), pltpu.VMEM((de, d), BF16),
                            pltpu.SemaphoreType.DMA((2, 3))]),
        out_shape=jax.ShapeDtypeStruct((n_rows * ROW_CHUNKS, w), U32),
        compiler_params=_cparams(1),
        name="experts",
    )(blk_e, n_used, next_e, run_idx, xs, w_gate, w_up, w_down)


def _combine_kernel(dest_ref, x1_ref, mf_ref, g_ref, b_ref, yp_ref, o_ref, buf, sems, *, tm, n_tiles):
    i = pl.program_id(0)

    def issue(tile, slot):
        base = tile * tm

        def body(g, carry):
            for u in range(ISSUE_UNROLL):
                r = g * ISSUE_UNROLL + u
                for c in range(TOP_K):
                    _row_copy(yp_ref, dest_ref[TOP_K * (base + r) + c], buf.at[slot, c], r,
                              sems.at[slot]).start(priority=c)
            return carry

        lax.fori_loop(0, tm // ISSUE_UNROLL, body, 0)

    @pl.when(i == 0)
    def _first():
        issue(0, 0)

    @pl.when(i + 1 < n_tiles)
    def _next():
        issue(i + 1, (i + 1) % 2)

    slot = i % 2
    for c in range(TOP_K):
        pltpu.make_async_copy(yp_ref.at[pl.ds(0, tm * ROW_CHUNKS), :], buf.at[slot, c], sems.at[slot]).wait()

    gates = mf_ref[...]
    ffn = (gates[:, 0:1] * _unpack_halves(_load_row_chunks(buf, (slot, 0), tm))
           + gates[:, 1:2] * _unpack_halves(_load_row_chunks(buf, (slot, 1), tm)))
    o_ref[...] = _layer_norm(DN_ALPHA * x1_ref[...] + ffn, g_ref[...], b_ref[...])


def _combine(dest_flat, x1, mf, ln_g, ln_b, yp, *, tm):
    t, d = x1.shape
    n_tiles = t // tm
    assert tm % ISSUE_UNROLL == 0
    vec = pl.BlockSpec((1, d), lambda i, dest: (0, 0))
    return pl.pallas_call(
        functools.partial(_combine_kernel, tm=tm, n_tiles=n_tiles),
        grid_spec=pltpu.PrefetchScalarGridSpec(
            num_scalar_prefetch=1,
            grid=(n_tiles,),
            in_specs=[pl.BlockSpec((tm, d), lambda i, dest: (i, 0)),
                      pl.BlockSpec((tm, LANES), lambda i, dest: (i, 0)),
                      vec, vec,
                      pl.BlockSpec(memory_space=pl.ANY)],
            out_specs=pl.BlockSpec((tm, d), lambda i, dest: (i, 0)),
            scratch_shapes=[pltpu.VMEM((2, TOP_K, tm * ROW_CHUNKS, LANES), U32),
                            pltpu.SemaphoreType.DMA((2,))]),
        out_shape=jax.ShapeDtypeStruct((t, d), F32),
        compiler_params=_cparams(1),
        name="combine",
    )(dest_flat, x1, mf, ln_g.reshape(1, d), ln_b.reshape(1, d), yp)


def _tile(n, pref):
    return pref if n % pref == 0 else n


def kernel(x, w_in, b_in, diff_lambda, head_norm_g, w_o_attn, rel_bias, conv_w, conv_b, conv_ln_g,
           conv_ln_b, w_conv_out, b_conv_out, w_out, ln1_g, ln1_b, router_g_w, router_g_b,
           router_e_w, router_e_b, expert_w_gate, expert_w_up, expert_w_down, ln2_g, ln2_b):
    bsz, s, d = x.shape
    t = bsz * s
    width = N_HEADS * 2 * HEAD_DIM
    assert w_in.shape[0] == DEPTH and w_in.shape[2] == 7 * width and width == d
    assert conv_w.shape[2] == width and TOP_K == 2
    tm_moe = _tile(t, 512)

    for li in range(DEPTH):
        lam_init = 0.8 - 0.6 * math.exp(-0.3 * li)
        x2 = x.reshape(t, d)
        q, k, v, ga, gb, hc = _proj_conv(
            x2, w_in[li].astype(BF16), b_in[li].reshape(1, -1), conv_w[li], conv_b[li], conv_ln_g[li],
            conv_ln_b[li], width=width, tm=_tile(s, 512), chunk=_tile(width, 512),
            q_scale=HEAD_DIM ** -0.5 * LOG2E, ts=_tile(s, 256), rc=16, seq=s)
        o_n = _attention(q.reshape(bsz, s, width), k.reshape(bsz, s, width), v.reshape(bsz, s, width),
                         rel_bias, diff_lambda[li], head_norm_g[li], tk=256 if s % 512 == 0 else 128,
                         lam_init=lam_init)

        n_r = N_GROUPS + N_EXPERTS
        w_r = jnp.pad(jnp.concatenate([router_g_w[li], router_e_w[li]], axis=1),
                      ((0, 0), (0, LANES - n_r))).astype(BF16)
        b_r = jnp.pad(jnp.concatenate([router_g_b[li], router_e_b[li]]), (0, LANES - n_r)).reshape(1, LANES)
        x1, x1p, mi, mf, cnt = _mix(
            o_n.reshape(t, width), hc.reshape(t, width), ga, gb, x2,
            w_o_attn[li].astype(BF16), w_conv_out[li].astype(BF16), b_conv_out[li],
            w_out[li].astype(BF16), ln1_g[li], ln1_b[li], w_r, b_r, tm=_tile(t, 512))

        counts = cnt[0, :N_EXPERTS].astype(I32)
        padded = (counts + tm_moe - 1) // tm_moe * tm_moe
        pad_ends = jnp.cumsum(padded)
        pad_starts = pad_ends - padded
        chosen = mi[:, 0:TOP_K, None] == jnp.arange(N_EXPERTS, dtype=I32)[None, None, :]
        dest_flat = (jnp.sum(jnp.where(chosen, pad_starts[None, None, :], 0), axis=-1)
                     + mi[:, TOP_K:2 * TOP_K]).reshape(t * TOP_K)
        n_rows = t * TOP_K + N_EXPERTS * tm_moe
        n_blocks = n_rows // tm_moe
        blk_start = jnp.arange(n_blocks, dtype=I32) * tm_moe
        blk_e = jnp.minimum(jnp.sum(blk_start[:, None] >= pad_ends[None, :], axis=1),
                            N_EXPERTS - 1).astype(I32)
        n_used = (pad_ends[-1:] // tm_moe).astype(I32)
        has_rows = padded > 0
        later = lax.cummin(jnp.where(has_rows, jnp.arange(N_EXPERTS, dtype=I32), N_EXPERTS), reverse=True)
        next_e = jnp.concatenate([later[1:], jnp.full((1,), N_EXPERTS, I32)])
        next_e = jnp.where(next_e < N_EXPERTS, next_e, -1).astype(I32)
        run_idx = (jnp.cumsum(has_rows.astype(I32)) - 1).astype(I32)

        xs = _dispatch(dest_flat, pad_ends, padded, n_used, x1p, n_rows, tm=_tile(t, 512), blk_rows=tm_moe)
        yp = _experts(blk_e, n_used, next_e, run_idx, xs, expert_w_gate[li], expert_w_up[li], expert_w_down[li],
                      tm=tm_moe)
        x = _combine(dest_flat, x1, mf, ln2_g[li], ln2_b[li], yp, tm=_tile(t, 256)).reshape(bsz, s, d)
    return x
```

```python
import functools
import math

import jax
import jax.numpy as jnp
from jax import lax
from jax.experimental import pallas as pl
from jax.experimental.pallas import tpu as pltpu

F32 = jnp.float32
BF16 = jnp.bfloat16
U32 = jnp.uint32
I32 = jnp.int32

N_HEADS = 8
HEAD_DIM = 64
CONV_TAPS = 31
REL_BUCKETS = 32
REL_MAX_DIST = 128
N_GROUPS = 4
EXPERTS_PER_GROUP = 8
N_EXPERTS = N_GROUPS * EXPERTS_PER_GROUP
TOP_K = 2
DEPTH = 1
DN_ALPHA = (2.0 * DEPTH) ** 0.25
LN_EPS = 1e-5
NEG_INF = -1e30
LOG2E = 1.4426950408889634

LANES = 128
SUBLANES = 8
VMEM_LIMIT = 56 * 1024 * 1024
CONV_HALO = 32
CONV_SHIFT_ROWS = 56
PROJ_DOTS_PER_CONV_CHUNK = 2
HEADS_PER_STEP = 4
ONES_ROWS = 16
ROW_CHUNKS = 4
ISSUE_UNROLL = 8


def _cparams(n_axes):
    return pltpu.CompilerParams(dimension_semantics=("arbitrary",) * n_axes, vmem_limit_bytes=VMEM_LIMIT)


def _sigmoid(x):
    return 1.0 / (1.0 + jnp.exp(-x))


def _layer_norm(z, g, b):
    mu = jnp.mean(z, axis=-1, keepdims=True)
    zc = z - mu
    var = jnp.mean(zc * zc, axis=-1, keepdims=True)
    return zc * lax.rsqrt(var + LN_EPS) * g + b


def _pack_halves(y):
    n = y.shape[1] // 2
    bits = pltpu.bitcast(y.astype(BF16).astype(F32), U32)
    return (bits[:, :n] >> 16) | (bits[:, n:] & jnp.uint32(0xFFFF0000))


def _unpack_halves(w):
    lo = pltpu.bitcast(w << 16, F32)
    hi = pltpu.bitcast(w & jnp.uint32(0xFFFF0000), F32)
    return jnp.concatenate([lo, hi], axis=1)


def _store_row_chunks(ref, prefix, packed, m):
    for q in range(ROW_CHUNKS):
        ref[prefix + (pl.ds(q, m, stride=ROW_CHUNKS), slice(None))] = packed[:, q * LANES:(q + 1) * LANES]


def _load_row_chunks(ref, prefix, m):
    return jnp.concatenate([ref[prefix + (pl.ds(q, m, stride=ROW_CHUNKS), slice(None))]
                            for q in range(ROW_CHUNKS)], axis=1)


def _zero_after(value):
    bits = pltpu.bitcast(value[0:SUBLANES, 0:LANES], U32)
    return pltpu.bitcast((bits >> 16) >> 16, F32)[0:1, :]


def _conv_rows(sh, w_ref, b_ref, g_ref, bb_ref, o_ref, out_row0, r0, rc, anchor):
    off = CONV_HALO - (CONV_TAPS - 1)
    acc = None
    for t in range(CONV_TAPS):
        k, base = (off + t) % SUBLANES, (off + t) // SUBLANES * SUBLANES
        slab = sh[k, r0 + base:r0 + base + rc, :].reshape(rc // SUBLANES, SUBLANES, -1)
        term = slab * w_ref[t][None]
        acc = term if acc is None else acc + term
    bias = b_ref[...]
    if anchor is not None:
        bias = bias + jnp.tile(anchor, (1, bias.shape[1] // LANES))
    y = _layer_norm(acc.reshape(rc, -1) + bias, g_ref[...], bb_ref[...])
    o_ref[out_row0 + r0:out_row0 + r0 + rc, :] = (y * _sigmoid(y)).astype(o_ref.dtype)


def _proj_conv_kernel(x_ref, w_ref, b_ref, cw_ref, cb_ref, cg_ref, cbb_ref,
                      q_ref, k_ref, v_ref, ga_ref, gb_ref, hc_ref, sh, halo_scr,
                      *, width, chunk, q_scale, tm, ts, rc, tiles_per_seq):
    i = pl.program_id(0)
    xb = x_ref[...].astype(BF16)

    def lin(col0):
        return (jnp.dot(xb, w_ref[:, col0:col0 + chunk], preferred_element_type=F32)
                + b_ref[:, col0:col0 + chunk])

    glu = jnp.concatenate([(lin(3 * width + c) * _sigmoid(lin(4 * width + c))).astype(BF16).astype(F32)
                           for c in range(0, width, chunk)], axis=1)

    n_ext = CONV_HALO + ts
    n_sh = n_ext - SUBLANES

    def load_half(half):
        if half == 0:
            prev = halo_scr[...]
            sh[0, 0:CONV_HALO, :] = jnp.where(i % tiles_per_seq != 0, prev, jnp.zeros_like(prev))
        else:
            sh[0, 0:CONV_HALO, :] = glu[half * ts - CONV_HALO:half * ts]
        sh[0, CONV_HALO:n_ext, :] = glu[half * ts:(half + 1) * ts]
        for k in range(1, SUBLANES):
            for r0 in range(0, n_sh, CONV_SHIFT_ROWS):
                n = min(CONV_SHIFT_ROWS, n_sh - r0)
                sh[k, r0:r0 + n, :] = sh[0, r0 + k:r0 + k + n, :]

    def epilogue(ref, group, c, fn):
        def run():
            z = lin(group * width + c)
            ref[:, c:c + chunk] = fn(z).astype(BF16)
            return _zero_after(z)
        return run

    others = []
    for c in range(0, width, chunk):
        others += [epilogue(q_ref, 0, c, lambda z: z * q_scale), epilogue(k_ref, 1, c, lambda z: z),
                   epilogue(v_ref, 2, c, lambda z: z), epilogue(ga_ref, 5, c, _sigmoid),
                   epilogue(gb_ref, 6, c, _sigmoid)]
    conv_chunks = [(half, r0) for half in range(tm // ts) for r0 in range(0, ts, rc)]
    anchor = None
    for half, r0 in conv_chunks:
        if r0 == 0:
            load_half(half)
        _conv_rows(sh, cw_ref, cb_ref, cg_ref, cbb_ref, hc_ref, half * ts, r0, rc, anchor)
        for _ in range(PROJ_DOTS_PER_CONV_CHUNK):
            if others:
                anchor = others.pop(0)()
    for run in others:
        run()
    halo_scr[...] = glu[tm - CONV_HALO:tm]


def _proj_conv(x2, w_in, b_in, conv_w, conv_b, ln_g, ln_b, *, width, tm, chunk, q_scale, ts, rc, seq):
    t, d = x2.shape
    n_cols = w_in.shape[1]
    c = width
    assert seq % tm == 0 and tm % ts == 0 and ts % rc == 0 and CONV_HALO >= CONV_TAPS - 1 and ts >= CONV_HALO
    out = jax.ShapeDtypeStruct((t, width), BF16)
    row_spec = pl.BlockSpec((tm, width), lambda i: (i, 0))
    const = lambda shape: pl.BlockSpec(shape, lambda i: (0,) * len(shape), pipeline_mode=pl.Buffered(1))
    kernel = functools.partial(_proj_conv_kernel, width=width, chunk=chunk, q_scale=q_scale, tm=tm, ts=ts,
                               rc=rc, tiles_per_seq=seq // tm)
    return pl.pallas_call(
        kernel,
        grid=(t // tm,),
        in_specs=[pl.BlockSpec((tm, d), lambda i: (i, 0)),
                  const((d, n_cols)), const((1, n_cols)),
                  const((CONV_TAPS, SUBLANES, c)), const((1, c)), const((1, c)), const((1, c))],
        out_specs=[row_spec] * 6,
        out_shape=[out] * 6,
        scratch_shapes=[pltpu.VMEM((SUBLANES, CONV_HALO + ts, c), F32), pltpu.VMEM((CONV_HALO, c), F32)],
        compiler_params=_cparams(1),
        name="proj_conv",
    )(x2, w_in, b_in, jnp.broadcast_to(conv_w[:, None, :], (CONV_TAPS, SUBLANES, c)),
      conv_b.reshape(1, c), ln_g.reshape(1, c), ln_b.reshape(1, c))


def _sublane_all(x, op):
    for shift in (4, 2, 1):
        x = op(x, pltpu.roll(x, shift, 0))
    return x


def _attn_kernel(tab_ref, bucket_ref, lam_ref, q_ref, k_ref, v_ref, g_ref, o_ref,
                 bias_scr, qz_scr, vt_scr, s0_scr, s1_scr, p0_scr, p1_scr, a0_scr, a1_scr,
                 m_scr, acc_scr, *, tq, tk, dh, lam_init):
    hp = pl.program_id(0)
    b = pl.program_id(1)
    qi = pl.program_id(2)
    hw = 2 * dh
    acc_rows = hw + ONES_ROWS
    n_kv = vt_scr.shape[1]
    sw = tk
    n_sub = 2 * tq // sw
    heads = range(HEADS_PER_STEP)

    @pl.when((b == 0) & (qi == 0))
    def _build_bias():
        for hh in heads:
            h = hp * HEADS_PER_STEP + hh
            far = tab_ref[REL_BUCKETS - 1, h]
            for t in range(3):
                for c in range(tq // sw):
                    bk = bucket_ref[t, :, c * sw:(c + 1) * sw]
                    acc = jnp.full((tk, sw), NEG_INF, F32)
                    for r in range(REL_BUCKETS):
                        acc = jnp.where(bk == r, (tab_ref[r, h] - far) * LOG2E, acc)
                    bias_scr[hh, t, :, c * sw:(c + 1) * sw] = acc

    @pl.when(qi == 0)
    def _transpose_values():
        for hh in heads:
            for jj in range(n_kv):
                vt_scr[hh, jj, 0:hw, :] = (v_ref[0, jj * tk:(jj + 1) * tk, hh * hw:(hh + 1) * hw]
                                           .astype(F32).T.astype(BF16))
                vt_scr[hh, jj, hw:hw + ONES_ROWS, :] = jnp.ones((ONES_ROWS, tk), BF16)

    lane = lax.broadcasted_iota(I32, (tq, hw), 1)
    for hh in heads:
        q = q_ref[0, :, hh * hw:(hh + 1) * hw]
        zero = jnp.zeros_like(q)
        qz_scr[hh, 0:tq, :] = jnp.where(lane < dh, q, zero)
        qz_scr[hh, tq:2 * tq, :] = jnp.where(lane >= dh, q, zero)
    m_scr[...] = jnp.full(m_scr.shape, NEG_INF, F32)
    acc_scr[...] = jnp.zeros(acc_scr.shape, F32)

    s_bufs, p_bufs, a_bufs = (s0_scr, s1_scr), (p0_scr, p1_scr), (a0_scr, a1_scr)
    p1_scr[...] = jnp.zeros(p1_scr.shape, BF16)
    a1_scr[...] = jnp.ones(a1_scr.shape, F32)

    all_cols = (slice(0, 2 * tq),)
    late_cols = (slice(tk, tq), slice(tq + tk, 2 * tq))

    def issue_scores(j, slot, cols=all_cols):
        rows = (slice(j * tk, (j + 1) * tk) if isinstance(j, int)
                else pl.ds(pl.multiple_of(j * tk, tk), tk))
        for hh in heads:
            kj = k_ref[0, rows, hh * hw:(hh + 1) * hw]
            for cs in cols:
                s_bufs[slot][hh, :, cs] = lax.dot_general(kj, qz_scr[hh, cs, :], (((1,), (1,)), ((), ())),
                                                          preferred_element_type=F32)

    def accumulate(hh, j_prev, slot_prev, cs=slice(0, 2 * tq)):
        n = cs.stop - cs.start
        pv = jnp.dot(vt_scr[hh, j_prev], p_bufs[slot_prev][hh, :, cs], preferred_element_type=F32)
        acc3 = acc_scr[hh, :, cs].reshape(acc_rows // SUBLANES, SUBLANES, n)
        return (a_bufs[slot_prev][hh, :, cs][None] * acc3).reshape(acc_rows, n) + pv

    def step(j, bias_idx, slot, issue_next=True, next_is_last=False, last=False):
        if issue_next:
            issue_scores(j + 1, 1 - slot, late_cols if next_is_last else all_cols)
        acc_new = [accumulate(hh, jnp.maximum(j - 1, 0), 1 - slot) for hh in heads]
        for hh in heads:
            for c in range(n_sub):
                cs = slice(c * sw, (c + 1) * sw)
                qs = (c * sw) % tq
                if last and qs < tk:
                    continue
                s = s_bufs[slot][hh, :, cs]
                if bias_idx is not None:
                    s = s + bias_scr[hh, bias_idx, :, qs:qs + sw]
                s3 = s.reshape(tk // SUBLANES, SUBLANES, sw)
                m_prev = m_scr[hh, :, cs]
                m_new = jnp.maximum(m_prev, _sublane_all(jnp.max(s3, axis=0), jnp.maximum))
                p3 = jnp.exp2(s3 - m_new[None])
                p_bufs[slot][hh, :, cs] = p3.reshape(tk, sw).astype(BF16)
                a_bufs[slot][hh, :, cs] = jnp.exp2(m_prev - m_new)
                m_scr[hh, :, cs] = m_new
        for hh in heads:
            acc_scr[hh] = acc_new[hh]

    issue_scores(0, 0)

    @pl.when(qi == 0)
    def _first_query_tile():
        step(0, 1, 0, next_is_last=True)
        step(1, 2, 1, issue_next=False, last=True)

    @pl.when(qi >= 1)
    def _later_query_tiles():
        def far_pair(i, carry):
            step(2 * i, None, 0)
            step(2 * i + 1, None, 1)
            return carry

        lax.fori_loop(0, qi - 1, far_pair, 0)
        step(2 * qi - 2, None, 0)
        step(2 * qi - 1, 0, 1)
        step(2 * qi, 1, 0, next_is_last=True)
        step(2 * qi + 1, 2, 1, issue_next=False, last=True)

    drained = [[accumulate(hh, 2 * qi + 1, 1, cs) for cs in late_cols] for hh in heads]
    for hh in heads:
        for cs, val in zip(late_cols, drained[hh]):
            acc_scr[hh, :, cs] = val

    lv = lam_ref[...]
    lam = (jnp.exp(jnp.sum(lv[0:1] * lv[1:2], axis=-1, keepdims=True))
           - jnp.exp(jnp.sum(lv[2:3] * lv[3:4], axis=-1, keepdims=True)) + lam_init)
    gain = g_ref[...] * (1.0 - lam_init)
    for hh in heads:
        acc_all = acc_scr[hh].reshape(acc_rows // SUBLANES, SUBLANES, 2 * tq)
        inv_l = 1.0 / acc_all[hw // SUBLANES]
        acc3 = acc_all[0:hw // SUBLANES]
        o3 = acc3[:, :, 0:tq] * inv_l[None, :, 0:tq] - lam * (acc3[:, :, tq:2 * tq] * inv_l[None, :, tq:2 * tq])
        ms = _sublane_all(jnp.sum(o3 * o3, axis=0), jnp.add) * (1.0 / hw)
        y = (o3 * lax.rsqrt(ms + LN_EPS)[None]).reshape(hw, tq) * gain
        o_ref[0, :, hh * hw:(hh + 1) * hw] = y.T.astype(o_ref.dtype)


def _rel_bucket(dist):
    max_exact = REL_BUCKETS // 2
    d = jnp.maximum(dist, 1).astype(F32)
    large = max_exact + (jnp.log(d / max_exact) / math.log(REL_MAX_DIST / max_exact)
                         * (REL_BUCKETS - max_exact)).astype(I32)
    large = jnp.minimum(large, REL_BUCKETS - 1)
    return jnp.where(dist < max_exact, dist, large)


def _near_buckets(tq, tk):
    tiles = []
    for rel_tile in (-1, 0, 1):
        dist = (jnp.arange(tq, dtype=I32)[None, :]
                - (rel_tile * tk + jnp.arange(tk, dtype=I32))[:, None])
        tiles.append(jnp.where(dist >= 0, _rel_bucket(jnp.maximum(dist, 0)), -1))
    return jnp.stack(tiles).astype(I32)


def _attention(q, k, v, rel_bias, diff_lambda, head_norm_g, *, tk, lam_init):
    bsz, s, width = q.shape
    hw = width // N_HEADS
    tq = 2 * tk
    hps, pw = HEADS_PER_STEP, HEADS_PER_STEP * hw
    assert hw == 2 * HEAD_DIM and hw % LANES == 0 and s % tq == 0 and N_HEADS % hps == 0
    assert tk + 1 >= REL_MAX_DIST
    kernel = functools.partial(_attn_kernel, tq=tq, tk=tk, dh=HEAD_DIM, lam_init=lam_init)
    stat = pltpu.VMEM((hps, SUBLANES, 2 * tq), F32)
    return pl.pallas_call(
        kernel,
        grid=(N_HEADS // hps, bsz, s // tq),
        in_specs=[pl.BlockSpec(memory_space=pltpu.SMEM),
                  pl.BlockSpec((3, tk, tq), lambda h, b, i: (0, 0, 0)),
                  pl.BlockSpec((4, HEAD_DIM), lambda h, b, i: (0, 0)),
                  pl.BlockSpec((1, tq, pw), lambda h, b, i: (b, i, h)),
                  pl.BlockSpec((1, s, pw), lambda h, b, i: (b, 0, h)),
                  pl.BlockSpec((1, s, pw), lambda h, b, i: (b, 0, h)),
                  pl.BlockSpec((hw, 1), lambda h, b, i: (0, 0))],
        out_specs=pl.BlockSpec((1, tq, pw), lambda h, b, i: (b, i, h)),
        out_shape=jax.ShapeDtypeStruct((bsz, s, width), BF16),
        scratch_shapes=[pltpu.VMEM((hps, 3, tk, tq), F32),
                        pltpu.VMEM((hps, 2 * tq, hw), BF16),
                        pltpu.VMEM((hps, s // tk, hw + ONES_ROWS, tk), BF16),
                        pltpu.VMEM((hps, tk, 2 * tq), F32),
                        pltpu.VMEM((hps, tk, 2 * tq), F32),
                        pltpu.VMEM((hps, tk, 2 * tq), BF16),
                        pltpu.VMEM((hps, tk, 2 * tq), BF16),
                        stat, stat,
                        stat,
                        pltpu.VMEM((hps, hw + ONES_ROWS, 2 * tq), F32)],
        compiler_params=_cparams(3),
        name="attn",
    )(rel_bias, _near_buckets(tq, tk), diff_lambda, q, k, v, head_norm_g.reshape(hw, 1))


def _mix_kernel(o_ref, hc_ref, ga_ref, gb_ref, x_ref, woa_ref, wco_ref, bco_ref, wout_ref,
                g1_ref, b1_ref, wr_ref, br_ref, tri_ref,
                x1_ref, x1p_ref, mi_ref, mf_ref, cnt_ref, carry_scr, *, tm):
    i = pl.program_id(0)

    @pl.when(i == 0)
    def _init():
        carry_scr[...] = jnp.zeros(carry_scr.shape, F32)

    halves = [slice(0, tm // 2), slice(tm // 2, tm)]
    branch = [(jnp.dot(o_ref[h, :], woa_ref[...], preferred_element_type=F32),
               jnp.dot(hc_ref[h, :], wco_ref[...], preferred_element_type=F32) + bco_ref[...]) for h in halves]
    mixed = [jnp.dot((ga_ref[h, :].astype(F32) * y_a + gb_ref[h, :].astype(F32) * y_b).astype(BF16),
                     wout_ref[...], preferred_element_type=F32) for h, (y_a, y_b) in zip(halves, branch)]
    x1_halves = [_layer_norm(DN_ALPHA * x_ref[h, :] + mx, g1_ref[...], b1_ref[...]) for h, mx in zip(halves, mixed)]
    logits = jnp.concatenate([jnp.dot(xh.astype(BF16), wr_ref[...], preferred_element_type=F32)
                              for xh in x1_halves], axis=0) + br_ref[...]
    x1 = jnp.concatenate(x1_halves, axis=0)
    x1_ref[...] = x1
    _store_row_chunks(x1p_ref, (), _pack_halves(x1), tm)

    lane = lax.broadcasted_iota(I32, (tm, LANES), 1)
    g_lane = lane < N_GROUPS
    gl = jnp.where(g_lane, logits, NEG_INF)
    g_max = jnp.max(gl, axis=-1, keepdims=True)
    g_idx = jnp.min(jnp.where(gl == g_max, lane, LANES), axis=-1, keepdims=True)
    p_g = 1.0 / jnp.sum(jnp.where(g_lane, jnp.exp(gl - g_max), 0.0), axis=-1, keepdims=True)
    lo = N_GROUPS + g_idx * EXPERTS_PER_GROUP
    el = jnp.where(lane >= lo, jnp.where(lane < lo + EXPERTS_PER_GROUP, logits, NEG_INF), NEG_INF)
    v1 = jnp.max(el, axis=-1, keepdims=True)
    i1 = jnp.min(jnp.where(el == v1, lane, LANES), axis=-1, keepdims=True)
    el2 = jnp.where(lane == i1, NEG_INF, el)
    v2 = jnp.max(el2, axis=-1, keepdims=True)
    i2 = jnp.min(jnp.where(lane == i1, LANES, jnp.where(el2 == v2, lane, LANES)), axis=-1, keepdims=True)
    ex = jnp.exp(v2 - v1)
    w1 = 1.0 / (1.0 + ex)
    gate1 = p_g * w1
    gate2 = p_g * (ex * w1)
    e1 = i1 - N_GROUPS
    e2 = i2 - N_GROUPS

    carry = carry_scr[...]
    tri = tri_ref[...]
    ranks = []
    for e in (e1, e2):
        hit = lane == e
        oh = jnp.where(hit, 1.0, 0.0)
        before = jnp.dot(tri, oh.astype(BF16), preferred_element_type=F32) + carry
        ranks.append(jnp.sum(jnp.where(hit, before, 0.0), axis=-1, keepdims=True).astype(I32))
        carry = carry + jnp.sum(oh, axis=0, keepdims=True)
    carry_scr[...] = carry
    cnt_ref[...] = carry

    mi_ref[...] = jnp.where(lane == 0, e1, jnp.where(lane == 1, e2,
                            jnp.where(lane == 2, ranks[0], jnp.where(lane == 3, ranks[1], 0))))
    mf_ref[...] = jnp.where(lane == 0, gate1, jnp.where(lane == 1, gate2, 0.0))


def _mix(o_n, hc, ga, gb, x2, w_oa, w_co, b_co, w_out, ln_g, ln_b, w_r, b_r, *, tm):
    t, d = x2.shape
    assert d // 2 == ROW_CHUNKS * LANES
    row = lambda w: pl.BlockSpec((tm, w), lambda i: (i, 0))
    full = lambda a: pl.BlockSpec(a.shape, lambda i: (0,) * a.ndim, pipeline_mode=pl.Buffered(1))
    tri = (jnp.arange(tm)[:, None] > jnp.arange(tm)[None, :]).astype(BF16)
    b_co, ln_g, ln_b = b_co.reshape(1, d), ln_g.reshape(1, d), ln_b.reshape(1, d)
    return pl.pallas_call(
        functools.partial(_mix_kernel, tm=tm),
        grid=(t // tm,),
        in_specs=[row(d), row(d), row(d), row(d), row(d),
                  full(w_oa), full(w_co), full(b_co), full(w_out), full(ln_g), full(ln_b),
                  full(w_r), full(b_r), full(tri)],
        out_specs=[row(d), pl.BlockSpec((tm * ROW_CHUNKS, LANES), lambda i: (i, 0)), row(LANES), row(LANES),
                   pl.BlockSpec((1, LANES), lambda i: (0, 0))],
        out_shape=[jax.ShapeDtypeStruct((t, d), F32),
                   jax.ShapeDtypeStruct((t * ROW_CHUNKS, LANES), U32),
                   jax.ShapeDtypeStruct((t, LANES), I32),
                   jax.ShapeDtypeStruct((t, LANES), F32),
                   jax.ShapeDtypeStruct((1, LANES), F32)],
        scratch_shapes=[pltpu.VMEM((1, LANES), F32)],
        compiler_params=_cparams(1),
        name="mix",
    )(o_n, hc, ga, gb, x2, w_oa, w_co, b_co, w_out, ln_g, ln_b, w_r, b_r, tri)


def _row_copy(src, src_row, dst, dst_row, sem):
    def piece(row):
        return pl.ds(pl.multiple_of(row * ROW_CHUNKS, ROW_CHUNKS), ROW_CHUNKS)
    return pltpu.make_async_copy(src.at[piece(src_row), :], dst.at[piece(dst_row), :], sem)


def _dispatch_kernel(dest_ref, pad_ends_ref, padded_ref, n_used_ref, x_ref, xs_ref, zero_scr, sem, zsem,
                     *, tm, blk_rows, n_blocks):
    base = pl.program_id(0) * tm

    piece = blk_rows * ROW_CHUNKS

    def expert_fills():
        out = []
        for e in range(N_EXPERTS):
            start = pl.multiple_of((pad_ends_ref[e] - blk_rows) * ROW_CHUNKS, SUBLANES)
            out.append((padded_ref[e] > 0,
                        pltpu.make_async_copy(zero_scr, xs_ref.at[pl.ds(start, piece), :], zsem.at[0])))
        return out

    def tail_fills():
        return [(blk >= n_used_ref[0],
                 pltpu.make_async_copy(zero_scr, xs_ref.at[pl.ds(blk * piece, piece), :], zsem.at[1]))
                for blk in range(n_blocks)]

    @pl.when(pl.program_id(0) == 0)
    def _zero_unowned_rows():
        zero_scr[...] = jnp.zeros(zero_scr.shape, zero_scr.dtype)
        for cond, cp in expert_fills() + tail_fills():
            pl.when(cond)(cp.start)
        for cond, cp in expert_fills():
            pl.when(cond)(cp.wait)

    def issue(g, carry):
        for u in range(ISSUE_UNROLL):
            r = g * ISSUE_UNROLL + u
            for c in range(TOP_K):
                _row_copy(x_ref, r, xs_ref, dest_ref[TOP_K * (base + r) + c], sem).start(priority=c)
        return carry

    lax.fori_loop(0, tm // ISSUE_UNROLL, issue, 0)
    for _ in range(TOP_K):
        pltpu.make_async_copy(x_ref, xs_ref.at[pl.ds(0, tm * ROW_CHUNKS), :], sem).wait()

    @pl.when(pl.program_id(0) == pl.num_programs(0) - 1)
    def _drain_tail_fills():
        for cond, cp in tail_fills():
            pl.when(cond)(cp.wait)


def _dispatch(dest_flat, pad_ends, padded, n_used, x1p, n_rows, *, tm, blk_rows):
    w = x1p.shape[1]
    t = x1p.shape[0] // ROW_CHUNKS
    assert tm % ISSUE_UNROLL == 0 and n_rows % blk_rows == 0
    kernel = functools.partial(_dispatch_kernel, tm=tm, blk_rows=blk_rows, n_blocks=n_rows // blk_rows)
    return pl.pallas_call(
        kernel,
        grid_spec=pltpu.PrefetchScalarGridSpec(
            num_scalar_prefetch=4,
            grid=(t // tm,),
            in_specs=[pl.BlockSpec((tm * ROW_CHUNKS, w), lambda i, *_: (i, 0))],
            out_specs=pl.BlockSpec(memory_space=pl.ANY),
            scratch_shapes=[pltpu.VMEM((blk_rows * ROW_CHUNKS, w), U32),
                            pltpu.SemaphoreType.DMA(()), pltpu.SemaphoreType.DMA((2,))]),
        out_shape=jax.ShapeDtypeStruct((n_rows * ROW_CHUNKS, w), U32),
        compiler_params=_cparams(1),
        name="dispatch",
    )(dest_flat, pad_ends.astype(I32), padded.astype(I32), n_used, x1p)


def _expert_kernel(blk_e_ref, n_used_ref, next_e_ref, run_ref, xs_ref, wg_hbm, wu_hbm, wd_hbm, y_ref,
                   wg_f, wu_f, wd_f, wg_b, wu_b, wd_b, sems):
    i = pl.program_id(0)
    active = i < n_used_ref[0]
    tm = xs_ref.shape[0] // ROW_CHUNKS
    e = blk_e_ref[i]

    def weight_copies(expert, slot):
        return [pltpu.make_async_copy(hbm.at[expert], buf.at[slot], sems.at[slot, n])
                for n, (hbm, buf) in enumerate(((wg_hbm, wg_f), (wu_hbm, wu_f), (wd_hbm, wd_f)))]

    @pl.when(jnp.logical_not(active))
    def _unused_block():
        y_ref[...] = jnp.zeros(y_ref.shape, y_ref.dtype)

    @pl.when(i == 0)
    def _first_fetch():
        for cp in weight_copies(e, 0):
            cp.start()

    @pl.when(active & ((i == 0) | (e != blk_e_ref[jnp.maximum(i - 1, 0)])))
    def _new_expert():
        slot = run_ref[e] % 2
        for cp in weight_copies(e, slot):
            cp.wait()
        wg_b[...] = wg_f[slot].astype(BF16)
        wu_b[...] = wu_f[slot].astype(BF16)
        wd_b[...] = wd_f[slot].astype(BF16)

        @pl.when(next_e_ref[e] >= 0)
        def _prefetch_next():
            for cp in weight_copies(next_e_ref[e], 1 - slot):
                cp.start()

    @pl.when(active)
    def _():
        xb = _unpack_halves(_load_row_chunks(xs_ref, (), tm)).astype(BF16)
        g = jnp.dot(xb, wg_b[...], preferred_element_type=F32)
        u = jnp.dot(xb, wu_b[...], preferred_element_type=F32)
        hid = (g * _sigmoid(g) * u).astype(BF16)
        y = jnp.dot(hid, wd_b[...], preferred_element_type=F32)
        _store_row_chunks(y_ref, (), _pack_halves(y), tm)


def _experts(blk_e, n_used, next_e, run_idx, xs, w_gate, w_up, w_down, *, tm):
    w = xs.shape[1]
    n_rows = xs.shape[0] // ROW_CHUNKS
    _, d, de = w_gate.shape
    row_map = lambda i, blk_e, n_used, next_e, run_idx: (jnp.minimum(i, n_used[0] - 1), 0)
    hbm = pl.BlockSpec(memory_space=pl.ANY)
    return pl.pallas_call(
        _expert_kernel,
        grid_spec=pltpu.PrefetchScalarGridSpec(
            num_scalar_prefetch=4,
            grid=(n_rows // tm,),
            in_specs=[pl.BlockSpec((tm * ROW_CHUNKS, w), row_map), hbm, hbm, hbm],
            out_specs=pl.BlockSpec((tm * ROW_CHUNKS, w), lambda i, blk_e, n_used, next_e, run_idx: (i, 0)),
            scratch_shapes=[pltpu.VMEM((2, d, de), F32), pltpu.VMEM((2, d, de), F32), pltpu.VMEM((2, de, d), F32),
                            pltpu.VMEM((d, de), BF16), pltpu.VMEM((d, de), BF16), pltpu.VMEM((de, d), BF16),
                            pltpu.SemaphoreType.DMA((2, 3))]),
        out_shape=jax.ShapeDtypeStruct((n_rows * ROW_CHUNKS, w), U32),
        compiler_params=_cparams(1),
        name="experts",
    )(blk_e, n_used, next_e, run_idx, xs, w_gate, w_up, w_down)


def _combine_kernel(dest_ref, x1_ref, mf_ref, g_ref, b_ref, yp_ref, o_ref, buf, sems, *, tm, n_tiles):
    i = pl.program_id(0)

    def issue(tile, slot):
        base = tile * tm

        def body(g, carry):
            for u in range(ISSUE_UNROLL):
                r = g * ISSUE_UNROLL + u
                for c in range(TOP_K):
                    _row_copy(yp_ref, dest_ref[TOP_K * (base + r) + c], buf.at[slot, c], r,
                              sems.at[slot]).start(priority=c)
            return carry

        lax.fori_loop(0, tm // ISSUE_UNROLL, body, 0)

    @pl.when(i == 0)
    def _first():
        issue(0, 0)

    @pl.when(i + 1 < n_tiles)
    def _next():
        issue(i + 1, (i + 1) % 2)

    slot = i % 2
    for c in range(TOP_K):
        pltpu.make_async_copy(yp_ref.at[pl.ds(0, tm * ROW_CHUNKS), :], buf.at[slot, c], sems.at[slot]).wait()

    gates = mf_ref[...]
    ffn = (gates[:, 0:1] * _unpack_halves(_load_row_chunks(buf, (slot, 0), tm))
           + gates[:, 1:2] * _unpack_halves(_load_row_chunks(buf, (slot, 1), tm)))
    o_ref[...] = _layer_norm(DN_ALPHA * x1_ref[...] + ffn, g_ref[...], b_ref[...])


def _combine(dest_flat, x1, mf, ln_g, ln_b, yp, *, tm):
    t, d = x1.shape
    n_tiles = t // tm
    assert tm % ISSUE_UNROLL == 0
    vec = pl.BlockSpec((1, d), lambda i, dest: (0, 0))
    return pl.pallas_call(
        functools.partial(_combine_kernel, tm=tm, n_tiles=n_tiles),
        grid_spec=pltpu.PrefetchScalarGridSpec(
            num_scalar_prefetch=1,
            grid=(n_tiles,),
            in_specs=[pl.BlockSpec((tm, d), lambda i, dest: (i, 0)),
                      pl.BlockSpec((tm, LANES), lambda i, dest: (i, 0)),
                      vec, vec,
                      pl.BlockSpec(memory_space=pl.ANY)],
            out_specs=pl.BlockSpec((tm, d), lambda i, dest: (i, 0)),
            scratch_shapes=[pltpu.VMEM((2, TOP_K, tm * ROW_CHUNKS, LANES), U32),
                            pltpu.SemaphoreType.DMA((2,))]),
        out_shape=jax.ShapeDtypeStruct((t, d), F32),
        compiler_params=_cparams(1),
        name="combine",
    )(dest_flat, x1, mf, ln_g.reshape(1, d), ln_b.reshape(1, d), yp)


def _tile(n, pref):
    return pref if n % pref == 0 else n


def kernel(x, w_in, b_in, diff_lambda, head_norm_g, w_o_attn, rel_bias, conv_w, conv_b, conv_ln_g,
           conv_ln_b, w_conv_out, b_conv_out, w_out, ln1_g, ln1_b, router_g_w, router_g_b,
           router_e_w, router_e_b, expert_w_gate, expert_w_up, expert_w_down, ln2_g, ln2_b):
    bsz, s, d = x.shape
    t = bsz * s
    width = N_HEADS * 2 * HEAD_DIM
    assert w_in.shape[0] == DEPTH and w_in.shape[2] == 7 * width and width == d
    assert conv_w.shape[2] == width and TOP_K == 2
    tm_moe = _tile(t, 512)

    for li in range(DEPTH):
        lam_init = 0.8 - 0.6 * math.exp(-0.3 * li)
        x2 = x.reshape(t, d)
        q, k, v, ga, gb, hc = _proj_conv(
            x2, w_in[li].astype(BF16), b_in[li].reshape(1, -1), conv_w[li], conv_b[li], conv_ln_g[li],
            conv_ln_b[li], width=width, tm=_tile(s, 512), chunk=_tile(width, 512),
            q_scale=HEAD_DIM ** -0.5 * LOG2E, ts=_tile(s, 256), rc=16, seq=s)
        o_n = _attention(q.reshape(bsz, s, width), k.reshape(bsz, s, width), v.reshape(bsz, s, width),
                         rel_bias, diff_lambda[li], head_norm_g[li], tk=256 if s % 512 == 0 else 128,
                         lam_init=lam_init)

        n_r = N_GROUPS + N_EXPERTS
        w_r = jnp.pad(jnp.concatenate([router_g_w[li], router_e_w[li]], axis=1),
                      ((0, 0), (0, LANES - n_r))).astype(BF16)
        b_r = jnp.pad(jnp.concatenate([router_g_b[li], router_e_b[li]]), (0, LANES - n_r)).reshape(1, LANES)
        x1, x1p, mi, mf, cnt = _mix(
            o_n.reshape(t, width), hc.reshape(t, width), ga, gb, x2,
            w_o_attn[li].astype(BF16), w_conv_out[li].astype(BF16), b_conv_out[li],
            w_out[li].astype(BF16), ln1_g[li], ln1_b[li], w_r, b_r, tm=_tile(t, 512))

        counts = cnt[0, :N_EXPERTS].astype(I32)
        padded = (counts + tm_moe - 1) // tm_moe * tm_moe
        pad_ends = jnp.cumsum(padded)
        pad_starts = pad_ends - padded
        chosen = mi[:, 0:TOP_K, None] == jnp.arange(N_EXPERTS, dtype=I32)[None, None, :]
        dest_flat = (jnp.sum(jnp.where(chosen, pad_starts[None, None, :], 0), axis=-1)
                     + mi[:, TOP_K:2 * TOP_K]).reshape(t * TOP_K)
        n_rows = t * TOP_K + N_EXPERTS * tm_moe
        n_blocks = n_rows // tm_moe
        blk_start = jnp.arange(n_blocks, dtype=I32) * tm_moe
        blk_e = jnp.minimum(jnp.sum(blk_start[:, None] >= pad_ends[None, :], axis=1),
                            N_EXPERTS - 1).astype(I32)
        n_used = (pad_ends[-1:] // tm_moe).astype(I32)
        has_rows = padded > 0
        later = lax.cummin(jnp.where(has_rows, jnp.arange(N_EXPERTS, dtype=I32), N_EXPERTS), reverse=True)
        next_e = jnp.concatenate([later[1:], jnp.full((1,), N_EXPERTS, I32)])
        next_e = jnp.where(next_e < N_EXPERTS, next_e, -1).astype(I32)
        run_idx = (jnp.cumsum(has_rows.astype(I32)) - 1).astype(I32)

        xs = _dispatch(dest_flat, pad_ends, padded, n_used, x1p, n_rows, tm=_tile(t, 512), blk_rows=tm_moe)
        yp = _experts(blk_e, n_used, next_e, run_idx, xs, expert_w_gate[li], expert_w_up[li], expert_w_down[li],
                      tm=tm_moe)
        x = _combine(dest_flat, x1, mf, ln2_g[li], ln2_b[li], yp, tm=_tile(t, 256)).reshape(bsz, s, d)
    return x
```
